```python
import jax, jax.numpy as jnp
from jax import lax
import numpy as np

D_MODEL = 4096
BATCH = 2
SEQ = 8192
DEPTH = 2

GRID_W = 64
CTX_LEN = 256
EPS = 1e-6
N_MOD = 6
F32 = jnp.float32
NEG_INF = -1e30

HEAD_DIM = 128
NA_HEADS = 8
NA_WIN_H = 8
NA_WIN_W = 16
NA_COL_BLOCK = 16
NA_COL_BAND = 32
NA_WIDTH = NA_HEADS * HEAD_DIM

MLA_HEADS = 8
MLA_Q_RANK = 1024
MLA_KV_RANK = 512
MLA_NOPE = 128
MLA_ROPE = 64
MLA_V = 128
MLA_WIDTH = MLA_HEADS * MLA_V
ROPE_BASE = 10000.0

SC_WIDTH = 1024
SC_TAPS = 3

N_BRANCH = 3
ATT_Q_BLOCK = 128

PEER_HEADS = 8
PEER_NKEYS = 128
PEER_EXPERTS = PEER_NKEYS * PEER_NKEYS
PEER_KEY_DIM = 256
PEER_TOPK = 16
PEER_TOKEN_BLOCK = 64

IN_SPLIT_SIZES = (NA_WIDTH, NA_WIDTH, NA_WIDTH,
                  MLA_Q_RANK, MLA_KV_RANK, MLA_ROPE,
                  SC_WIDTH, SC_WIDTH, SC_WIDTH,
                  N_BRANCH * D_MODEL)
IN_COLS = int(sum(IN_SPLIT_SIZES))
IN_SPLIT_POINTS = tuple(int(v) for v in np.cumsum(IN_SPLIT_SIZES)[:-1])

kernel_name = "hybrid_flow_backbone_block"


def rms_norm(x, g):
    xf = x.astype(F32)
    y = xf * lax.rsqrt(jnp.mean(jnp.square(xf), axis=-1, keepdims=True) + EPS)
    return (y * g.astype(F32)).astype(x.dtype)


def modulate(h, shift, scale):
    return h * (1 + scale) + shift


def axial_rope_angles(n_tokens):
    pos = jnp.arange(n_tokens, dtype=jnp.int32)
    n_freq = MLA_ROPE // 4
    inv_freq = jnp.power(ROPE_BASE, -jnp.arange(n_freq, dtype=F32) / n_freq)
    row = (pos // GRID_W).astype(F32)
    col = (pos % GRID_W).astype(F32)
    ang = jnp.concatenate([row[:, None] * inv_freq, col[:, None] * inv_freq], axis=-1)
    return jnp.cos(ang), jnp.sin(ang)


def apply_rope(x, cos, sin):
    half = x.shape[-1] // 2
    x1 = x[..., :half].astype(F32)
    x2 = x[..., half:].astype(F32)
    return jnp.concatenate([x1 * cos - x2 * sin, x1 * sin + x2 * cos], axis=-1).astype(x.dtype)


def blocked_attention(q, k, v):
    b, sq, h, dk = q.shape
    nb = sq // ATT_Q_BLOCK
    scale = dk ** -0.5
    q_blocks = q.reshape(b, nb, ATT_Q_BLOCK, h, dk).swapaxes(0, 1)

    def one_block(qb):
        s = jnp.einsum('bqhd,bkhd->bhqk', qb, k, preferred_element_type=F32) * scale
        p = jax.nn.softmax(s, axis=-1).astype(v.dtype)
        return jnp.einsum('bhqk,bkhd->bqhd', p, v)

    o = lax.map(one_block, q_blocks)
    return o.swapaxes(0, 1).reshape(b, sq, h, v.shape[-1])


def na_column_tables():
    n_cb = GRID_W // NA_COL_BLOCK
    q_cols = np.arange(GRID_W).reshape(n_cb, NA_COL_BLOCK)
    band_start = np.clip(np.arange(n_cb) * NA_COL_BLOCK - NA_WIN_W // 2, 0, GRID_W - NA_COL_BAND)
    band_cols = band_start[:, None] + np.arange(NA_COL_BAND)[None, :]
    win_start = np.clip(q_cols - NA_WIN_W // 2, 0, GRID_W - NA_WIN_W)
    kcol = band_cols[:, None, :]
    in_win = (kcol >= win_start[..., None]) & (kcol < win_start[..., None] + NA_WIN_W)
    dc_idx = np.clip(kcol - q_cols[..., None] + NA_WIN_W - 1, 0, 2 * NA_WIN_W - 2)
    return band_cols, in_win, dc_idx


def neighbourhood_attention(ql, kl, vl, kc, vc, rpb):
    b, s, h, dh = ql.shape
    rows = s // GRID_W
    wh = min(NA_WIN_H, rows)
    n_cb = GRID_W // NA_COL_BLOCK
    scale = dh ** -0.5
    band_cols, in_win, dc_idx = na_column_tables()
    col_bias = jnp.where(in_win, rpb[:, :, dc_idx].astype(F32), NEG_INF)
    qg = ql.reshape(b, rows, n_cb, NA_COL_BLOCK, h, dh)
    kg = kl.reshape(b, rows, GRID_W, h, dh)
    vg = vl.reshape(b, rows, GRID_W, h, dh)
    n_loc = wh * NA_COL_BAND

    def one_row(r):
        start = jnp.clip(r - NA_WIN_H // 2, 0, rows - wh)
        k_rows = lax.dynamic_slice_in_dim(kg, start, wh, axis=1)[:, :, band_cols]
        v_rows = lax.dynamic_slice_in_dim(vg, start, wh, axis=1)[:, :, band_cols]
        q_r = lax.dynamic_index_in_dim(qg, r, axis=1, keepdims=False)
        dr_idx = start + jnp.arange(wh) - r + NA_WIN_H - 1
        bias = jnp.take(col_bias, dr_idx, axis=1).transpose(0, 2, 3, 1, 4)
        s_loc = jnp.einsum('bnqhd,bwnkhd->bhnqwk', q_r, k_rows, preferred_element_type=F32) * scale + bias
        s_ctx = jnp.einsum('bnqhd,bchd->bhnqc', q_r, kc, preferred_element_type=F32) * scale
        s_all = jnp.concatenate([s_loc.reshape(b, h, n_cb, NA_COL_BLOCK, n_loc), s_ctx], axis=-1)
        p = jax.nn.softmax(s_all, axis=-1).astype(vl.dtype)
        p_loc = p[..., :n_loc].reshape(b, h, n_cb, NA_COL_BLOCK, wh, NA_COL_BAND)
        o = (jnp.einsum('bhnqwk,bwnkhd->bnqhd', p_loc, v_rows)
             + jnp.einsum('bhnqc,bchd->bnqhd', p[..., n_loc:], vc))
        return o.reshape(b, GRID_W, h, dh)

    o = lax.map(one_row, jnp.arange(rows))
    return o.swapaxes(0, 1).reshape(b, s, h, dh)


def mla_qkv(cq, ckv, k_rope, q_norm, kv_norm, w_uq, w_ukv, rope):
    b, t, _ = cq.shape
    q = (rms_norm(cq, q_norm) @ w_uq).reshape(b, t, MLA_HEADS, MLA_NOPE + MLA_ROPE)
    kv = (rms_norm(ckv, kv_norm) @ w_ukv).reshape(b, t, MLA_HEADS, MLA_NOPE + MLA_V)
    q_nope, q_rope = q[..., :MLA_NOPE], q[..., MLA_NOPE:]
    k_nope, v = kv[..., :MLA_NOPE], kv[..., MLA_NOPE:]
    if rope is not None:
        cos, sin = rope
        q_rope = apply_rope(q_rope, cos[:, None], sin[:, None])
        k_rope = apply_rope(k_rope, cos, sin)
    k_rope = jnp.broadcast_to(k_rope[:, :, None, :], (b, t, MLA_HEADS, MLA_ROPE))
    return (jnp.concatenate([q_nope, q_rope], axis=-1), jnp.concatenate([k_nope, k_rope], axis=-1), v)


def gated_short_conv(gate_b, gate_c, xin, conv_w):
    u = gate_c * xin
    y = lax.conv_general_dilated(u, conv_w[:, None, :], window_strides=(1,),
                                 padding=((SC_TAPS // 2, SC_TAPS // 2),),
                                 dimension_numbers=('NWC', 'WIO', 'NWC'),
                                 feature_group_count=SC_WIDTH)
    return gate_b * y


def merge_branches(ya, yb, yc, gate_pre, w_br_a, w_br_b, w_br_c, w_out):
    b, t = gate_pre.shape[:2]
    g = jax.nn.sigmoid(gate_pre.astype(F32)).astype(gate_pre.dtype).reshape(b, t, N_BRANCH, D_MODEL)
    m = (g[:, :, 0] * (ya.reshape(b, t, -1) @ w_br_a)
         + g[:, :, 1] * (yb.reshape(b, t, -1) @ w_br_b)
         + g[:, :, 2] * (yc @ w_br_c))
    return m @ w_out


def token_mixer(hc, hl, w_in, rpb, q_norm, kv_norm, w_uq, w_ukv, conv_w,
                w_br_a, w_br_b, w_br_c, w_out, ctx_out):
    pc = jnp.split(hc @ w_in, IN_SPLIT_POINTS, axis=-1)
    pl = jnp.split(hl @ w_in, IN_SPLIT_POINTS, axis=-1)
    s = hl.shape[1]

    def heads(t):
        return t.reshape(t.shape[0], t.shape[1], NA_HEADS, HEAD_DIM)

    qa_c, ka_c, va_c = heads(pc[0]), heads(pc[1]), heads(pc[2])
    qa_l, ka_l, va_l = heads(pl[0]), heads(pl[1]), heads(pl[2])
    ya_l = neighbourhood_attention(qa_l, ka_l, va_l, ka_c, va_c, rpb)
    qb_c, kb_c, vb_c = mla_qkv(pc[3], pc[4], pc[5], q_norm, kv_norm, w_uq, w_ukv, None)
    qb_l, kb_l, vb_l = mla_qkv(pl[3], pl[4], pl[5], q_norm, kv_norm, w_uq, w_ukv, axial_rope_angles(s))
    yb_l = blocked_attention(qb_l, jnp.concatenate([kb_c, kb_l], axis=1), jnp.concatenate([vb_c, vb_l], axis=1))
    yc_l = gated_short_conv(pl[6], pl[7], pl[8], conv_w)
    out_l = merge_branches(ya_l, yb_l, yc_l, pl[9], w_br_a, w_br_b, w_br_c, w_out)
    out_c = None
    if ctx_out:
        ya_c = blocked_attention(qa_c, ka_c, va_c)
        yb_c = blocked_attention(qb_c, kb_c, vb_c)
        yc_c = gated_short_conv(pc[6], pc[7], pc[8], conv_w)
        out_c = merge_branches(ya_c, yb_c, yc_c, pc[9], w_br_a, w_br_b, w_br_c, w_out)
    return out_c, out_l


def peer_ffn(h, w_q, sub_keys, u, v):
    b, t, d = h.shape
    n = b * t
    hf = h.reshape(n, d)
    q = (hf @ w_q).reshape(n, PEER_HEADS, 2, PEER_KEY_DIM // 2)
    s = jnp.einsum('nhpd,hpkd->nhpk', q, sub_keys, preferred_element_type=F32)
    s1, i1 = lax.top_k(s[:, :, 0], PEER_TOPK)
    s2, i2 = lax.top_k(s[:, :, 1], PEER_TOPK)
    cand_s = (s1[..., :, None] + s2[..., None, :]).reshape(n, PEER_HEADS, PEER_TOPK * PEER_TOPK)
    cand_i = (i1[..., :, None] * PEER_NKEYS + i2[..., None, :]).reshape(n, PEER_HEADS, PEER_TOPK * PEER_TOPK)
    top_s, top_pos = lax.top_k(cand_s, PEER_TOPK)
    experts = jnp.take_along_axis(cand_i, top_pos, axis=-1).reshape(n, PEER_HEADS * PEER_TOPK)
    gates = jax.nn.softmax(top_s, axis=-1).reshape(n, PEER_HEADS * PEER_TOPK).astype(h.dtype)
    nb = n // PEER_TOKEN_BLOCK

    def one_block(args):
        hb, eb, gb = args
        a = jax.nn.gelu(jnp.einsum('nd,ned->ne', hb, u[eb]), approximate=False)
        return jnp.einsum('ne,ned->nd', a * gb, v[eb])

    out = lax.map(one_block, (hf.reshape(nb, PEER_TOKEN_BLOCK, d),
                              experts.reshape(nb, PEER_TOKEN_BLOCK, -1),
                              gates.reshape(nb, PEER_TOKEN_BLOCK, -1)))
    return out.reshape(b, t, d)


def setup_inputs(seed: int = 0) -> dict:
    key = jax.random.key(seed)
    ks = jax.random.split(key, 24)
    L, D = DEPTH, D_MODEL

    def nrm(k, shape, scale):
        return jax.random.normal(k, shape, F32) * scale

    def gain(k, shape):
        return 1.0 + 0.05 * jax.random.normal(k, shape, F32)

    return {
        "x": nrm(ks[0], (BATCH, SEQ, D), 1.0),
        "c": nrm(ks[1], (BATCH, D), 1.0),
        "ctx": nrm(ks[2], (BATCH, CTX_LEN, D), 1.0),
        "c_ctx": nrm(ks[3], (D,), 1.0),
        "w_ada": nrm(ks[4], (L, D, N_MOD * D), 0.5 * D ** -0.5),
        "b_ada": nrm(ks[5], (L, N_MOD * D), 0.02),
        "norm_mix": gain(ks[6], (L, D)),
        "norm_ffn": gain(ks[7], (L, D)),
        "w_in": nrm(ks[8], (L, D, IN_COLS), D ** -0.5),
        "na_rpb": nrm(ks[9], (L, NA_HEADS, 2 * NA_WIN_H - 1, 2 * NA_WIN_W - 1), 0.5),
        "mla_q_norm": gain(ks[10], (L, MLA_Q_RANK)),
        "mla_kv_norm": gain(ks[11], (L, MLA_KV_RANK)),
        "mla_w_uq": nrm(ks[12], (L, MLA_Q_RANK, MLA_HEADS * (MLA_NOPE + MLA_ROPE)), MLA_Q_RANK ** -0.5),
        "mla_w_ukv": nrm(ks[13], (L, MLA_KV_RANK, MLA_HEADS * (MLA_NOPE + MLA_V)), MLA_KV_RANK ** -0.5),
        "conv_w": nrm(ks[14], (L, SC_TAPS, SC_WIDTH), SC_TAPS ** -0.5),
        "w_branch_a": nrm(ks[15], (L, NA_WIDTH, D), NA_WIDTH ** -0.5),
        "w_branch_b": nrm(ks[16], (L, MLA_WIDTH, D), MLA_WIDTH ** -0.5),
        "w_branch_c": nrm(ks[17], (L, SC_WIDTH, D), SC_WIDTH ** -0.5),
        "w_out": nrm(ks[18], (L, D, D), D ** -0.5),
        "peer_w_q": nrm(ks[19], (L, D, PEER_HEADS * PEER_KEY_DIM), D ** -0.5),
        "peer_sub_keys": nrm(ks[20], (L, PEER_HEADS, 2, PEER_NKEYS, PEER_KEY_DIM // 2), (PEER_KEY_DIM // 2) ** -0.5),
        "peer_u": nrm(ks[21], (L, PEER_EXPERTS, D), D ** -0.5),
        "peer_v": nrm(ks[22], (L, PEER_EXPERTS, D), 1.0),
        "final_norm": gain(ks[23], (D,)),
    }


def reference(x, c, ctx, c_ctx, w_ada, b_ada, norm_mix, norm_ffn, w_in, na_rpb, mla_q_norm, mla_kv_norm,
              mla_w_uq, mla_w_ukv, conv_w, w_branch_a, w_branch_b, w_branch_c, w_out,
              peer_w_q, peer_sub_keys, peer_u, peer_v, final_norm):
    silu_c = jax.nn.silu(c)[:, None, :]
    silu_cc = jax.nn.silu(c_ctx)
    xl, xc = x, ctx
    for l in range(DEPTH):
        last = l == DEPTH - 1
        mod_l = jnp.split(silu_c @ w_ada[l] + b_ada[l], N_MOD, axis=-1)
        mod_c = jnp.split(silu_cc @ w_ada[l] + b_ada[l], N_MOD, axis=-1)
        hl = modulate(rms_norm(xl, norm_mix[l]), mod_l[0], mod_l[1])
        hc = modulate(rms_norm(xc, norm_mix[l]), mod_c[0], mod_c[1])
        out_c, out_l = token_mixer(hc, hl, w_in[l], na_rpb[l], mla_q_norm[l], mla_kv_norm[l],
                                   mla_w_uq[l], mla_w_ukv[l], conv_w[l], w_branch_a[l], w_branch_b[l],
                                   w_branch_c[l], w_out[l], not last)
        xl = xl + mod_l[2] * out_l
        hl = modulate(rms_norm(xl, norm_ffn[l]), mod_l[3], mod_l[4])
        xl = xl + mod_l[5] * peer_ffn(hl, peer_w_q[l], peer_sub_keys[l], peer_u[l], peer_v[l])
        if not last:
            xc = xc + mod_c[2] * out_c
            hc = modulate(rms_norm(xc, norm_ffn[l]), mod_c[3], mod_c[4])
            xc = xc + mod_c[5] * peer_ffn(hc, peer_w_q[l], peer_sub_keys[l], peer_u[l], peer_v[l])
    return rms_norm(xl, final_norm)
```

```python
import functools

import numpy as np
import jax
import jax.numpy as jnp
from jax import lax
from jax.experimental import pallas as pl
from jax.experimental.pallas import tpu as pltpu

F32 = jnp.float32
MXU_DTYPE = jnp.bfloat16

GRID_W = 64
EPS = 1e-6
N_MOD = 6
NEG_INF = -1e30
HEAD_DIM = 128
NA_HEADS = 8
NA_WIN_H = 8
NA_WIN_W = 16
MLA_HEADS = 8
MLA_NOPE = 128
MLA_ROPE = 64
MLA_V = 128
ROPE_BASE = 10000.0
SC_TAPS = 3
N_BRANCH = 3
PEER_HEADS = 8
PEER_NKEYS = 128
PEER_KEY_DIM = 256
PEER_TOPK = 16

V7X_VMEM_BYTES = 64 * 1024 * 1024
LANES = 128
SUBLANES = 8

ROW_BLOCK = 512
MLA_HEAD_PAD = 256
NA_Q_ROWS = 8
NA_K_ROWS = 16
NA_K_CHUNK_ROWS = 4


def _cparams(n_axes, vmem_bytes):
    limit = min(int(vmem_bytes) + (4 << 20), V7X_VMEM_BYTES - (6 << 20))
    return pltpu.CompilerParams(dimension_semantics=("arbitrary",) * n_axes,
                                vmem_limit_bytes=limit)


def _nbytes(shape, dtype):
    return int(np.prod(shape)) * jnp.dtype(dtype).itemsize


def _tile(n, prefs):
    return next(t for t in prefs if n % t == 0)


def _mod_row(row0, nl, seq, n_batch):
    return jnp.where(row0 >= nl, n_batch, row0 // seq)


def _ada_kernel(c_ref, w_ref, b_ref, o_ref):
    c = c_ref[...]
    s = (c * jax.nn.sigmoid(c)).astype(MXU_DTYPE)
    acc = jnp.dot(s, w_ref[...].astype(MXU_DTYPE), preferred_element_type=F32)
    o_ref[...] = acc + b_ref[...]


def _ada_table(cmat, w_ada, b_ada):
    rows, d = cmat.shape
    n = w_ada.shape[1]
    tn = 512
    vm = 2 * (_nbytes((d, tn), F32) + _nbytes((rows, tn), F32) * 2) + _nbytes((rows, d), F32) * 2 \
        + _nbytes((d, tn), MXU_DTYPE)
    return pl.pallas_call(
        _ada_kernel, name="ada_table",
        out_shape=jax.ShapeDtypeStruct((rows, n), F32),
        grid=(n // tn,),
        in_specs=[pl.BlockSpec((rows, d), lambda j: (0, 0)),
                  pl.BlockSpec((d, tn), lambda j: (0, j)),
                  pl.BlockSpec((1, tn), lambda j: (0, j))],
        out_specs=pl.BlockSpec((rows, tn), lambda j: (0, j)),
        compiler_params=_cparams(1, vm),
    )(cmat, w_ada, b_ada.reshape(1, n))


def _normmod_kernel(x_ref, g_ref, sh_ref, sc_ref, o_ref):
    x = x_ref[...]
    y = x * lax.rsqrt(jnp.mean(x * x, axis=-1, keepdims=True) + EPS)
    o_ref[...] = ((y * g_ref[...]) * (1.0 + sc_ref[0]) + sh_ref[0]).astype(o_ref.dtype)


def _normmod(x, gain, modtab, m_shift, *, nl, seq, n_batch, skip):
    n, d = x.shape
    tb = 256
    off = skip * (ROW_BLOCK // tb)

    def mrow(i, m):
        return (_mod_row(i * tb, nl, seq, n_batch) * N_MOD + m, 0, 0)

    vm = 2 * (_nbytes((tb, d), F32) + _nbytes((tb, d), MXU_DTYPE)) + 3 * _nbytes((tb, d), F32)
    return pl.pallas_call(
        _normmod_kernel, name="normmod",
        out_shape=jax.ShapeDtypeStruct((n, d), MXU_DTYPE),
        grid=(n // tb - off,),
        in_specs=[pl.BlockSpec((tb, d), lambda i: (i, 0)),
                  pl.BlockSpec((1, d), lambda i: (0, 0)),
                  pl.BlockSpec((1, 1, d), lambda i: mrow(i, m_shift)),
                  pl.BlockSpec((1, 1, d), lambda i: mrow(i, m_shift + 1))],
        out_specs=pl.BlockSpec((tb, d), lambda i: (i, 0)),
        compiler_params=_cparams(1, vm),
    )(x, gain.reshape(1, d), modtab, modtab)


def _mm_kernel(x_ref, w_ref, o_ref):
    o_ref[...] = jnp.dot(x_ref[...], w_ref[...], preferred_element_type=F32).astype(o_ref.dtype)


def _matmul(x, w, *, tn, out_dtype, skip=0, name="matmul"):
    n, k = x.shape
    nn = w.shape[1]
    tm = ROW_BLOCK
    vm = 2 * (_nbytes((tm, k), x.dtype) + _nbytes((k, tn), w.dtype) + _nbytes((tm, tn), out_dtype)) \
        + _nbytes((tm, tn), F32)
    return pl.pallas_call(
        _mm_kernel, name=name,
        out_shape=jax.ShapeDtypeStruct((n, nn), out_dtype),
        grid=(n // tm - skip, nn // tn),
        in_specs=[pl.BlockSpec((tm, k), lambda i, j: (i, 0)),
                  pl.BlockSpec((k, tn), lambda i, j: (0, j))],
        out_specs=pl.BlockSpec((tm, tn), lambda i, j: (i, j)),
        compiler_params=_cparams(2, vm),
    )(x, w)


def _mm_resid_kernel(x_ref, w_ref, r_ref, g_ref, o_ref, acc_ref):
    k = pl.program_id(2)

    @pl.when(k == 0)
    def _():
        acc_ref[...] = jnp.zeros_like(acc_ref)

    acc_ref[...] += jnp.dot(x_ref[...], w_ref[...], preferred_element_type=F32)

    @pl.when(k == pl.num_programs(2) - 1)
    def _():
        o_ref[...] = r_ref[...] + g_ref[0] * acc_ref[...]


def _matmul_resid(x, w, resid, modtab, m_gate, *, tn, tk, nl, seq, n_batch, skip, name):
    n, kk = x.shape
    nn = w.shape[1]
    tm = ROW_BLOCK

    def mrow(i, j, k):
        return (_mod_row(i * tm, nl, seq, n_batch) * N_MOD + m_gate, 0, j)

    vm = 2 * (_nbytes((tm, tk), x.dtype) + _nbytes((tk, tn), w.dtype) + 2 * _nbytes((tm, tn), F32)) \
        + 2 * _nbytes((tm, tn), F32)
    return pl.pallas_call(
        _mm_resid_kernel, name=name,
        out_shape=jax.ShapeDtypeStruct((n, nn), F32),
        grid=(n // tm - skip, nn // tn, kk // tk),
        in_specs=[pl.BlockSpec((tm, tk), lambda i, j, k: (i, k)),
                  pl.BlockSpec((tk, tn), lambda i, j, k: (k, j)),
                  pl.BlockSpec((tm, tn), lambda i, j, k: (i, j)),
                  pl.BlockSpec((1, 1, tn), mrow)],
        out_specs=pl.BlockSpec((tm, tn), lambda i, j, k: (i, j)),
        scratch_shapes=[pltpu.VMEM((tm, tn), F32)],
        compiler_params=_cparams(3, vm),
    )(x, w, resid, modtab)


def _na_bias_tables(rows):
    n_blk = rows // NA_Q_ROWS
    rl = np.arange(NA_Q_ROWS)[:, None, None, None]
    c = np.arange(GRID_W)[None, :, None, None]
    krl = np.arange(NA_K_ROWS)[None, None, :, None]
    kc = np.arange(GRID_W)[None, None, None, :]
    cases, case_of_blk = [], []
    for i in range(n_blk):
        r0 = i * NA_Q_ROWS
        k0 = int(np.clip(r0 - NA_K_CHUNK_ROWS, 0, rows - NA_K_ROWS))
        r = r0 + rl
        kr = k0 + krl
        start = np.clip(r - NA_WIN_H // 2, 0, rows - NA_WIN_H)
        ws = np.clip(c - NA_WIN_W // 2, 0, GRID_W - NA_WIN_W)
        ok = (kr >= start) & (kr < start + NA_WIN_H) & (kc >= ws) & (kc < ws + NA_WIN_W)
        dr = np.clip(kr - r + NA_WIN_H - 1, 0, 2 * NA_WIN_H - 2)
        dc = np.clip(kc - c + NA_WIN_W - 1, 0, 2 * NA_WIN_W - 2)
        shape = (NA_Q_ROWS, GRID_W, NA_K_ROWS, GRID_W)
        key = (np.broadcast_to(ok, shape).tobytes(), np.broadcast_to(dr, shape).tobytes())
        for ci, (kk, _) in enumerate(cases):
            if kk == key:
                case_of_blk.append(ci)
                break
        else:
            case_of_blk.append(len(cases))
            nq, nk = NA_Q_ROWS * GRID_W, NA_K_ROWS * GRID_W
            cases.append((key, (np.broadcast_to(ok, shape).reshape(nq, nk),
                                np.broadcast_to(dr, shape).reshape(nq, nk),
                                np.broadcast_to(dc, shape).reshape(nq, nk))))
    ok = np.stack([t[1][0] for t in cases])
    dr = np.stack([t[1][1] for t in cases])
    dc = np.stack([t[1][2] for t in cases])
    return ok, dr, dc, np.asarray(case_of_blk, np.int32)


def _na_kernel(case_ref, q_ref, k0_ref, k1_ref, k2_ref, k3_ref, v0_ref, v1_ref, v2_ref, v3_ref,
               kc_ref, vc_ref, b_ref, o_ref):
    del case_ref
    scale = HEAD_DIM ** -0.5
    nt = (((1,), (1,)), ((), ()))
    q = q_ref[...]
    k_loc = jnp.concatenate([k0_ref[...], k1_ref[...], k2_ref[...], k3_ref[...]], axis=0)
    v_loc = jnp.concatenate([v0_ref[...], v1_ref[...], v2_ref[...], v3_ref[...]], axis=0)
    s_loc = lax.dot_general(q, k_loc, nt, preferred_element_type=F32) * scale + b_ref[0, 0]
    s_ctx = lax.dot_general(q, kc_ref[...], nt, preferred_element_type=F32) * scale
    m = jnp.maximum(jnp.max(s_loc, axis=-1, keepdims=True), jnp.max(s_ctx, axis=-1, keepdims=True))
    p_loc = jnp.exp(s_loc - m)
    p_ctx = jnp.exp(s_ctx - m)
    denom = jnp.sum(p_loc, axis=-1, keepdims=True) + jnp.sum(p_ctx, axis=-1, keepdims=True)
    o = (jnp.dot(p_loc.astype(MXU_DTYPE), v_loc, preferred_element_type=F32)
         + jnp.dot(p_ctx.astype(MXU_DTYPE), vc_ref[...], preferred_element_type=F32))
    o_ref[...] = (o / denom).astype(o_ref.dtype)


def _na_attention(proj, bias, case_of_blk, *, nl, seq, lc, n_batch, col_q, col_k, col_v):
    n = proj.shape[0]
    rows = seq // GRID_W
    tq = NA_Q_ROWS * GRID_W
    tkc = NA_K_CHUNK_ROWS * GRID_W
    n_blk = rows // NA_Q_ROWS
    n_chunk = NA_K_ROWS // NA_K_CHUNK_ROWS
    max_c0 = rows // NA_K_CHUNK_ROWS - n_chunk

    def qmap(h, b, i, case):
        return (b * (seq // tq) + i, col_q + h)

    def kvmap(col, j):
        def f(h, b, i, case):
            c0 = jnp.clip(i * (NA_Q_ROWS // NA_K_CHUNK_ROWS) - 1, 0, max_c0)
            return (b * (seq // tkc) + c0 + j, col + h)
        return f

    def ctxmap(col):
        return lambda h, b, i, case: (nl // lc + b, col + h)

    in_specs = [pl.BlockSpec((tq, HEAD_DIM), qmap)]
    in_specs += [pl.BlockSpec((tkc, HEAD_DIM), kvmap(col_k, j)) for j in range(n_chunk)]
    in_specs += [pl.BlockSpec((tkc, HEAD_DIM), kvmap(col_v, j)) for j in range(n_chunk)]
    in_specs += [pl.BlockSpec((lc, HEAD_DIM), ctxmap(col_k)), pl.BlockSpec((lc, HEAD_DIM), ctxmap(col_v))]
    in_specs += [pl.BlockSpec((1, 1, tq, NA_K_ROWS * GRID_W), lambda h, b, i, case: (case[i], h, 0, 0))]
    vm = 2 * (_nbytes((tq, NA_K_ROWS * GRID_W), F32) + 12 * _nbytes((tq, HEAD_DIM), MXU_DTYPE)) \
        + 6 * _nbytes((tq, NA_K_ROWS * GRID_W + lc), F32)
    grid_spec = pltpu.PrefetchScalarGridSpec(
        num_scalar_prefetch=1, grid=(NA_HEADS, n_batch, n_blk),
        in_specs=in_specs,
        out_specs=pl.BlockSpec((tq, HEAD_DIM), lambda h, b, i, case: (b * (seq // tq) + i, h)))
    return pl.pallas_call(
        _na_kernel, name="na_attention",
        out_shape=jax.ShapeDtypeStruct((n, NA_HEADS * HEAD_DIM), MXU_DTYPE),
        grid_spec=grid_spec,
        compiler_params=_cparams(3, vm),
    )(case_of_blk, *([proj] * (1 + 2 * n_chunk + 2)), bias)


def _ctx_attn_kernel(q_ref, k_ref, v_ref, prev_ref, o_ref, *, scale):
    del prev_ref
    s = lax.dot_general(q_ref[...], k_ref[...], (((1,), (1,)), ((), ())),
                        preferred_element_type=F32) * scale
    m = jnp.max(s, axis=-1, keepdims=True)
    p = jnp.exp(s - m)
    o = jnp.dot(p.astype(MXU_DTYPE), v_ref[...], preferred_element_type=F32)
    o_ref[...] = (o / jnp.sum(p, axis=-1, keepdims=True)).astype(o_ref.dtype)


def _ctx_attention(qa, ka, va, y, *, nl, lc, n_batch, n_heads, dk, dv, col_q, col_k, col_v, scale):
    vm = 2 * (2 * _nbytes((lc, dk), MXU_DTYPE) + 2 * _nbytes((lc, dv), MXU_DTYPE)) + 4 * _nbytes((lc, lc), F32)
    return pl.pallas_call(
        functools.partial(_ctx_attn_kernel, scale=scale), name="ctx_attention",
        out_shape=jax.ShapeDtypeStruct(y.shape, y.dtype),
        grid=(n_batch, n_heads),
        in_specs=[pl.BlockSpec((lc, dk), lambda b, h: (nl // lc + b, col_q + h)),
                  pl.BlockSpec((lc, dk), lambda b, h: (nl // lc + b, col_k + h)),
                  pl.BlockSpec((lc, dv), lambda b, h: (nl // lc + b, col_v + h)),
                  pl.BlockSpec(memory_space=pl.ANY)],
        out_specs=pl.BlockSpec((lc, dv), lambda b, h: (nl // lc + b, h)),
        input_output_aliases={3: 0},
        compiler_params=_cparams(2, vm),
    )(qa, ka, va, y)


def _mla_up_kernel(cq_ref, ckv_ref, kr_ref, qn_ref, kvn_ref, wq_ref, wk_ref, wv_ref, cos_ref, sin_ref,
                   q_ref, k_ref, v_ref):
    def rms(x, g):
        return (x * lax.rsqrt(jnp.mean(x * x, axis=-1, keepdims=True) + EPS) * g).astype(MXU_DTYPE)

    cos = cos_ref[...]
    sin = sin_ref[...]

    def rope(x):
        return x * cos + pltpu.roll(x, 2 * (MLA_ROPE // 2), axis=1) * sin

    cqn = rms(cq_ref[...].astype(F32), qn_ref[...])
    kvn = rms(ckv_ref[...].astype(F32), kvn_ref[...])
    q = jnp.dot(cqn, wq_ref[...], preferred_element_type=F32)
    kn = jnp.dot(kvn, wk_ref[...], preferred_element_type=F32)
    v_ref[...] = jnp.dot(kvn, wv_ref[...], preferred_element_type=F32).astype(v_ref.dtype)
    kr = rope(kr_ref[...].astype(F32)).astype(k_ref.dtype)
    for h in range(MLA_HEADS):
        c0 = h * MLA_HEAD_PAD
        q_ref[:, c0:c0 + MLA_NOPE] = q[:, c0:c0 + MLA_NOPE].astype(q_ref.dtype)
        q_ref[:, c0 + MLA_NOPE:c0 + MLA_HEAD_PAD] = rope(q[:, c0 + MLA_NOPE:c0 + MLA_HEAD_PAD]).astype(q_ref.dtype)
        k_ref[:, c0:c0 + MLA_NOPE] = kn[:, h * MLA_NOPE:(h + 1) * MLA_NOPE].astype(k_ref.dtype)
        k_ref[:, c0 + MLA_NOPE:c0 + MLA_HEAD_PAD] = kr


def _mla_up(proj, q_norm, kv_norm, wq, wk, wv, cos_t, sin_t, *, col_cq, col_ckv, col_kr):
    n = proj.shape[0]
    tm = 256
    rq, rkv = wq.shape[0], wk.shape[0]
    wq_cols = wq.shape[1]
    vm = 2 * (_nbytes(wq.shape, MXU_DTYPE) + _nbytes(wk.shape, MXU_DTYPE) + _nbytes(wv.shape, MXU_DTYPE)
              + _nbytes((tm, rq + rkv + LANES), MXU_DTYPE) + 2 * _nbytes((tm, LANES), F32)
              + _nbytes((tm, 2 * wq_cols + wv.shape[1]), MXU_DTYPE)) + 3 * _nbytes((tm, wq_cols), F32)
    return pl.pallas_call(
        _mla_up_kernel, name="mla_up",
        out_shape=(jax.ShapeDtypeStruct((n, wq_cols), MXU_DTYPE),
                   jax.ShapeDtypeStruct((n, wq_cols), MXU_DTYPE),
                   jax.ShapeDtypeStruct((n, wv.shape[1]), MXU_DTYPE)),
        grid=(n // tm,),
        in_specs=[pl.BlockSpec((tm, rq), lambda i: (i, col_cq)),
                  pl.BlockSpec((tm, rkv), lambda i: (i, col_ckv)),
                  pl.BlockSpec((tm, LANES), lambda i: (i, col_kr)),
                  pl.BlockSpec((1, rq), lambda i: (0, 0)),
                  pl.BlockSpec((1, rkv), lambda i: (0, 0)),
                  pl.BlockSpec(wq.shape, lambda i: (0, 0)),
                  pl.BlockSpec(wk.shape, lambda i: (0, 0)),
                  pl.BlockSpec(wv.shape, lambda i: (0, 0)),
                  pl.BlockSpec((tm, LANES), lambda i: (i, 0)),
                  pl.BlockSpec((tm, LANES), lambda i: (i, 0))],
        out_specs=(pl.BlockSpec((tm, wq_cols), lambda i: (i, 0)),
                   pl.BlockSpec((tm, wq_cols), lambda i: (i, 0)),
                   pl.BlockSpec((tm, wv.shape[1]), lambda i: (i, 0))),
        compiler_params=_cparams(1, vm),
    )(proj, proj, proj, q_norm.reshape(1, rq), kv_norm.reshape(1, rkv), wq, wk, wv, cos_t, sin_t)


def _flash_kernel(q_ref, kc_ref, kl_ref, vc_ref, vl_ref, o_ref, m_ref, l_ref, acc_ref, *, scale, tk):
    nt = (((1,), (1,)), ((), ()))
    q = q_ref[...]

    def chunk(k, v):
        s = lax.dot_general(q, k, nt, preferred_element_type=F32) * scale
        m_old = m_ref[...]
        m_new = jnp.maximum(m_old, jnp.max(s, axis=-1, keepdims=True))
        alpha = jnp.exp(m_old - m_new)
        p = jnp.exp(s - m_new)
        l_ref[...] = alpha * l_ref[...] + jnp.sum(p, axis=-1, keepdims=True)
        acc_ref[...] = alpha * acc_ref[...] + jnp.dot(p.astype(MXU_DTYPE), v, preferred_element_type=F32)
        m_ref[...] = m_new

    m_ref[...] = jnp.full_like(m_ref, -jnp.inf)
    l_ref[...] = jnp.zeros_like(l_ref)
    acc_ref[...] = jnp.zeros_like(acc_ref)
    chunk(kc_ref[...], vc_ref[...])

    def body(j, carry):
        r0 = pl.multiple_of(j * tk, tk)
        chunk(kl_ref[pl.ds(r0, tk), :], vl_ref[pl.ds(r0, tk), :])
        return carry

    lax.fori_loop(0, kl_ref.shape[0] // tk, body, 0)
    o_ref[...] = (acc_ref[...] / l_ref[...]).astype(o_ref.dtype)


def _mla_attention(qf, kf, vf, *, nl, seq, lc, n_batch):
    n = qf.shape[0]
    tq, tk = 512, 512
    dk, dv = MLA_HEAD_PAD, MLA_V
    scale = (MLA_NOPE + MLA_ROPE) ** -0.5
    vm = 2 * (_nbytes((tq, dk), MXU_DTYPE) + _nbytes((lc + seq, dk + dv), MXU_DTYPE) + _nbytes((tq, dv), MXU_DTYPE)) \
        + 6 * _nbytes((tq, tk), F32) + 2 * _nbytes((tq, dv), F32)
    return pl.pallas_call(
        functools.partial(_flash_kernel, scale=scale, tk=tk), name="mla_attention",
        out_shape=jax.ShapeDtypeStruct((n, MLA_HEADS * dv), MXU_DTYPE),
        grid=(n_batch, MLA_HEADS, seq // tq),
        in_specs=[pl.BlockSpec((tq, dk), lambda b, h, i: (b * (seq // tq) + i, h)),
                  pl.BlockSpec((lc, dk), lambda b, h, i: (nl // lc + b, h)),
                  pl.BlockSpec((seq, dk), lambda b, h, i: (b, h)),
                  pl.BlockSpec((lc, dv), lambda b, h, i: (nl // lc + b, h)),
                  pl.BlockSpec((seq, dv), lambda b, h, i: (b, h))],
        out_specs=pl.BlockSpec((tq, dv), lambda b, h, i: (b * (seq // tq) + i, h)),
        scratch_shapes=[pltpu.VMEM((tq, 1), F32), pltpu.VMEM((tq, 1), F32), pltpu.VMEM((tq, dv), F32)],
        compiler_params=_cparams(3, vm),
    )(qf, kf, kf, vf, vf)


def _merge_kernel(ya_ref, yb_ref, gb_ref, gc_ref, xin_ref, gcp_ref, xinp_ref, gcn_ref, xinn_ref, cw_ref,
                  wa_ref, wb_ref, wc_ref, g0_ref, g1_ref, g2_ref, o_ref, yc_ref, *, nl, seq, lc):
    i = pl.program_id(0)
    j = pl.program_id(1)
    tm = ya_ref.shape[0]

    @pl.when(j == 0)
    def _():
        u = gc_ref[...].astype(F32) * xin_ref[...].astype(F32)
        u_prev_halo = (gcp_ref[...].astype(F32) * xinp_ref[...].astype(F32))[SUBLANES - 1:SUBLANES, :]
        u_next_halo = (gcn_ref[...].astype(F32) * xinn_ref[...].astype(F32))[0:1, :]
        row = lax.broadcasted_iota(jnp.int32, (tm, 1), 0)
        g = i * tm + row
        pos = jnp.where(g < nl, g % seq, (g - nl) % lc)
        seq_len = jnp.where(g < nl, seq, lc)
        u_dn = jnp.where(row == 0, u_prev_halo, pltpu.roll(u, 1, axis=0))
        u_up = jnp.where(row == tm - 1, u_next_halo, pltpu.roll(u, tm - 1, axis=0))
        u_dn = jnp.where(pos == 0, 0.0, u_dn)
        u_up = jnp.where(pos == seq_len - 1, 0.0, u_up)
        cw = cw_ref[...]
        y = cw[0:1, :] * u_dn + cw[1:2, :] * u + cw[2:3, :] * u_up
        yc_ref[...] = (gb_ref[...].astype(F32) * y).astype(yc_ref.dtype)

    def branch(y, w_ref, g_ref):
        gate = jax.nn.sigmoid(g_ref[...].astype(F32))
        return gate * jnp.dot(y, w_ref[...], preferred_element_type=F32)

    m = (branch(ya_ref[...], wa_ref, g0_ref) + branch(yb_ref[...], wb_ref, g1_ref)
         + branch(yc_ref[...], wc_ref, g2_ref))
    o_ref[...] = m.astype(o_ref.dtype)


def _merge(ya, yb, proj, conv_w, wa, wb, wc, *, d, nl, seq, lc, skip, col_gb, col_gc, col_xin, col_gate):
    n = ya.shape[0]
    tm, tn = ROW_BLOCK, 1024 if d % 1024 == 0 else d
    w = ya.shape[1]
    hb = tm // SUBLANES
    last_halo = n // SUBLANES - 1
    n_col = d // tn

    def row(i, j):
        return (i, 0)

    def cur(col):
        return lambda i, j: (i, col)

    def prev(col):
        return lambda i, j: (jnp.maximum(i * hb - 1, 0), col)

    def nxt(col):
        return lambda i, j: (jnp.minimum((i + 1) * hb, last_halo), col)

    def gate(br):
        return lambda i, j: (i, col_gate + br * n_col + j)

    in_specs = [pl.BlockSpec((tm, w), row), pl.BlockSpec((tm, w), row),
                pl.BlockSpec((tm, w), cur(col_gb)), pl.BlockSpec((tm, w), cur(col_gc)),
                pl.BlockSpec((tm, w), cur(col_xin)),
                pl.BlockSpec((SUBLANES, w), prev(col_gc)), pl.BlockSpec((SUBLANES, w), prev(col_xin)),
                pl.BlockSpec((SUBLANES, w), nxt(col_gc)), pl.BlockSpec((SUBLANES, w), nxt(col_xin)),
                pl.BlockSpec((SC_TAPS, w), lambda i, j: (0, 0)),
                pl.BlockSpec((w, tn), lambda i, j: (0, j)), pl.BlockSpec((w, tn), lambda i, j: (0, j)),
                pl.BlockSpec((w, tn), lambda i, j: (0, j)),
                pl.BlockSpec((tm, tn), gate(0)), pl.BlockSpec((tm, tn), gate(1)), pl.BlockSpec((tm, tn), gate(2))]
    vm = 2 * (5 * _nbytes((tm, w), MXU_DTYPE) + 3 * _nbytes((w, tn), MXU_DTYPE) + 4 * _nbytes((tm, tn), MXU_DTYPE)) \
        + _nbytes((tm, w), MXU_DTYPE) + 6 * _nbytes((tm, max(w, tn)), F32)
    return pl.pallas_call(
        functools.partial(_merge_kernel, nl=nl, seq=seq, lc=lc), name="conv_merge",
        out_shape=jax.ShapeDtypeStruct((n, d), MXU_DTYPE),
        grid=(n // tm - skip, n_col),
        in_specs=in_specs,
        out_specs=pl.BlockSpec((tm, tn), lambda i, j: (i, j)),
        scratch_shapes=[pltpu.VMEM((tm, w), MXU_DTYPE)],
        compiler_params=_cparams(2, vm),
    )(ya, yb, proj, proj, proj, proj, proj, proj, proj, conv_w, wa, wb, wc, proj, proj, proj)


def _topk_cols(s, k):
    r = s.shape[0]
    iota = lax.broadcasted_iota(jnp.int32, s.shape, 0)
    vals, idxs = [], []
    for _ in range(k):
        m = jnp.max(s, axis=0, keepdims=True)
        idx = jnp.min(jnp.where(s == m, iota, r), axis=0, keepdims=True)
        vals.append(m)
        idxs.append(idx)
        s = jnp.where(iota == idx, -jnp.inf, s)
    return jnp.concatenate(vals, axis=0), jnp.concatenate(idxs, axis=0)


def _take_rows(table, idx):
    iota = lax.broadcasted_iota(jnp.int32, table.shape, 0)
    rows = [jnp.sum(jnp.where(iota == idx[j:j + 1, :], table, 0), axis=0, keepdims=True)
            for j in range(idx.shape[0])]
    return jnp.concatenate(rows, axis=0)


def _peer_topk_kernel(q_ref, keys_ref, g_ref, i1_ref, i2_ref):
    nt = (((1,), (1,)), ((), ()))
    half = PEER_KEY_DIM // 2
    q = q_ref[...]
    s1 = lax.dot_general(keys_ref[0, :PEER_NKEYS, :], q[:, :half], nt, preferred_element_type=F32)
    s2 = lax.dot_general(keys_ref[0, PEER_NKEYS:, :], q[:, half:], nt, preferred_element_type=F32)
    v1, j1 = _topk_cols(s1, PEER_TOPK)
    v2, j2 = _topk_cols(s2, PEER_TOPK)
    cand = jnp.concatenate([v1[a:a + 1, :] + v2 for a in range(PEER_TOPK)], axis=0)
    top_s, pos = _topk_cols(cand, PEER_TOPK)
    i1_ref[...] = _take_rows(j1, jnp.right_shift(pos, PEER_TOPK.bit_length() - 1))
    i2_ref[...] = _take_rows(j2, jnp.bitwise_and(pos, PEER_TOPK - 1))
    e = jnp.exp(top_s - top_s[0:1, :])
    g_ref[...] = e / jnp.sum(e, axis=0, keepdims=True)


def _peer_topk(q, keys, *, skip):
    n = q.shape[0]
    t = 256
    off = skip * (ROW_BLOCK // t)
    rows = PEER_HEADS * PEER_TOPK
    out_spec = pl.BlockSpec((PEER_TOPK, t), lambda i, h: (h, i))
    vm = 2 * (_nbytes((t, PEER_KEY_DIM), MXU_DTYPE) + _nbytes(keys.shape[1:], MXU_DTYPE)) \
        + 8 * _nbytes((PEER_TOPK * PEER_TOPK, t), F32)
    return pl.pallas_call(
        _peer_topk_kernel, name="peer_topk",
        out_shape=(jax.ShapeDtypeStruct((rows, n), F32), jax.ShapeDtypeStruct((rows, n), jnp.int32),
                   jax.ShapeDtypeStruct((rows, n), jnp.int32)),
        grid=(n // t - off, PEER_HEADS),
        in_specs=[pl.BlockSpec((t, PEER_KEY_DIM), lambda i, h: (i, h)),
                  pl.BlockSpec((1,) + keys.shape[1:], lambda i, h: (h, 0, 0))],
        out_specs=(out_spec, out_spec, out_spec),
        compiler_params=_cparams(2, vm),
    )(q, keys)


def _peer_gates_kernel(i1_ref, i2_ref, g_ref, o_ref):
    n_sel = i1_ref.shape[1]
    iota = lax.broadcasted_iota(jnp.int32, (PEER_NKEYS, n_sel), 0)
    nt = (((1,), (1,)), ((), ()))

    def body(t, carry):
        i1b = i1_ref[pl.ds(t, 1), :]
        i2b = i2_ref[pl.ds(t, 1), :]
        g = g_ref[pl.ds(t, 1), :]
        g_hi = g.astype(MXU_DTYPE).astype(F32)
        g_lo = g - g_hi
        m1 = iota == i1b
        lhs = jnp.concatenate([jnp.where(m1, g_hi, 0.0), jnp.where(m1, g_lo, 0.0)], axis=1).astype(MXU_DTYPE)
        hot2 = jnp.where(iota == i2b, 1.0, 0.0).astype(MXU_DTYPE)
        rhs = jnp.concatenate([hot2, hot2], axis=1)
        o_ref[t] = lax.dot_general(lhs, rhs, nt, preferred_element_type=F32)
        return carry

    lax.fori_loop(0, i1_ref.shape[0], body, 0)


def _peer_gates(i1, i2, g, *, skip):
    n, n_sel = i1.shape
    t = 64
    off = skip * (ROW_BLOCK // t)
    spec = pl.BlockSpec((t, n_sel), lambda i: (i, 0))
    vm = 2 * (3 * _nbytes((t, n_sel), F32) + _nbytes((t, PEER_NKEYS, PEER_NKEYS), F32))
    return pl.pallas_call(
        _peer_gates_kernel, name="peer_gates",
        out_shape=jax.ShapeDtypeStruct((n, PEER_NKEYS, PEER_NKEYS), F32),
        grid=(n // t - off,),
        in_specs=[spec, spec, spec],
        out_specs=pl.BlockSpec((t, PEER_NKEYS, PEER_NKEYS), lambda i: (i, 0, 0)),
        compiler_params=_cparams(1, vm),
    )(i1, i2, g)


def _peer_up_kernel(h_ref, u_ref, g_ref, o_ref):
    acc = jnp.dot(h_ref[...], u_ref[...], preferred_element_type=F32)
    for jj in range(g_ref.shape[1]):
        a = acc[:, jj * PEER_NKEYS:(jj + 1) * PEER_NKEYS]
        act = 0.5 * a * (1.0 + lax.erf(a * (2.0 ** -0.5)))
        o_ref[:, jj * PEER_NKEYS:(jj + 1) * PEER_NKEYS] = (act * g_ref[:, jj, :]).astype(o_ref.dtype)


def _peer_up(h, ut, gmat, *, skip):
    n, d = h.shape
    e = ut.shape[1]
    tm = ROW_BLOCK
    ti = SUBLANES
    tn = ti * PEER_NKEYS
    vm = 2 * (_nbytes((tm, d), MXU_DTYPE) + _nbytes((d, tn), MXU_DTYPE) + _nbytes((tm, ti, PEER_NKEYS), F32)
              + _nbytes((tm, tn), MXU_DTYPE)) + 3 * _nbytes((tm, tn), F32)
    return pl.pallas_call(
        _peer_up_kernel, name="peer_up",
        out_shape=jax.ShapeDtypeStruct((n, e), MXU_DTYPE),
        grid=(n // tm - skip, e // tn),
        in_specs=[pl.BlockSpec((tm, d), lambda i, j: (i, 0)),
                  pl.BlockSpec((d, tn), lambda i, j: (0, j)),
                  pl.BlockSpec((tm, ti, PEER_NKEYS), lambda i, j: (i, j, 0))],
        out_specs=pl.BlockSpec((tm, tn), lambda i, j: (i, j)),
        compiler_params=_cparams(2, vm),
    )(h, ut, gmat)


def _final_norm_kernel(x_ref, g_ref, o_ref):
    x = x_ref[...]
    o_ref[...] = x * lax.rsqrt(jnp.mean(x * x, axis=-1, keepdims=True) + EPS) * g_ref[...]


def _final_norm(x, gain, *, nl):
    d = x.shape[1]
    tb = 256
    vm = 4 * _nbytes((tb, d), F32) + 2 * _nbytes((tb, d), F32)
    return pl.pallas_call(
        _final_norm_kernel, name="final_norm",
        out_shape=jax.ShapeDtypeStruct((nl, d), F32),
        grid=(nl // tb,),
        in_specs=[pl.BlockSpec((tb, d), lambda i: (i, 0)), pl.BlockSpec((1, d), lambda i: (0, 0))],
        out_specs=pl.BlockSpec((tb, d), lambda i: (i, 0)),
        compiler_params=_cparams(1, vm),
    )(x, gain.reshape(1, d))


def _rope_lane_layout(w):
    half = MLA_ROPE // 2
    z = jnp.zeros(w.shape[:-1] + (LANES // 2 - half,), w.dtype)
    return jnp.concatenate([w[..., :half], z, w[..., half:], z], axis=-1)


def _rope_tables(seq, nc):
    pos = jnp.arange(seq, dtype=jnp.int32)
    n_freq = MLA_ROPE // 4
    inv_freq = jnp.power(ROPE_BASE, -jnp.arange(n_freq, dtype=F32) / n_freq)
    row = (pos // GRID_W).astype(F32)
    col = (pos % GRID_W).astype(F32)
    ang = jnp.concatenate([row[:, None] * inv_freq, col[:, None] * inv_freq], axis=-1)
    cos, sin = jnp.cos(ang), jnp.sin(ang)
    cos_l = _rope_lane_layout(jnp.concatenate([cos, cos], axis=-1))
    sin_l = _rope_lane_layout(jnp.concatenate([-sin, sin], axis=-1))
    one = _rope_lane_layout(jnp.ones((nc, MLA_ROPE), F32))
    return one, jnp.zeros((nc, LANES), F32), cos_l, sin_l


def kernel(x, c, ctx, c_ctx, w_ada, b_ada, norm_mix, norm_ffn, w_in, na_rpb, mla_q_norm, mla_kv_norm, mla_w_uq, mla_w_ukv, conv_w, w_branch_a, w_branch_b, w_branch_c, w_out, peer_w_q, peer_sub_keys, peer_u, peer_v, final_norm):
    n_batch, seq, d = x.shape
    lc = ctx.shape[1]
    depth = w_ada.shape[0]
    nc = n_batch * lc
    nl = n_batch * seq
    na_w = NA_HEADS * HEAD_DIM
    q_rank, kv_rank = mla_w_uq.shape[1], mla_w_ukv.shape[1]
    sc_w = conv_w.shape[2]
    rows = seq // GRID_W
    assert nc == ROW_BLOCK and seq % ROW_BLOCK == 0 and d % LANES == 0
    assert rows % NA_Q_ROWS == 0 and rows >= NA_K_ROWS
    assert na_w == q_rank == sc_w == MLA_HEADS * MLA_V and kv_rank * 2 == na_w
    kw = dict(nl=nl, seq=seq, n_batch=n_batch)

    split = np.cumsum([0, na_w, na_w, na_w, q_rank, kv_rank, MLA_ROPE, sc_w, sc_w, sc_w, N_BRANCH * d])
    wide = na_w
    col_gate = 8 * wide
    proj_cols = col_gate + N_BRANCH * d

    cmat = jnp.zeros((SUBLANES, d), F32).at[:n_batch].set(c).at[n_batch].set(c_ctx)
    xs = jnp.concatenate([x.reshape(nl, d), ctx.reshape(nc, d)], axis=0)

    ok, dr_idx, dc_idx, case_of_blk = _na_bias_tables(rows)
    one_c, zero_c, cos_l, sin_l = _rope_tables(seq, nc)
    cos_t = jnp.concatenate([cos_l] * n_batch + [one_c], axis=0)
    sin_t = jnp.concatenate([sin_l] * n_batch + [zero_c], axis=0)

    for l in range(depth):
        last = l == depth - 1
        skip = 1 if last else 0
        wl = w_in[l]
        seg = [wl[:, split[i]:split[i + 1]] for i in range(10)]
        pad = jnp.zeros((d, col_gate - 7 * wide - kv_rank - LANES), wl.dtype)
        w_proj = jnp.concatenate(
            [seg[0], seg[1], seg[2], seg[3], seg[6], seg[7], seg[8], seg[4], _rope_lane_layout(seg[5]), pad, seg[9]],
            axis=1).astype(MXU_DTYPE)
        wq = mla_w_uq[l].reshape(q_rank, MLA_HEADS, MLA_NOPE + MLA_ROPE)
        wq = jnp.concatenate([wq[..., :MLA_NOPE], _rope_lane_layout(wq[..., MLA_NOPE:])], axis=-1)
        wq = wq.reshape(q_rank, MLA_HEADS * MLA_HEAD_PAD).astype(MXU_DTYPE)
        wkv = mla_w_ukv[l].reshape(kv_rank, MLA_HEADS, MLA_NOPE + MLA_V)
        wk = wkv[..., :MLA_NOPE].reshape(kv_rank, MLA_HEADS * MLA_NOPE).astype(MXU_DTYPE)
        wv = wkv[..., MLA_NOPE:].reshape(kv_rank, MLA_HEADS * MLA_V).astype(MXU_DTYPE)
        bias = jnp.where(ok[None], na_rpb[l][:, dr_idx, dc_idx], NEG_INF).transpose(1, 0, 2, 3)

        mod = _ada_table(cmat, w_ada[l], b_ada[l])
        modtab = mod[:n_batch + 1].reshape((n_batch + 1) * N_MOD, 1, d)

        h = _normmod(xs, norm_mix[l], modtab, 0, skip=0, **kw)
        proj = _matmul(h, w_proj, tn=_tile(proj_cols, (1024, 512, 256)), out_dtype=MXU_DTYPE, name="in_proj")
        hd = wide // HEAD_DIM
        ya = _na_attention(proj, bias, jnp.asarray(case_of_blk), lc=lc, col_q=0, col_k=hd, col_v=2 * hd, **kw)
        qf, kf, vf = _mla_up(proj, mla_q_norm[l], mla_kv_norm[l], wq, wk, wv, cos_t, sin_t,
                             col_cq=3, col_ckv=7 * wide // kv_rank, col_kr=(7 * wide + kv_rank) // LANES)
        yb = _mla_attention(qf, kf, vf, lc=lc, **kw)
        if not last:
            ya = _ctx_attention(proj, proj, proj, ya, nl=nl, lc=lc, n_batch=n_batch, n_heads=NA_HEADS, dk=HEAD_DIM,
                                dv=HEAD_DIM, col_q=0, col_k=hd, col_v=2 * hd, scale=HEAD_DIM ** -0.5)
            yb = _ctx_attention(qf, kf, vf, yb, nl=nl, lc=lc, n_batch=n_batch, n_heads=MLA_HEADS, dk=MLA_HEAD_PAD,
                                dv=MLA_V, col_q=0, col_k=0, col_v=0, scale=(MLA_NOPE + MLA_ROPE) ** -0.5)
        tn_m = 1024 if d % 1024 == 0 else d
        m = _merge(ya, yb, proj, conv_w[l], w_branch_a[l].astype(MXU_DTYPE), w_branch_b[l].astype(MXU_DTYPE),
                   w_branch_c[l].astype(MXU_DTYPE), d=d, nl=nl, seq=seq, lc=lc, skip=skip,
                   col_gb=4, col_gc=5, col_xin=6, col_gate=col_gate // tn_m)
        xs = _matmul_resid(m, w_out[l].astype(MXU_DTYPE), xs, modtab, 2, tn=tn_m, tk=d, skip=skip,
                           name="out_proj", **kw)

        h2 = _normmod(xs, norm_ffn[l], modtab, 3, skip=skip, **kw)
        pq = _matmul(h2, peer_w_q[l].astype(MXU_DTYPE), tn=1024, out_dtype=MXU_DTYPE, skip=skip, name="peer_query")
        keys = peer_sub_keys[l].reshape(PEER_HEADS, 2 * PEER_NKEYS, PEER_KEY_DIM // 2).astype(MXU_DTYPE)
        gates_t, i1_t, i2_t = _peer_topk(pq, keys, skip=skip)
        gmat = _peer_gates(i1_t.T, i2_t.T, gates_t.T, skip=skip)
        hexp = _peer_up(h2, peer_u[l].T.astype(MXU_DTYPE), gmat, skip=skip)
        xs = _matmul_resid(hexp, peer_v[l].astype(MXU_DTYPE), xs, modtab, 5, tn=min(d, 2048), tk=1024,
                           skip=skip, name="peer_down", **kw)

    out = _final_norm(xs, final_norm, nl=nl)
    return out.reshape(n_batch, seq, d)
```

```python
import functools
import math

import numpy as np
import jax
import jax.numpy as jnp
from jax import lax
from jax.experimental import pallas as pl
from jax.experimental.pallas import tpu as pltpu

F32 = jnp.float32
MXU_DTYPE = jnp.bfloat16

GRID_W = 64
EPS = 1e-6
N_MOD = 6
NEG_INF = -1e30
HEAD_DIM = 128
NA_HEADS = 8
NA_WIN_H = 8
NA_WIN_W = 16
MLA_HEADS = 8
MLA_NOPE = 128
MLA_ROPE = 64
MLA_V = 128
ROPE_BASE = 10000.0
SC_TAPS = 3
N_BRANCH = 3
PEER_HEADS = 8
PEER_NKEYS = 128
PEER_KEY_DIM = 256
PEER_TOPK = 16

V7X_VMEM_BYTES = 64 * 1024 * 1024
LANES = 128
SUBLANES = 8

ROW_BLOCK = 512
MLA_HEAD_PAD = 256
NA_Q_ROWS = 8
NA_K_ROWS = 16
NA_K_CHUNK_ROWS = 4


def _cparams(n_axes, vmem_bytes):
    limit = min(int(vmem_bytes) + (4 << 20), V7X_VMEM_BYTES - (6 << 20))
    return pltpu.CompilerParams(dimension_semantics=("arbitrary",) * n_axes,
                                vmem_limit_bytes=limit)


def _nbytes(shape, dtype):
    return int(np.prod(shape)) * jnp.dtype(dtype).itemsize


def _tile(n, prefs):
    return next(t for t in prefs if n % t == 0)


def _mod_row(row0, nl, seq, n_batch):
    return jnp.where(row0 >= nl, n_batch, row0 // seq)


def _ada_kernel(c_ref, w_ref, b_ref, o_ref):
    c = c_ref[...]
    s = (c * jax.nn.sigmoid(c)).astype(MXU_DTYPE)
    acc = jnp.dot(s, w_ref[...].astype(MXU_DTYPE), preferred_element_type=F32)
    o_ref[...] = acc + b_ref[...]


def _ada_table(cmat, w_ada, b_ada):
    rows, d = cmat.shape
    n = w_ada.shape[1]
    tn = 512
    vm = 2 * (_nbytes((d, tn), F32) + _nbytes((rows, tn), F32) * 2) + _nbytes((rows, d), F32) * 2 \
        + _nbytes((d, tn), MXU_DTYPE)
    return pl.pallas_call(
        _ada_kernel, name="ada_table",
        out_shape=jax.ShapeDtypeStruct((rows, n), F32),
        grid=(n // tn,),
        in_specs=[pl.BlockSpec((rows, d), lambda j: (0, 0)),
                  pl.BlockSpec((d, tn), lambda j: (0, j)),
                  pl.BlockSpec((1, tn), lambda j: (0, j))],
        out_specs=pl.BlockSpec((rows, tn), lambda j: (0, j)),
        compiler_params=_cparams(1, vm),
    )(cmat, w_ada, b_ada.reshape(1, n))


def _normmod_kernel(x_ref, g_ref, sh_ref, sc_ref, o_ref):
    x = x_ref[...]
    y = x * lax.rsqrt(jnp.mean(x * x, axis=-1, keepdims=True) + EPS)
    o_ref[...] = ((y * g_ref[...]) * (1.0 + sc_ref[0]) + sh_ref[0]).astype(o_ref.dtype)


def _normmod(x, gain, modtab, m_shift, *, nl, seq, n_batch, skip):
    n, d = x.shape
    tb = 256
    off = skip * (ROW_BLOCK // tb)

    def mrow(i, m):
        return (_mod_row(i * tb, nl, seq, n_batch) * N_MOD + m, 0, 0)

    vm = 2 * (_nbytes((tb, d), F32) + _nbytes((tb, d), MXU_DTYPE)) + 3 * _nbytes((tb, d), F32)
    return pl.pallas_call(
        _normmod_kernel, name="normmod",
        out_shape=jax.ShapeDtypeStruct((n, d), MXU_DTYPE),
        grid=(n // tb - off,),
        in_specs=[pl.BlockSpec((tb, d), lambda i: (i, 0)),
                  pl.BlockSpec((1, d), lambda i: (0, 0)),
                  pl.BlockSpec((1, 1, d), lambda i: mrow(i, m_shift)),
                  pl.BlockSpec((1, 1, d), lambda i: mrow(i, m_shift + 1))],
        out_specs=pl.BlockSpec((tb, d), lambda i: (i, 0)),
        compiler_params=_cparams(1, vm),
    )(x, gain.reshape(1, d), modtab, modtab)


def _mm_kernel(x_ref, w_ref, o_ref):
    o_ref[...] = jnp.dot(x_ref[...], w_ref[...], preferred_element_type=F32).astype(o_ref.dtype)


def _matmul(x, w, *, tn, out_dtype, skip=0, name="matmul"):
    n, k = x.shape
    nn = w.shape[1]
    tm = ROW_BLOCK
    vm = 2 * (_nbytes((tm, k), x.dtype) + _nbytes((k, tn), w.dtype) + _nbytes((tm, tn), out_dtype)) \
        + _nbytes((tm, tn), F32)
    return pl.pallas_call(
        _mm_kernel, name=name,
        out_shape=jax.ShapeDtypeStruct((n, nn), out_dtype),
        grid=(n // tm - skip, nn // tn),
        in_specs=[pl.BlockSpec((tm, k), lambda i, j: (i, 0)),
                  pl.BlockSpec((k, tn), lambda i, j: (0, j))],
        out_specs=pl.BlockSpec((tm, tn), lambda i, j: (i, j)),
        compiler_params=_cparams(2, vm),
    )(x, w)


def _mm_resid_kernel(x_ref, w_ref, r_ref, g_ref, o_ref, acc_ref):
    k = pl.program_id(2)

    @pl.when(k == 0)
    def _():
        acc_ref[...] = jnp.zeros_like(acc_ref)

    acc_ref[...] += jnp.dot(x_ref[...], w_ref[...], preferred_element_type=F32)

    @pl.when(k == pl.num_programs(2) - 1)
    def _():
        o_ref[...] = r_ref[...] + g_ref[0] * acc_ref[...]


def _matmul_resid(x, w, resid, modtab, m_gate, *, tn, tk, nl, seq, n_batch, skip, name):
    n, kk = x.shape
    nn = w.shape[1]
    tm = ROW_BLOCK

    def mrow(i, j, k):
        return (_mod_row(i * tm, nl, seq, n_batch) * N_MOD + m_gate, 0, j)

    vm = 2 * (_nbytes((tm, tk), x.dtype) + _nbytes((tk, tn), w.dtype) + 2 * _nbytes((tm, tn), F32)) \
        + 2 * _nbytes((tm, tn), F32)
    return pl.pallas_call(
        _mm_resid_kernel, name=name,
        out_shape=jax.ShapeDtypeStruct((n, nn), F32),
        grid=(n // tm - skip, nn // tn, kk // tk),
        in_specs=[pl.BlockSpec((tm, tk), lambda i, j, k: (i, k)),
                  pl.BlockSpec((tk, tn), lambda i, j, k: (k, j)),
                  pl.BlockSpec((tm, tn), lambda i, j, k: (i, j)),
                  pl.BlockSpec((1, 1, tn), mrow)],
        out_specs=pl.BlockSpec((tm, tn), lambda i, j, k: (i, j)),
        scratch_shapes=[pltpu.VMEM((tm, tn), F32)],
        compiler_params=_cparams(3, vm),
    )(x, w, resid, modtab)


def _na_bias_tables(rows):
    n_dr = 2 * NA_WIN_H - 1
    n_blk = rows // NA_Q_ROWS
    rl = np.arange(NA_Q_ROWS)[:, None]
    krl = np.arange(NA_K_ROWS)[None, :]
    cases, case_of_blk = [], []
    for i in range(n_blk):
        r0 = i * NA_Q_ROWS
        k0 = int(np.clip(r0 - NA_K_CHUNK_ROWS, 0, rows - NA_K_ROWS))
        r = r0 + rl
        kr = k0 + krl
        start = np.clip(r - NA_WIN_H // 2, 0, rows - NA_WIN_H)
        ok = (kr >= start) & (kr < start + NA_WIN_H)
        dr = np.where(ok, kr - r + NA_WIN_H - 1, n_dr).astype(np.int32)
        assert dr.min() >= 0
        for ci, known in enumerate(cases):
            if np.array_equal(known, dr):
                case_of_blk.append(ci)
                break
        else:
            case_of_blk.append(len(cases))
            cases.append(dr)
    c = np.arange(GRID_W)[:, None]
    kc = np.arange(GRID_W)[None, :]
    ws = np.clip(c - NA_WIN_W // 2, 0, GRID_W - NA_WIN_W)
    col_ok = (kc >= ws) & (kc < ws + NA_WIN_W)
    dc = np.clip(kc - c + NA_WIN_W - 1, 0, 2 * NA_WIN_W - 2)
    dc_hot = (dc[None] == np.arange(2 * NA_WIN_W - 1)[:, None, None]).astype(np.float32)
    return np.stack(cases), np.asarray(case_of_blk, np.int32), dc_hot, col_ok


def _na_bias(rpb, dr_of, dc_hot, col_ok):
    n_case = dr_of.shape[0]
    n_h = rpb.shape[0]
    tb = jnp.einsum('hdk,kcq->hdcq', rpb.astype(F32), dc_hot, precision=lax.Precision.HIGHEST)
    tb = jnp.where(col_ok, tb, NEG_INF)
    tb = jnp.concatenate([tb, jnp.full((n_h, 1, GRID_W, GRID_W), NEG_INF, F32)], axis=1)
    rows = jnp.take(tb.transpose(1, 0, 2, 3).reshape(tb.shape[1], -1), dr_of.reshape(-1), axis=0)
    b = rows.reshape(n_case, NA_Q_ROWS, NA_K_ROWS, n_h, GRID_W, GRID_W).transpose(0, 3, 1, 4, 2, 5)
    return b.reshape(n_case, n_h, NA_Q_ROWS * GRID_W, NA_K_ROWS * GRID_W)


def _na_kernel(case_ref, q_ref, k0_ref, k1_ref, k2_ref, k3_ref, v0_ref, v1_ref, v2_ref, v3_ref,
               kc_ref, vc_ref, b_ref, o_ref):
    del case_ref
    scale = HEAD_DIM ** -0.5
    nt = (((1,), (1,)), ((), ()))
    q = q_ref[...]
    k_loc = jnp.concatenate([k0_ref[...], k1_ref[...], k2_ref[...], k3_ref[...]], axis=0)
    v_loc = jnp.concatenate([v0_ref[...], v1_ref[...], v2_ref[...], v3_ref[...]], axis=0)
    s_loc = lax.dot_general(q, k_loc, nt, preferred_element_type=F32) * scale + b_ref[0, 0]
    s_ctx = lax.dot_general(q, kc_ref[...], nt, preferred_element_type=F32) * scale
    m = jnp.maximum(jnp.max(s_loc, axis=-1, keepdims=True), jnp.max(s_ctx, axis=-1, keepdims=True))
    p_loc = jnp.exp(s_loc - m)
    p_ctx = jnp.exp(s_ctx - m)
    denom = jnp.sum(p_loc, axis=-1, keepdims=True) + jnp.sum(p_ctx, axis=-1, keepdims=True)
    o = (jnp.dot(p_loc.astype(MXU_DTYPE), v_loc, preferred_element_type=F32)
         + jnp.dot(p_ctx.astype(MXU_DTYPE), vc_ref[...], preferred_element_type=F32))
    o_ref[...] = (o / denom).astype(o_ref.dtype)


def _na_attention(proj, bias, case_of_blk, *, nl, seq, lc, n_batch, col_q, col_k, col_v):
    n = proj.shape[0]
    rows = seq // GRID_W
    tq = NA_Q_ROWS * GRID_W
    tkc = NA_K_CHUNK_ROWS * GRID_W
    n_blk = rows // NA_Q_ROWS
    n_chunk = NA_K_ROWS // NA_K_CHUNK_ROWS
    max_c0 = rows // NA_K_CHUNK_ROWS - n_chunk

    def qmap(h, b, i, case):
        return (b * (seq // tq) + i, col_q + h)

    def kvmap(col, j):
        def f(h, b, i, case):
            c0 = jnp.clip(i * (NA_Q_ROWS // NA_K_CHUNK_ROWS) - 1, 0, max_c0)
            return (b * (seq // tkc) + c0 + j, col + h)
        return f

    def ctxmap(col):
        return lambda h, b, i, case: (nl // lc + b, col + h)

    in_specs = [pl.BlockSpec((tq, HEAD_DIM), qmap)]
    in_specs += [pl.BlockSpec((tkc, HEAD_DIM), kvmap(col_k, j)) for j in range(n_chunk)]
    in_specs += [pl.BlockSpec((tkc, HEAD_DIM), kvmap(col_v, j)) for j in range(n_chunk)]
    in_specs += [pl.BlockSpec((lc, HEAD_DIM), ctxmap(col_k)), pl.BlockSpec((lc, HEAD_DIM), ctxmap(col_v))]
    in_specs += [pl.BlockSpec((1, 1, tq, NA_K_ROWS * GRID_W), lambda h, b, i, case: (case[i], h, 0, 0))]
    vm = 2 * (_nbytes((tq, NA_K_ROWS * GRID_W), F32) + 12 * _nbytes((tq, HEAD_DIM), MXU_DTYPE)) \
        + 6 * _nbytes((tq, NA_K_ROWS * GRID_W + lc), F32)
    grid_spec = pltpu.PrefetchScalarGridSpec(
        num_scalar_prefetch=1, grid=(NA_HEADS, n_batch, n_blk),
        in_specs=in_specs,
        out_specs=pl.BlockSpec((tq, HEAD_DIM), lambda h, b, i, case: (b * (seq // tq) + i, h)))
    return pl.pallas_call(
        _na_kernel, name="na_attention",
        out_shape=jax.ShapeDtypeStruct((n, NA_HEADS * HEAD_DIM), MXU_DTYPE),
        grid_spec=grid_spec,
        compiler_params=_cparams(3, vm),
    )(case_of_blk, *([proj] * (1 + 2 * n_chunk + 2)), bias)


def _ctx_attn_kernel(q_ref, k_ref, v_ref, prev_ref, o_ref, *, scale):
    del prev_ref
    s = lax.dot_general(q_ref[...], k_ref[...], (((1,), (1,)), ((), ())),
                        preferred_element_type=F32) * scale
    m = jnp.max(s, axis=-1, keepdims=True)
    p = jnp.exp(s - m)
    o = jnp.dot(p.astype(MXU_DTYPE), v_ref[...], preferred_element_type=F32)
    o_ref[...] = (o / jnp.sum(p, axis=-1, keepdims=True)).astype(o_ref.dtype)


def _ctx_attention(qa, ka, va, y, *, nl, lc, n_batch, n_heads, dk, dv, col_q, col_k, col_v, scale):
    vm = 2 * (2 * _nbytes((lc, dk), MXU_DTYPE) + 2 * _nbytes((lc, dv), MXU_DTYPE)) + 4 * _nbytes((lc, lc), F32)
    return pl.pallas_call(
        functools.partial(_ctx_attn_kernel, scale=scale), name="ctx_attention",
        out_shape=jax.ShapeDtypeStruct(y.shape, y.dtype),
        grid=(n_batch, n_heads),
        in_specs=[pl.BlockSpec((lc, dk), lambda b, h: (nl // lc + b, col_q + h)),
                  pl.BlockSpec((lc, dk), lambda b, h: (nl // lc + b, col_k + h)),
                  pl.BlockSpec((lc, dv), lambda b, h: (nl // lc + b, col_v + h)),
                  pl.BlockSpec(memory_space=pl.ANY)],
        out_specs=pl.BlockSpec((lc, dv), lambda b, h: (nl // lc + b, h)),
        input_output_aliases={3: 0},
        compiler_params=_cparams(2, vm),
    )(qa, ka, va, y)


def _mla_up_kernel(cq_ref, ckv_ref, kr_ref, qn_ref, kvn_ref, wq_ref, wk_ref, wv_ref, cos_ref, sin_ref,
                   q_ref, k_ref, v_ref):
    def rms(x, g):
        return (x * lax.rsqrt(jnp.mean(x * x, axis=-1, keepdims=True) + EPS) * g).astype(MXU_DTYPE)

    cos = cos_ref[...]
    sin = sin_ref[...]

    def rope(x):
        return x * cos + pltpu.roll(x, 2 * (MLA_ROPE // 2), axis=1) * sin

    cqn = rms(cq_ref[...].astype(F32), qn_ref[...])
    kvn = rms(ckv_ref[...].astype(F32), kvn_ref[...])
    q = jnp.dot(cqn, wq_ref[...], preferred_element_type=F32)
    kn = jnp.dot(kvn, wk_ref[...], preferred_element_type=F32)
    v_ref[...] = jnp.dot(kvn, wv_ref[...], preferred_element_type=F32).astype(v_ref.dtype)
    kr = rope(kr_ref[...].astype(F32)).astype(k_ref.dtype)
    for h in range(MLA_HEADS):
        c0 = h * MLA_HEAD_PAD
        q_ref[:, c0:c0 + MLA_NOPE] = q[:, c0:c0 + MLA_NOPE].astype(q_ref.dtype)
        q_ref[:, c0 + MLA_NOPE:c0 + MLA_HEAD_PAD] = rope(q[:, c0 + MLA_NOPE:c0 + MLA_HEAD_PAD]).astype(q_ref.dtype)
        k_ref[:, c0:c0 + MLA_NOPE] = kn[:, h * MLA_NOPE:(h + 1) * MLA_NOPE].astype(k_ref.dtype)
        k_ref[:, c0 + MLA_NOPE:c0 + MLA_HEAD_PAD] = kr


def _mla_up(proj, q_norm, kv_norm, wq, wk, wv, cos_t, sin_t, *, col_cq, col_ckv, col_kr):
    n = proj.shape[0]
    tm = 256
    rq, rkv = wq.shape[0], wk.shape[0]
    wq_cols = wq.shape[1]
    vm = 2 * (_nbytes(wq.shape, MXU_DTYPE) + _nbytes(wk.shape, MXU_DTYPE) + _nbytes(wv.shape, MXU_DTYPE)
              + _nbytes((tm, rq + rkv + LANES), MXU_DTYPE) + 2 * _nbytes((tm, LANES), F32)
              + _nbytes((tm, 2 * wq_cols + wv.shape[1]), MXU_DTYPE)) + 3 * _nbytes((tm, wq_cols), F32)
    return pl.pallas_call(
        _mla_up_kernel, name="mla_up",
        out_shape=(jax.ShapeDtypeStruct((n, wq_cols), MXU_DTYPE),
                   jax.ShapeDtypeStruct((n, wq_cols), MXU_DTYPE),
                   jax.ShapeDtypeStruct((n, wv.shape[1]), MXU_DTYPE)),
        grid=(n // tm,),
        in_specs=[pl.BlockSpec((tm, rq), lambda i: (i, col_cq)),
                  pl.BlockSpec((tm, rkv), lambda i: (i, col_ckv)),
                  pl.BlockSpec((tm, LANES), lambda i: (i, col_kr)),
                  pl.BlockSpec((1, rq), lambda i: (0, 0)),
                  pl.BlockSpec((1, rkv), lambda i: (0, 0)),
                  pl.BlockSpec(wq.shape, lambda i: (0, 0)),
                  pl.BlockSpec(wk.shape, lambda i: (0, 0)),
                  pl.BlockSpec(wv.shape, lambda i: (0, 0)),
                  pl.BlockSpec((tm, LANES), lambda i: (i, 0)),
                  pl.BlockSpec((tm, LANES), lambda i: (i, 0))],
        out_specs=(pl.BlockSpec((tm, wq_cols), lambda i: (i, 0)),
                   pl.BlockSpec((tm, wq_cols), lambda i: (i, 0)),
                   pl.BlockSpec((tm, wv.shape[1]), lambda i: (i, 0))),
        compiler_params=_cparams(1, vm),
    )(proj, proj, proj, q_norm.reshape(1, rq), kv_norm.reshape(1, rkv), wq, wk, wv, cos_t, sin_t)


def _flash_kernel(q_ref, kc_ref, kl_ref, vc_ref, vl_ref, o_ref, m_ref, l_ref, acc_ref, *, scale, tk):
    nt = (((1,), (1,)), ((), ()))
    q = q_ref[...]
    c = scale * math.log2(math.e)

    def chunk(k, v):
        s = lax.dot_general(q, k, nt, preferred_element_type=F32)
        m_old = m_ref[...]
        m_new = jnp.maximum(m_old, jnp.max(s, axis=-1, keepdims=True) * c)
        alpha = jnp.exp2(m_old - m_new)
        p = jnp.exp2(s * c - pltpu.repeat(m_new, s.shape[1] // LANES, 1))
        l_ref[...] = alpha * l_ref[...] + jnp.sum(p, axis=-1, keepdims=True)
        acc_ref[...] = alpha * acc_ref[...] + jnp.dot(p.astype(MXU_DTYPE), v, preferred_element_type=F32)
        m_ref[...] = m_new

    m_ref[...] = jnp.full_like(m_ref, -jnp.inf)
    l_ref[...] = jnp.zeros_like(l_ref)
    acc_ref[...] = jnp.zeros_like(acc_ref)
    chunk(kc_ref[...], vc_ref[...])

    def body(j, carry):
        r0 = pl.multiple_of(j * tk, tk)
        chunk(kl_ref[pl.ds(r0, tk), :], vl_ref[pl.ds(r0, tk), :])
        return carry

    lax.fori_loop(0, kl_ref.shape[0] // tk, body, 0)
    o_ref[...] = (acc_ref[...] / l_ref[...]).astype(o_ref.dtype)


def _mla_attention(qf, kf, vf, *, nl, seq, lc, n_batch):
    n = qf.shape[0]
    tq, tk = _tile(seq, (1024, 512)), 512
    dk, dv = MLA_HEAD_PAD, MLA_V
    assert dv == LANES
    scale = (MLA_NOPE + MLA_ROPE) ** -0.5
    vm = 2 * (_nbytes((tq, dk), MXU_DTYPE) + _nbytes((lc + seq, dk + dv), MXU_DTYPE) + _nbytes((tq, dv), MXU_DTYPE)) \
        + 6 * _nbytes((tq, tk), F32) + 2 * _nbytes((tq, dv), F32)
    return pl.pallas_call(
        functools.partial(_flash_kernel, scale=scale, tk=tk), name="mla_attention",
        out_shape=jax.ShapeDtypeStruct((n, MLA_HEADS * dv), MXU_DTYPE),
        grid=(n_batch, MLA_HEADS, seq // tq),
        in_specs=[pl.BlockSpec((tq, dk), lambda b, h, i: (b * (seq // tq) + i, h)),
                  pl.BlockSpec((lc, dk), lambda b, h, i: (nl // lc + b, h)),
                  pl.BlockSpec((seq, dk), lambda b, h, i: (b, h)),
                  pl.BlockSpec((lc, dv), lambda b, h, i: (nl // lc + b, h)),
                  pl.BlockSpec((seq, dv), lambda b, h, i: (b, h))],
        out_specs=pl.BlockSpec((tq, dv), lambda b, h, i: (b * (seq // tq) + i, h)),
        scratch_shapes=[pltpu.VMEM((tq, LANES), F32), pltpu.VMEM((tq, LANES), F32), pltpu.VMEM((tq, dv), F32)],
        compiler_params=_cparams(3, vm),
    )(qf, kf, kf, vf, vf)


def _merge_kernel(ya_ref, yb_ref, gb_ref, gc_ref, xin_ref, gcp_ref, xinp_ref, gcn_ref, xinn_ref, cw_ref,
                  wa_ref, wb_ref, wc_ref, g0_ref, g1_ref, g2_ref, o_ref, yc_ref, *, nl, seq, lc):
    i = pl.program_id(0)
    j = pl.program_id(1)
    tm = ya_ref.shape[0]

    @pl.when(j == 0)
    def _():
        u = gc_ref[...].astype(F32) * xin_ref[...].astype(F32)
        u_prev_halo = (gcp_ref[...].astype(F32) * xinp_ref[...].astype(F32))[SUBLANES - 1:SUBLANES, :]
        u_next_halo = (gcn_ref[...].astype(F32) * xinn_ref[...].astype(F32))[0:1, :]
        row = lax.broadcasted_iota(jnp.int32, (tm, 1), 0)
        g = i * tm + row
        pos = jnp.where(g < nl, g % seq, (g - nl) % lc)
        seq_len = jnp.where(g < nl, seq, lc)
        u_dn = jnp.where(row == 0, u_prev_halo, pltpu.roll(u, 1, axis=0))
        u_up = jnp.where(row == tm - 1, u_next_halo, pltpu.roll(u, tm - 1, axis=0))
        u_dn = jnp.where(pos == 0, 0.0, u_dn)
        u_up = jnp.where(pos == seq_len - 1, 0.0, u_up)
        cw = cw_ref[...]
        y = cw[0:1, :] * u_dn + cw[1:2, :] * u + cw[2:3, :] * u_up
        yc_ref[...] = (gb_ref[...].astype(F32) * y).astype(yc_ref.dtype)

    def branch(y, w_ref, g_ref):
        gate = jax.nn.sigmoid(g_ref[...].astype(F32))
        return gate * jnp.dot(y, w_ref[...], preferred_element_type=F32)

    m = (branch(ya_ref[...], wa_ref, g0_ref) + branch(yb_ref[...], wb_ref, g1_ref)
         + branch(yc_ref[...], wc_ref, g2_ref))
    o_ref[...] = m.astype(o_ref.dtype)


def _merge(ya, yb, proj, conv_w, wa, wb, wc, *, d, nl, seq, lc, skip, col_gb, col_gc, col_xin, col_gate):
    n = ya.shape[0]
    tm, tn = ROW_BLOCK, 1024 if d % 1024 == 0 else d
    w = ya.shape[1]
    hb = tm // SUBLANES
    last_halo = n // SUBLANES - 1
    n_col = d // tn

    def row(i, j):
        return (i, 0)

    def cur(col):
        return lambda i, j: (i, col)

    def prev(col):
        return lambda i, j: (jnp.maximum(i * hb - 1, 0), col)

    def nxt(col):
        return lambda i, j: (jnp.minimum((i + 1) * hb, last_halo), col)

    def gate(br):
        return lambda i, j: (i, col_gate + br * n_col + j)

    in_specs = [pl.BlockSpec((tm, w), row), pl.BlockSpec((tm, w), row),
                pl.BlockSpec((tm, w), cur(col_gb)), pl.BlockSpec((tm, w), cur(col_gc)),
                pl.BlockSpec((tm, w), cur(col_xin)),
                pl.BlockSpec((SUBLANES, w), prev(col_gc)), pl.BlockSpec((SUBLANES, w), prev(col_xin)),
                pl.BlockSpec((SUBLANES, w), nxt(col_gc)), pl.BlockSpec((SUBLANES, w), nxt(col_xin)),
                pl.BlockSpec((SC_TAPS, w), lambda i, j: (0, 0)),
                pl.BlockSpec((w, tn), lambda i, j: (0, j)), pl.BlockSpec((w, tn), lambda i, j: (0, j)),
                pl.BlockSpec((w, tn), lambda i, j: (0, j)),
                pl.BlockSpec((tm, tn), gate(0)), pl.BlockSpec((tm, tn), gate(1)), pl.BlockSpec((tm, tn), gate(2))]
    vm = 2 * (5 * _nbytes((tm, w), MXU_DTYPE) + 3 * _nbytes((w, tn), MXU_DTYPE) + 4 * _nbytes((tm, tn), MXU_DTYPE)) \
        + _nbytes((tm, w), MXU_DTYPE) + 6 * _nbytes((tm, max(w, tn)), F32)
    return pl.pallas_call(
        functools.partial(_merge_kernel, nl=nl, seq=seq, lc=lc), name="conv_merge",
        out_shape=jax.ShapeDtypeStruct((n, d), MXU_DTYPE),
        grid=(n // tm - skip, n_col),
        in_specs=in_specs,
        out_specs=pl.BlockSpec((tm, tn), lambda i, j: (i, j)),
        scratch_shapes=[pltpu.VMEM((tm, w), MXU_DTYPE)],
        compiler_params=_cparams(2, vm),
    )(ya, yb, proj, proj, proj, proj, proj, proj, proj, conv_w, wa, wb, wc, proj, proj, proj)


def _topk_cols(s, k, ids=None):
    if ids is None:
        ids = lax.broadcasted_iota(jnp.int32, s.shape, 0)
    big = jnp.iinfo(jnp.int32).max
    vals, idxs = [], []
    for _ in range(k):
        m = jnp.max(s, axis=0, keepdims=True)
        idx = jnp.min(jnp.where(s == m, ids, big), axis=0, keepdims=True)
        vals.append(m)
        idxs.append(idx)
        s = jnp.where(ids == idx, -jnp.inf, s)
    return jnp.concatenate(vals, axis=0), jnp.concatenate(idxs, axis=0)


def _candidate_rows():
    k = PEER_TOPK
    groups = []
    for b0 in range(0, k, SUBLANES):
        groups.append((0, 1, b0, b0 + SUBLANES))
    a = 1
    while a < k and k // (a + 1) > 1:
        assert k // (a + 1) <= SUBLANES
        groups.append((a, a + 1, 0, SUBLANES))
        a += 1
    while a < k:
        groups.append((a, a + SUBLANES, 0, 1))
        a += SUBLANES
    covered = {(aa, bb) for a0, a1, b0, b1 in groups for aa in range(a0, a1) for bb in range(b0, b1)}
    assert all((aa, bb) in covered for aa in range(k) for bb in range(k) if (aa + 1) * (bb + 1) <= k)
    return groups


def _take_rows(table, idx):
    iota = lax.broadcasted_iota(jnp.int32, table.shape, 0)
    rows = [jnp.sum(jnp.where(iota == idx[j:j + 1, :], table, 0), axis=0, keepdims=True)
            for j in range(idx.shape[0])]
    return jnp.concatenate(rows, axis=0)


def _peer_topk_kernel(q_ref, keys_ref, g_ref, i1_ref, i2_ref):
    nt = (((1,), (1,)), ((), ()))
    half = PEER_KEY_DIM // 2
    q = q_ref[...]
    s1 = lax.dot_general(keys_ref[0, :PEER_NKEYS, :], q[:, :half], nt, preferred_element_type=F32)
    s2 = lax.dot_general(keys_ref[0, PEER_NKEYS:, :], q[:, half:], nt, preferred_element_type=F32)
    v1, j1 = _topk_cols(s1, PEER_TOPK)
    v2, j2 = _topk_cols(s2, PEER_TOPK)
    sub = lax.broadcasted_iota(jnp.int32, (SUBLANES, q.shape[0]), 0)
    cand, flat = [], []
    for a0, a1, b0, b1 in _candidate_rows():
        cand.append(v1[a0:a1, :] + v2[b0:b1, :])
        flat.append(sub * (1 if a1 - a0 == 1 else PEER_TOPK) + (a0 * PEER_TOPK + b0))
    top_s, pos = _topk_cols(jnp.concatenate(cand, axis=0), PEER_TOPK, jnp.concatenate(flat, axis=0))
    i1_ref[...] = _take_rows(j1, jnp.right_shift(pos, PEER_TOPK.bit_length() - 1))
    i2_ref[...] = _take_rows(j2, jnp.bitwise_and(pos, PEER_TOPK - 1))
    e = jnp.exp(top_s - top_s[0:1, :])
    g_ref[...] = e / jnp.sum(e, axis=0, keepdims=True)


def _peer_topk(q, keys, *, skip):
    n = q.shape[0]
    t = 256
    off = skip * (ROW_BLOCK // t)
    rows = PEER_HEADS * PEER_TOPK
    out_spec = pl.BlockSpec((PEER_TOPK, t), lambda i, h: (h, i))
    vm = 2 * (_nbytes((t, PEER_KEY_DIM), MXU_DTYPE) + _nbytes(keys.shape[1:], MXU_DTYPE)) \
        + 8 * _nbytes((PEER_TOPK * PEER_TOPK, t), F32)
    return pl.pallas_call(
        _peer_topk_kernel, name="peer_topk",
        out_shape=(jax.ShapeDtypeStruct((rows, n), F32), jax.ShapeDtypeStruct((rows, n), jnp.int32),
                   jax.ShapeDtypeStruct((rows, n), jnp.int32)),
        grid=(n // t - off, PEER_HEADS),
        in_specs=[pl.BlockSpec((t, PEER_KEY_DIM), lambda i, h: (i, h)),
                  pl.BlockSpec((1,) + keys.shape[1:], lambda i, h: (h, 0, 0))],
        out_specs=(out_spec, out_spec, out_spec),
        compiler_params=_cparams(2, vm),
    )(q, keys)


def _peer_gates_kernel(i1_ref, i2_ref, g_ref, o_ref):
    n_sel = i1_ref.shape[1]
    iota = lax.broadcasted_iota(jnp.int32, (PEER_NKEYS, n_sel), 0)
    nt = (((1,), (1,)), ((), ()))

    def body(t, carry):
        i1b = i1_ref[pl.ds(t, 1), :]
        i2b = i2_ref[pl.ds(t, 1), :]
        g = g_ref[pl.ds(t, 1), :]
        g_hi = g.astype(MXU_DTYPE).astype(F32)
        g_lo = g - g_hi
        m1 = iota == i1b
        lhs = jnp.concatenate([jnp.where(m1, g_hi, 0.0), jnp.where(m1, g_lo, 0.0)], axis=1).astype(MXU_DTYPE)
        hot2 = jnp.where(iota == i2b, 1.0, 0.0).astype(MXU_DTYPE)
        rhs = jnp.concatenate([hot2, hot2], axis=1)
        o_ref[t] = lax.dot_general(lhs, rhs, nt, preferred_element_type=F32)
        return carry

    lax.fori_loop(0, i1_ref.shape[0], body, 0, unroll=32)


def _peer_gates(i1, i2, g, *, skip):
    n, n_sel = i1.shape
    t = 128
    off = skip * (ROW_BLOCK // t)
    spec = pl.BlockSpec((t, n_sel), lambda i: (i, 0))
    vm = 2 * (3 * _nbytes((t, n_sel), F32) + _nbytes((t, PEER_NKEYS, PEER_NKEYS), F32))
    return pl.pallas_call(
        _peer_gates_kernel, name="peer_gates",
        out_shape=jax.ShapeDtypeStruct((n, PEER_NKEYS, PEER_NKEYS), F32),
        grid=(n // t - off,),
        in_specs=[spec, spec, spec],
        out_specs=pl.BlockSpec((t, PEER_NKEYS, PEER_NKEYS), lambda i: (i, 0, 0)),
        compiler_params=_cparams(1, vm),
    )(i1, i2, g)


def _peer_up_kernel(h_ref, u_ref, g_ref, o_ref):
    acc = jnp.dot(h_ref[...], u_ref[...], preferred_element_type=F32)
    for jj in range(g_ref.shape[1]):
        a = acc[:, jj * PEER_NKEYS:(jj + 1) * PEER_NKEYS]
        act = 0.5 * a * (1.0 + lax.erf(a * (2.0 ** -0.5)))
        o_ref[:, jj * PEER_NKEYS:(jj + 1) * PEER_NKEYS] = (act * g_ref[:, jj, :]).astype(o_ref.dtype)


def _peer_up(h, ut, gmat, *, skip):
    n, d = h.shape
    e = ut.shape[1]
    tm = ROW_BLOCK
    ti = SUBLANES
    tn = ti * PEER_NKEYS
    vm = 2 * (_nbytes((tm, d), MXU_DTYPE) + _nbytes((d, tn), MXU_DTYPE) + _nbytes((tm, ti, PEER_NKEYS), F32)
              + _nbytes((tm, tn), MXU_DTYPE)) + 3 * _nbytes((tm, tn), F32)
    return pl.pallas_call(
        _peer_up_kernel, name="peer_up",
        out_shape=jax.ShapeDtypeStruct((n, e), MXU_DTYPE),
        grid=(n // tm - skip, e // tn),
        in_specs=[pl.BlockSpec((tm, d), lambda i, j: (i, 0)),
                  pl.BlockSpec((d, tn), lambda i, j: (0, j)),
                  pl.BlockSpec((tm, ti, PEER_NKEYS), lambda i, j: (i, j, 0))],
        out_specs=pl.BlockSpec((tm, tn), lambda i, j: (i, j)),
        compiler_params=_cparams(2, vm),
    )(h, ut, gmat)


def _final_norm_kernel(x_ref, g_ref, o_ref):
    x = x_ref[...]
    o_ref[...] = x * lax.rsqrt(jnp.mean(x * x, axis=-1, keepdims=True) + EPS) * g_ref[...]


def _final_norm(x, gain, *, nl):
    d = x.shape[1]
    tb = 256
    vm = 4 * _nbytes((tb, d), F32) + 2 * _nbytes((tb, d), F32)
    return pl.pallas_call(
        _final_norm_kernel, name="final_norm",
        out_shape=jax.ShapeDtypeStruct((nl, d), F32),
        grid=(nl // tb,),
        in_specs=[pl.BlockSpec((tb, d), lambda i: (i, 0)), pl.BlockSpec((1, d), lambda i: (0, 0))],
        out_specs=pl.BlockSpec((tb, d), lambda i: (i, 0)),
        compiler_params=_cparams(1, vm),
    )(x, gain.reshape(1, d))


def _rope_lane_layout(w):
    half = MLA_ROPE // 2
    z = jnp.zeros(w.shape[:-1] + (LANES // 2 - half,), w.dtype)
    return jnp.concatenate([w[..., :half], z, w[..., half:], z], axis=-1)


def _rope_tables(seq, nc):
    pos = jnp.arange(seq, dtype=jnp.int32)
    n_freq = MLA_ROPE // 4
    inv_freq = jnp.power(ROPE_BASE, -jnp.arange(n_freq, dtype=F32) / n_freq)
    row = (pos // GRID_W).astype(F32)
    col = (pos % GRID_W).astype(F32)
    ang = jnp.concatenate([row[:, None] * inv_freq, col[:, None] * inv_freq], axis=-1)
    cos, sin = jnp.cos(ang), jnp.sin(ang)
    cos_l = _rope_lane_layout(jnp.concatenate([cos, cos], axis=-1))
    sin_l = _rope_lane_layout(jnp.concatenate([-sin, sin], axis=-1))
    one = _rope_lane_layout(jnp.ones((nc, MLA_ROPE), F32))
    return one, jnp.zeros((nc, LANES), F32), cos_l, sin_l


def kernel(x, c, ctx, c_ctx, w_ada, b_ada, norm_mix, norm_ffn, w_in, na_rpb, mla_q_norm, mla_kv_norm, mla_w_uq, mla_w_ukv, conv_w, w_branch_a, w_branch_b, w_branch_c, w_out, peer_w_q, peer_sub_keys, peer_u, peer_v, final_norm):
    n_batch, seq, d = x.shape
    lc = ctx.shape[1]
    depth = w_ada.shape[0]
    nc = n_batch * lc
    nl = n_batch * seq
    na_w = NA_HEADS * HEAD_DIM
    q_rank, kv_rank = mla_w_uq.shape[1], mla_w_ukv.shape[1]
    sc_w = conv_w.shape[2]
    rows = seq // GRID_W
    assert nc == ROW_BLOCK and seq % ROW_BLOCK == 0 and d % LANES == 0
    assert rows % NA_Q_ROWS == 0 and rows >= NA_K_ROWS
    assert na_w == q_rank == sc_w == MLA_HEADS * MLA_V and kv_rank * 2 == na_w
    kw = dict(nl=nl, seq=seq, n_batch=n_batch)

    split = np.cumsum([0, na_w, na_w, na_w, q_rank, kv_rank, MLA_ROPE, sc_w, sc_w, sc_w, N_BRANCH * d])
    wide = na_w
    col_gate = 8 * wide
    proj_cols = col_gate + N_BRANCH * d

    cmat = jnp.zeros((SUBLANES, d), F32).at[:n_batch].set(c).at[n_batch].set(c_ctx)
    xs = jnp.concatenate([x.reshape(nl, d), ctx.reshape(nc, d)], axis=0)

    dr_of, case_of_blk, dc_hot, col_ok = _na_bias_tables(rows)
    one_c, zero_c, cos_l, sin_l = _rope_tables(seq, nc)
    cos_t = jnp.concatenate([cos_l] * n_batch + [one_c], axis=0)
    sin_t = jnp.concatenate([sin_l] * n_batch + [zero_c], axis=0)

    for l in range(depth):
        last = l == depth - 1
        skip = 1 if last else 0
        wl = w_in[l]
        seg = [wl[:, split[i]:split[i + 1]] for i in range(10)]
        pad = jnp.zeros((d, col_gate - 7 * wide - kv_rank - LANES), wl.dtype)
        w_proj = jnp.concatenate(
            [seg[0], seg[1], seg[2], seg[3], seg[6], seg[7], seg[8], seg[4], _rope_lane_layout(seg[5]), pad, seg[9]],
            axis=1).astype(MXU_DTYPE)
        wq = mla_w_uq[l].reshape(q_rank, MLA_HEADS, MLA_NOPE + MLA_ROPE)
        wq = jnp.concatenate([wq[..., :MLA_NOPE], _rope_lane_layout(wq[..., MLA_NOPE:])], axis=-1)
        wq = wq.reshape(q_rank, MLA_HEADS * MLA_HEAD_PAD).astype(MXU_DTYPE)
        wkv = mla_w_ukv[l].reshape(kv_rank, MLA_HEADS, MLA_NOPE + MLA_V)
        wk = wkv[..., :MLA_NOPE].reshape(kv_rank, MLA_HEADS * MLA_NOPE).astype(MXU_DTYPE)
        wv = wkv[..., MLA_NOPE:].reshape(kv_rank, MLA_HEADS * MLA_V).astype(MXU_DTYPE)
        bias = _na_bias(na_rpb[l], dr_of, dc_hot, col_ok)

        mod = _ada_table(cmat, w_ada[l], b_ada[l])
        modtab = mod[:n_batch + 1].reshape((n_batch + 1) * N_MOD, 1, d)

        h = _normmod(xs, norm_mix[l], modtab, 0, skip=0, **kw)
        proj = _matmul(h, w_proj, tn=_tile(proj_cols, (1024, 512, 256)), out_dtype=MXU_DTYPE, name="in_proj")
        hd = wide // HEAD_DIM
        ya = _na_attention(proj, bias, jnp.asarray(case_of_blk), lc=lc, col_q=0, col_k=hd, col_v=2 * hd, **kw)
        qf, kf, vf = _mla_up(proj, mla_q_norm[l], mla_kv_norm[l], wq, wk, wv, cos_t, sin_t,
                             col_cq=3, col_ckv=7 * wide // kv_rank, col_kr=(7 * wide + kv_rank) // LANES)
        yb = _mla_attention(qf, kf, vf, lc=lc, **kw)
        if not last:
            ya = _ctx_attention(proj, proj, proj, ya, nl=nl, lc=lc, n_batch=n_batch, n_heads=NA_HEADS, dk=HEAD_DIM,
                                dv=HEAD_DIM, col_q=0, col_k=hd, col_v=2 * hd, scale=HEAD_DIM ** -0.5)
            yb = _ctx_attention(qf, kf, vf, yb, nl=nl, lc=lc, n_batch=n_batch, n_heads=MLA_HEADS, dk=MLA_HEAD_PAD,
                                dv=MLA_V, col_q=0, col_k=0, col_v=0, scale=(MLA_NOPE + MLA_ROPE) ** -0.5)
        tn_m = 1024 if d % 1024 == 0 else d
        m = _merge(ya, yb, proj, conv_w[l], w_branch_a[l].astype(MXU_DTYPE), w_branch_b[l].astype(MXU_DTYPE),
                   w_branch_c[l].astype(MXU_DTYPE), d=d, nl=nl, seq=seq, lc=lc, skip=skip,
                   col_gb=4, col_gc=5, col_xin=6, col_gate=col_gate // tn_m)
        xs = _matmul_resid(m, w_out[l].astype(MXU_DTYPE), xs, modtab, 2, tn=tn_m, tk=d, skip=skip,
                           name="out_proj", **kw)

        h2 = _normmod(xs, norm_ffn[l], modtab, 3, skip=skip, **kw)
        pq = _matmul(h2, peer_w_q[l].astype(MXU_DTYPE), tn=1024, out_dtype=MXU_DTYPE, skip=skip, name="peer_query")
        keys = peer_sub_keys[l].reshape(PEER_HEADS, 2 * PEER_NKEYS, PEER_KEY_DIM // 2).astype(MXU_DTYPE)
        gates_t, i1_t, i2_t = _peer_topk(pq, keys, skip=skip)
        gmat = _peer_gates(i1_t.T, i2_t.T, gates_t.T, skip=skip)
        hexp = _peer_up(h2, peer_u[l].T.astype(MXU_DTYPE), gmat, skip=skip)
        xs = _matmul_resid(hexp, peer_v[l].astype(MXU_DTYPE), xs, modtab, 5, tn=min(d, 2048), tk=2048,
                           skip=skip, name="peer_down", **kw)

    out = _final_norm(xs, final_norm, nl=nl)
    return out.reshape(n_batch, seq, d)
```

```python
import functools
import math

import numpy as np
import jax
import jax.numpy as jnp
from jax import lax
from jax.experimental import pallas as pl
from jax.experimental.pallas import tpu as pltpu

F32 = jnp.float32
MXU_DTYPE = jnp.bfloat16

GRID_W = 64
EPS = 1e-6
N_MOD = 6
NEG_INF = -1e30
HEAD_DIM = 128
NA_HEADS = 8
NA_WIN_H = 8
NA_WIN_W = 16
MLA_HEADS = 8
MLA_NOPE = 128
MLA_ROPE = 64
MLA_V = 128
ROPE_BASE = 10000.0
SC_TAPS = 3
N_BRANCH = 3
PEER_HEADS = 8
PEER_NKEYS = 128
PEER_KEY_DIM = 256
PEER_TOPK = 16

V7X_VMEM_BYTES = 64 * 1024 * 1024
LANES = 128
SUBLANES = 8

ROW_BLOCK = 512
MLA_HEAD_PAD = 256
NA_Q_ROWS = 8
NA_K_ROWS = 16
NA_K_CHUNK_ROWS = 4


def _cparams(n_axes, vmem_bytes):
    limit = min(int(vmem_bytes) + (4 << 20), V7X_VMEM_BYTES - (6 << 20))
    return pltpu.CompilerParams(dimension_semantics=("arbitrary",) * n_axes,
                                vmem_limit_bytes=limit)


def _nbytes(shape, dtype):
    return int(np.prod(shape)) * jnp.dtype(dtype).itemsize


def _tile(n, prefs):
    return next(t for t in prefs if n % t == 0)


def _mod_row(row0, nl, seq, n_batch):
    return jnp.where(row0 >= nl, n_batch, row0 // seq)


def _ada_kernel(c_ref, w_ref, b_ref, o_ref):
    c = c_ref[...]
    s = (c * jax.nn.sigmoid(c)).astype(MXU_DTYPE)
    acc = jnp.dot(s, w_ref[...].astype(MXU_DTYPE), preferred_element_type=F32)
    o_ref[...] = acc + b_ref[...]


def _ada_table(cmat, w_ada, b_ada, layer):
    rows, d = cmat.shape
    n = w_ada.shape[2]
    tn = 512
    vm = 2 * (_nbytes((d, tn), F32) + _nbytes((rows, tn), F32) * 2) + _nbytes((rows, d), F32) * 2 \
        + _nbytes((d, tn), MXU_DTYPE)
    return pl.pallas_call(
        _ada_kernel, name="ada_table",
        out_shape=jax.ShapeDtypeStruct((rows, n), F32),
        grid=(n // tn,),
        in_specs=[pl.BlockSpec((rows, d), lambda j: (0, 0)),
                  pl.BlockSpec((None, d, tn), lambda j: (layer, 0, j)),
                  pl.BlockSpec((None, 1, tn), lambda j: (layer, 0, j))],
        out_specs=pl.BlockSpec((rows, tn), lambda j: (0, j)),
        compiler_params=_cparams(1, vm),
    )(cmat, w_ada, b_ada)


def _normmod_kernel(x_ref, g_ref, sh_ref, sc_ref, o_ref):
    x = x_ref[...]
    y = x * lax.rsqrt(jnp.mean(x * x, axis=-1, keepdims=True) + EPS)
    o_ref[...] = ((y * g_ref[...]) * (1.0 + sc_ref[0]) + sh_ref[0]).astype(o_ref.dtype)


def _normmod(x, gain, modtab, m_shift, *, nl, seq, n_batch, skip):
    n, d = x.shape
    tb = 256
    off = skip * (ROW_BLOCK // tb)

    def mrow(i, m):
        return (_mod_row(i * tb, nl, seq, n_batch) * N_MOD + m, 0, 0)

    vm = 2 * (_nbytes((tb, d), F32) + _nbytes((tb, d), MXU_DTYPE)) + 3 * _nbytes((tb, d), F32)
    return pl.pallas_call(
        _normmod_kernel, name="normmod",
        out_shape=jax.ShapeDtypeStruct((n, d), MXU_DTYPE),
        grid=(n // tb - off,),
        in_specs=[pl.BlockSpec((tb, d), lambda i: (i, 0)),
                  pl.BlockSpec((1, d), lambda i: (0, 0)),
                  pl.BlockSpec((1, 1, d), lambda i: mrow(i, m_shift)),
                  pl.BlockSpec((1, 1, d), lambda i: mrow(i, m_shift + 1))],
        out_specs=pl.BlockSpec((tb, d), lambda i: (i, 0)),
        compiler_params=_cparams(1, vm),
    )(x, gain.reshape(1, d), modtab, modtab)


def _mm_kernel(x_ref, w_ref, o_ref):
    o_ref[...] = jnp.dot(x_ref[...], w_ref[...], preferred_element_type=F32).astype(o_ref.dtype)


def _matmul(x, w, layer, *, tn, out_dtype, skip=0, name="matmul"):
    n, k = x.shape
    nn = w.shape[2]
    tm = ROW_BLOCK
    vm = 2 * (_nbytes((tm, k), x.dtype) + _nbytes((k, tn), w.dtype) + _nbytes((tm, tn), out_dtype)) \
        + _nbytes((tm, tn), F32)
    return pl.pallas_call(
        _mm_kernel, name=name,
        out_shape=jax.ShapeDtypeStruct((n, nn), out_dtype),
        grid=(nn // tn, n // tm - skip),
        in_specs=[pl.BlockSpec((tm, k), lambda j, i: (i, 0)),
                  pl.BlockSpec((None, k, tn), lambda j, i: (layer, 0, j))],
        out_specs=pl.BlockSpec((tm, tn), lambda j, i: (i, j)),
        compiler_params=_cparams(2, vm),
    )(x, w)


def _mm_resid_kernel(x_ref, w_ref, r_ref, g_ref, o_ref, acc_ref):
    k = pl.program_id(2)

    @pl.when(k == 0)
    def _():
        acc_ref[...] = jnp.zeros_like(acc_ref)

    acc_ref[...] += jnp.dot(x_ref[...], w_ref[...], preferred_element_type=F32)

    @pl.when(k == pl.num_programs(2) - 1)
    def _():
        o_ref[...] = r_ref[...] + g_ref[0] * acc_ref[...]


def _matmul_resid(x, w, layer, resid, modtab, m_gate, *, tn, tk, nl, seq, n_batch, skip, name):
    n, kk = x.shape
    nn = w.shape[2]
    tm = ROW_BLOCK

    def mrow(i, j, k):
        return (_mod_row(i * tm, nl, seq, n_batch) * N_MOD + m_gate, 0, j)

    vm = 2 * (_nbytes((tm, tk), x.dtype) + _nbytes((tk, tn), w.dtype) + 2 * _nbytes((tm, tn), F32)) \
        + 2 * _nbytes((tm, tn), F32)
    return pl.pallas_call(
        _mm_resid_kernel, name=name,
        out_shape=jax.ShapeDtypeStruct((n, nn), F32),
        grid=(n // tm - skip, nn // tn, kk // tk),
        in_specs=[pl.BlockSpec((tm, tk), lambda i, j, k: (i, k)),
                  pl.BlockSpec((None, tk, tn), lambda i, j, k: (layer, k, j)),
                  pl.BlockSpec((tm, tn), lambda i, j, k: (i, j)),
                  pl.BlockSpec((1, 1, tn), mrow)],
        out_specs=pl.BlockSpec((tm, tn), lambda i, j, k: (i, j)),
        scratch_shapes=[pltpu.VMEM((tm, tn), F32)],
        compiler_params=_cparams(3, vm),
    )(x, w, resid, modtab)


def _na_bias_tables(rows):
    n_dr = 2 * NA_WIN_H - 1
    n_blk = rows // NA_Q_ROWS
    rl = np.arange(NA_Q_ROWS)[:, None]
    krl = np.arange(NA_K_ROWS)[None, :]
    cases, case_of_blk = [], []
    for i in range(n_blk):
        r0 = i * NA_Q_ROWS
        k0 = int(np.clip(r0 - NA_K_CHUNK_ROWS, 0, rows - NA_K_ROWS))
        r = r0 + rl
        kr = k0 + krl
        start = np.clip(r - NA_WIN_H // 2, 0, rows - NA_WIN_H)
        ok = (kr >= start) & (kr < start + NA_WIN_H)
        dr = np.where(ok, kr - r + NA_WIN_H - 1, n_dr).astype(np.int32)
        assert dr.min() >= 0
        for ci, known in enumerate(cases):
            if np.array_equal(known, dr):
                case_of_blk.append(ci)
                break
        else:
            case_of_blk.append(len(cases))
            cases.append(dr)
    c = np.arange(GRID_W)[:, None]
    kc = np.arange(GRID_W)[None, :]
    ws = np.clip(c - NA_WIN_W // 2, 0, GRID_W - NA_WIN_W)
    col_ok = (kc >= ws) & (kc < ws + NA_WIN_W)
    dc = np.clip(kc - c + NA_WIN_W - 1, 0, 2 * NA_WIN_W - 2)
    dc_hot = (dc[None] == np.arange(2 * NA_WIN_W - 1)[:, None, None]).astype(np.float32)
    return np.stack(cases), np.asarray(case_of_blk, np.int32), dc_hot, col_ok


def _na_bias(rpb, dr_of, dc_hot, col_ok):
    n_case = dr_of.shape[0]
    n_h = rpb.shape[0]
    tb = jnp.einsum('hdk,kcq->hdcq', rpb.astype(F32), dc_hot, precision=lax.Precision.HIGHEST)
    tb = jnp.where(col_ok, tb, NEG_INF)
    tb = jnp.concatenate([tb, jnp.full((n_h, 1, GRID_W, GRID_W), NEG_INF, F32)], axis=1)
    rows = jnp.take(tb.transpose(1, 0, 2, 3).reshape(tb.shape[1], -1), dr_of.reshape(-1), axis=0)
    b = rows.reshape(n_case, NA_Q_ROWS, NA_K_ROWS, n_h, GRID_W, GRID_W).transpose(0, 3, 1, 4, 2, 5)
    return b.reshape(n_case, n_h, NA_Q_ROWS * GRID_W, NA_K_ROWS * GRID_W)


def _na_kernel(case_ref, q_ref, k0_ref, k1_ref, k2_ref, k3_ref, v0_ref, v1_ref, v2_ref, v3_ref,
               kc_ref, vc_ref, b_ref, o_ref):
    del case_ref
    scale = HEAD_DIM ** -0.5
    nt = (((1,), (1,)), ((), ()))
    q = q_ref[...]
    k_loc = jnp.concatenate([k0_ref[...], k1_ref[...], k2_ref[...], k3_ref[...]], axis=0)
    v_loc = jnp.concatenate([v0_ref[...], v1_ref[...], v2_ref[...], v3_ref[...]], axis=0)
    s_loc = lax.dot_general(q, k_loc, nt, preferred_element_type=F32) * scale + b_ref[0, 0]
    s_ctx = lax.dot_general(q, kc_ref[...], nt, preferred_element_type=F32) * scale
    m = jnp.maximum(jnp.max(s_loc, axis=-1, keepdims=True), jnp.max(s_ctx, axis=-1, keepdims=True))
    p_loc = jnp.exp(s_loc - m)
    p_ctx = jnp.exp(s_ctx - m)
    denom = jnp.sum(p_loc, axis=-1, keepdims=True) + jnp.sum(p_ctx, axis=-1, keepdims=True)
    o = (jnp.dot(p_loc.astype(MXU_DTYPE), v_loc, preferred_element_type=F32)
         + jnp.dot(p_ctx.astype(MXU_DTYPE), vc_ref[...], preferred_element_type=F32))
    o_ref[...] = (o / denom).astype(o_ref.dtype)


def _na_attention(proj, bias, case_of_blk, *, nl, seq, lc, n_batch, col_q, col_k, col_v):
    n = proj.shape[0]
    rows = seq // GRID_W
    tq = NA_Q_ROWS * GRID_W
    tkc = NA_K_CHUNK_ROWS * GRID_W
    n_blk = rows // NA_Q_ROWS
    n_chunk = NA_K_ROWS // NA_K_CHUNK_ROWS
    max_c0 = rows // NA_K_CHUNK_ROWS - n_chunk

    def qmap(h, b, i, case):
        return (b * (seq // tq) + i, col_q + h)

    def kvmap(col, j):
        def f(h, b, i, case):
            c0 = jnp.clip(i * (NA_Q_ROWS // NA_K_CHUNK_ROWS) - 1, 0, max_c0)
            return (b * (seq // tkc) + c0 + j, col + h)
        return f

    def ctxmap(col):
        return lambda h, b, i, case: (nl // lc + b, col + h)

    in_specs = [pl.BlockSpec((tq, HEAD_DIM), qmap)]
    in_specs += [pl.BlockSpec((tkc, HEAD_DIM), kvmap(col_k, j)) for j in range(n_chunk)]
    in_specs += [pl.BlockSpec((tkc, HEAD_DIM), kvmap(col_v, j)) for j in range(n_chunk)]
    in_specs += [pl.BlockSpec((lc, HEAD_DIM), ctxmap(col_k)), pl.BlockSpec((lc, HEAD_DIM), ctxmap(col_v))]
    in_specs += [pl.BlockSpec((1, 1, tq, NA_K_ROWS * GRID_W), lambda h, b, i, case: (case[i], h, 0, 0))]
    vm = 2 * (_nbytes((tq, NA_K_ROWS * GRID_W), F32) + 12 * _nbytes((tq, HEAD_DIM), MXU_DTYPE)) \
        + 6 * _nbytes((tq, NA_K_ROWS * GRID_W + lc), F32)
    grid_spec = pltpu.PrefetchScalarGridSpec(
        num_scalar_prefetch=1, grid=(NA_HEADS, n_batch, n_blk),
        in_specs=in_specs,
        out_specs=pl.BlockSpec((tq, HEAD_DIM), lambda h, b, i, case: (b * (seq // tq) + i, h)))
    return pl.pallas_call(
        _na_kernel, name="na_attention",
        out_shape=jax.ShapeDtypeStruct((n, NA_HEADS * HEAD_DIM), MXU_DTYPE),
        grid_spec=grid_spec,
        compiler_params=_cparams(3, vm),
    )(case_of_blk, *([proj] * (1 + 2 * n_chunk + 2)), bias)


def _ctx_attn_kernel(q_ref, k_ref, v_ref, prev_ref, o_ref, *, scale):
    del prev_ref
    s = lax.dot_general(q_ref[...], k_ref[...], (((1,), (1,)), ((), ())),
                        preferred_element_type=F32) * scale
    m = jnp.max(s, axis=-1, keepdims=True)
    p = jnp.exp(s - m)
    o = jnp.dot(p.astype(MXU_DTYPE), v_ref[...], preferred_element_type=F32)
    o_ref[...] = (o / jnp.sum(p, axis=-1, keepdims=True)).astype(o_ref.dtype)


def _ctx_attention(qa, ka, va, y, *, nl, lc, n_batch, n_heads, dk, dv, col_q, col_k, col_v, scale):
    vm = 2 * (2 * _nbytes((lc, dk), MXU_DTYPE) + 2 * _nbytes((lc, dv), MXU_DTYPE)) + 4 * _nbytes((lc, lc), F32)
    return pl.pallas_call(
        functools.partial(_ctx_attn_kernel, scale=scale), name="ctx_attention",
        out_shape=jax.ShapeDtypeStruct(y.shape, y.dtype),
        grid=(n_batch, n_heads),
        in_specs=[pl.BlockSpec((lc, dk), lambda b, h: (nl // lc + b, col_q + h)),
                  pl.BlockSpec((lc, dk), lambda b, h: (nl // lc + b, col_k + h)),
                  pl.BlockSpec((lc, dv), lambda b, h: (nl // lc + b, col_v + h)),
                  pl.BlockSpec(memory_space=pl.ANY)],
        out_specs=pl.BlockSpec((lc, dv), lambda b, h: (nl // lc + b, h)),
        input_output_aliases={3: 0},
        compiler_params=_cparams(2, vm),
    )(qa, ka, va, y)


def _mla_up_kernel(cq_ref, ckv_ref, kr_ref, qn_ref, kvn_ref, wq_ref, wk_ref, wv_ref, cos_ref, sin_ref,
                   q_ref, k_ref, v_ref):
    def rms(x, g):
        return (x * lax.rsqrt(jnp.mean(x * x, axis=-1, keepdims=True) + EPS) * g).astype(MXU_DTYPE)

    cos = cos_ref[...]
    sin = sin_ref[...]

    def rope(x):
        return x * cos + pltpu.roll(x, 2 * (MLA_ROPE // 2), axis=1) * sin

    cqn = rms(cq_ref[...].astype(F32), qn_ref[...])
    kvn = rms(ckv_ref[...].astype(F32), kvn_ref[...])
    q = jnp.dot(cqn, wq_ref[...], preferred_element_type=F32)
    kn = jnp.dot(kvn, wk_ref[...], preferred_element_type=F32)
    v_ref[...] = jnp.dot(kvn, wv_ref[...], preferred_element_type=F32).astype(v_ref.dtype)
    kr = rope(kr_ref[...].astype(F32)).astype(k_ref.dtype)
    for h in range(MLA_HEADS):
        c0 = h * MLA_HEAD_PAD
        q_ref[:, c0:c0 + MLA_NOPE] = q[:, c0:c0 + MLA_NOPE].astype(q_ref.dtype)
        q_ref[:, c0 + MLA_NOPE:c0 + MLA_HEAD_PAD] = rope(q[:, c0 + MLA_NOPE:c0 + MLA_HEAD_PAD]).astype(q_ref.dtype)
        k_ref[:, c0:c0 + MLA_NOPE] = kn[:, h * MLA_NOPE:(h + 1) * MLA_NOPE].astype(k_ref.dtype)
        k_ref[:, c0 + MLA_NOPE:c0 + MLA_HEAD_PAD] = kr


def _mla_up(proj, q_norm, kv_norm, wq, wk, wv, cos_t, sin_t, *, col_cq, col_ckv, col_kr):
    n = proj.shape[0]
    tm = 256
    rq, rkv = wq.shape[0], wk.shape[0]
    wq_cols = wq.shape[1]
    vm = 2 * (_nbytes(wq.shape, MXU_DTYPE) + _nbytes(wk.shape, MXU_DTYPE) + _nbytes(wv.shape, MXU_DTYPE)
              + _nbytes((tm, rq + rkv + LANES), MXU_DTYPE) + 2 * _nbytes((tm, LANES), F32)
              + _nbytes((tm, 2 * wq_cols + wv.shape[1]), MXU_DTYPE)) + 3 * _nbytes((tm, wq_cols), F32)
    return pl.pallas_call(
        _mla_up_kernel, name="mla_up",
        out_shape=(jax.ShapeDtypeStruct((n, wq_cols), MXU_DTYPE),
                   jax.ShapeDtypeStruct((n, wq_cols), MXU_DTYPE),
                   jax.ShapeDtypeStruct((n, wv.shape[1]), MXU_DTYPE)),
        grid=(n // tm,),
        in_specs=[pl.BlockSpec((tm, rq), lambda i: (i, col_cq)),
                  pl.BlockSpec((tm, rkv), lambda i: (i, col_ckv)),
                  pl.BlockSpec((tm, LANES), lambda i: (i, col_kr)),
                  pl.BlockSpec((1, rq), lambda i: (0, 0)),
                  pl.BlockSpec((1, rkv), lambda i: (0, 0)),
                  pl.BlockSpec(wq.shape, lambda i: (0, 0)),
                  pl.BlockSpec(wk.shape, lambda i: (0, 0)),
                  pl.BlockSpec(wv.shape, lambda i: (0, 0)),
                  pl.BlockSpec((tm, LANES), lambda i: (i, 0)),
                  pl.BlockSpec((tm, LANES), lambda i: (i, 0))],
        out_specs=(pl.BlockSpec((tm, wq_cols), lambda i: (i, 0)),
                   pl.BlockSpec((tm, wq_cols), lambda i: (i, 0)),
                   pl.BlockSpec((tm, wv.shape[1]), lambda i: (i, 0))),
        compiler_params=_cparams(1, vm),
    )(proj, proj, proj, q_norm.reshape(1, rq), kv_norm.reshape(1, rkv), wq, wk, wv, cos_t, sin_t)


def _flash_kernel(q_ref, kc_ref, kl_ref, vc_ref, vl_ref, o_ref, m_ref, l_ref, acc_ref, *, scale, tk):
    nt = (((1,), (1,)), ((), ()))
    q = q_ref[...]
    c = scale * math.log2(math.e)

    def chunk(k, v):
        s = lax.dot_general(q, k, nt, preferred_element_type=F32)
        m_old = m_ref[...]
        m_new = jnp.maximum(m_old, jnp.max(s, axis=-1, keepdims=True) * c)
        alpha = jnp.exp2(m_old - m_new)
        p = jnp.exp2(s * c - jnp.tile(m_new, (1, s.shape[1] // LANES)))
        l_ref[...] = alpha * l_ref[...] + jnp.sum(p, axis=-1, keepdims=True)
        acc_ref[...] = alpha * acc_ref[...] + jnp.dot(p.astype(MXU_DTYPE), v, preferred_element_type=F32)
        m_ref[...] = m_new

    m_ref[...] = jnp.full_like(m_ref, -jnp.inf)
    l_ref[...] = jnp.zeros_like(l_ref)
    acc_ref[...] = jnp.zeros_like(acc_ref)
    chunk(kc_ref[...], vc_ref[...])

    def body(j, carry):
        r0 = pl.multiple_of(j * tk, tk)
        chunk(kl_ref[pl.ds(r0, tk), :], vl_ref[pl.ds(r0, tk), :])
        return carry

    lax.fori_loop(0, kl_ref.shape[0] // tk, body, 0)
    o_ref[...] = (acc_ref[...] / l_ref[...]).astype(o_ref.dtype)


def _mla_attention(qf, kf, vf, *, nl, seq, lc, n_batch):
    n = qf.shape[0]
    tq, tk = _tile(seq, (1024, 512)), 512
    dk, dv = MLA_HEAD_PAD, MLA_V
    assert dv == LANES
    scale = (MLA_NOPE + MLA_ROPE) ** -0.5
    vm = 2 * (_nbytes((tq, dk), MXU_DTYPE) + _nbytes((lc + seq, dk + dv), MXU_DTYPE) + _nbytes((tq, dv), MXU_DTYPE)) \
        + 6 * _nbytes((tq, tk), F32) + 2 * _nbytes((tq, dv), F32)
    return pl.pallas_call(
        functools.partial(_flash_kernel, scale=scale, tk=tk), name="mla_attention",
        out_shape=jax.ShapeDtypeStruct((n, MLA_HEADS * dv), MXU_DTYPE),
        grid=(n_batch, MLA_HEADS, seq // tq),
        in_specs=[pl.BlockSpec((tq, dk), lambda b, h, i: (b * (seq // tq) + i, h)),
                  pl.BlockSpec((lc, dk), lambda b, h, i: (nl // lc + b, h)),
                  pl.BlockSpec((seq, dk), lambda b, h, i: (b, h)),
                  pl.BlockSpec((lc, dv), lambda b, h, i: (nl // lc + b, h)),
                  pl.BlockSpec((seq, dv), lambda b, h, i: (b, h))],
        out_specs=pl.BlockSpec((tq, dv), lambda b, h, i: (b * (seq // tq) + i, h)),
        scratch_shapes=[pltpu.VMEM((tq, LANES), F32), pltpu.VMEM((tq, LANES), F32), pltpu.VMEM((tq, dv), F32)],
        compiler_params=_cparams(3, vm),
    )(qf, kf, kf, vf, vf)


def _merge_kernel(ya_ref, yb_ref, gb_ref, gc_ref, xin_ref, gcp_ref, xinp_ref, gcn_ref, xinn_ref, cw_ref,
                  wa_ref, wb_ref, wc_ref, g0_ref, g1_ref, g2_ref, o_ref, yc_ref, *, nl, seq, lc):
    i = pl.program_id(0)
    j = pl.program_id(1)
    tm = ya_ref.shape[0]

    @pl.when(j == 0)
    def _():
        u = gc_ref[...].astype(F32) * xin_ref[...].astype(F32)
        u_prev_halo = (gcp_ref[...].astype(F32) * xinp_ref[...].astype(F32))[SUBLANES - 1:SUBLANES, :]
        u_next_halo = (gcn_ref[...].astype(F32) * xinn_ref[...].astype(F32))[0:1, :]
        row = lax.broadcasted_iota(jnp.int32, (tm, 1), 0)
        g = i * tm + row
        pos = jnp.where(g < nl, g % seq, (g - nl) % lc)
        seq_len = jnp.where(g < nl, seq, lc)
        u_dn = jnp.where(row == 0, u_prev_halo, pltpu.roll(u, 1, axis=0))
        u_up = jnp.where(row == tm - 1, u_next_halo, pltpu.roll(u, tm - 1, axis=0))
        u_dn = jnp.where(pos == 0, 0.0, u_dn)
        u_up = jnp.where(pos == seq_len - 1, 0.0, u_up)
        cw = cw_ref[...]
        y = cw[0:1, :] * u_dn + cw[1:2, :] * u + cw[2:3, :] * u_up
        yc_ref[...] = (gb_ref[...].astype(F32) * y).astype(yc_ref.dtype)

    def branch(y, w_ref, g_ref):
        gate = jax.nn.sigmoid(g_ref[...].astype(F32))
        return gate * jnp.dot(y, w_ref[...], preferred_element_type=F32)

    m = (branch(ya_ref[...], wa_ref, g0_ref) + branch(yb_ref[...], wb_ref, g1_ref)
         + branch(yc_ref[...], wc_ref, g2_ref))
    o_ref[...] = m.astype(o_ref.dtype)


def _merge(ya, yb, proj, conv_w, wa, wb, wc, layer, *, d, nl, seq, lc, skip, col_gb, col_gc, col_xin, col_gate):
    n = ya.shape[0]
    tm, tn = ROW_BLOCK, 1024 if d % 1024 == 0 else d
    w = ya.shape[1]
    hb = tm // SUBLANES
    last_halo = n // SUBLANES - 1
    n_col = d // tn

    def row(i, j):
        return (i, 0)

    def cur(col):
        return lambda i, j: (i, col)

    def prev(col):
        return lambda i, j: (jnp.maximum(i * hb - 1, 0), col)

    def nxt(col):
        return lambda i, j: (jnp.minimum((i + 1) * hb, last_halo), col)

    def gate(br):
        return lambda i, j: (i, col_gate + br * n_col + j)

    in_specs = [pl.BlockSpec((tm, w), row), pl.BlockSpec((tm, w), row),
                pl.BlockSpec((tm, w), cur(col_gb)), pl.BlockSpec((tm, w), cur(col_gc)),
                pl.BlockSpec((tm, w), cur(col_xin)),
                pl.BlockSpec((SUBLANES, w), prev(col_gc)), pl.BlockSpec((SUBLANES, w), prev(col_xin)),
                pl.BlockSpec((SUBLANES, w), nxt(col_gc)), pl.BlockSpec((SUBLANES, w), nxt(col_xin)),
                pl.BlockSpec((None, SC_TAPS, w), lambda i, j: (layer, 0, 0)),
                pl.BlockSpec((None, w, tn), lambda i, j: (layer, 0, j)),
                pl.BlockSpec((None, w, tn), lambda i, j: (layer, 0, j)),
                pl.BlockSpec((None, w, tn), lambda i, j: (layer, 0, j)),
                pl.BlockSpec((tm, tn), gate(0)), pl.BlockSpec((tm, tn), gate(1)), pl.BlockSpec((tm, tn), gate(2))]
    vm = 2 * (5 * _nbytes((tm, w), MXU_DTYPE) + 3 * _nbytes((w, tn), MXU_DTYPE) + 4 * _nbytes((tm, tn), MXU_DTYPE)) \
        + _nbytes((tm, w), MXU_DTYPE) + 6 * _nbytes((tm, max(w, tn)), F32)
    return pl.pallas_call(
        functools.partial(_merge_kernel, nl=nl, seq=seq, lc=lc), name="conv_merge",
        out_shape=jax.ShapeDtypeStruct((n, d), MXU_DTYPE),
        grid=(n // tm - skip, n_col),
        in_specs=in_specs,
        out_specs=pl.BlockSpec((tm, tn), lambda i, j: (i, j)),
        scratch_shapes=[pltpu.VMEM((tm, w), MXU_DTYPE)],
        compiler_params=_cparams(2, vm),
    )(ya, yb, proj, proj, proj, proj, proj, proj, proj, conv_w, wa, wb, wc, proj, proj, proj)


def _topk_cols(s, k, ids=None):
    if ids is None:
        ids = lax.broadcasted_iota(jnp.int32, s.shape, 0)
    big = jnp.iinfo(jnp.int32).max
    vals, idxs = [], []
    for _ in range(k):
        m = jnp.max(s, axis=0, keepdims=True)
        idx = jnp.min(jnp.where(s == m, ids, big), axis=0, keepdims=True)
        vals.append(m)
        idxs.append(idx)
        s = jnp.where(ids == idx, -jnp.inf, s)
    return jnp.concatenate(vals, axis=0), jnp.concatenate(idxs, axis=0)


def _candidate_rows():
    k = PEER_TOPK
    groups = []
    for b0 in range(0, k, SUBLANES):
        groups.append((0, 1, b0, b0 + SUBLANES))
    a = 1
    while a < k and k // (a + 1) > 1:
        assert k // (a + 1) <= SUBLANES
        groups.append((a, a + 1, 0, SUBLANES))
        a += 1
    while a < k:
        groups.append((a, a + SUBLANES, 0, 1))
        a += SUBLANES
    covered = {(aa, bb) for a0, a1, b0, b1 in groups for aa in range(a0, a1) for bb in range(b0, b1)}
    assert all((aa, bb) in covered for aa in range(k) for bb in range(k) if (aa + 1) * (bb + 1) <= k)
    return groups


def _take_rows(table, idx):
    iota = lax.broadcasted_iota(jnp.int32, table.shape, 0)
    rows = [jnp.sum(jnp.where(iota == idx[j:j + 1, :], table, 0), axis=0, keepdims=True)
            for j in range(idx.shape[0])]
    return jnp.concatenate(rows, axis=0)


def _peer_topk_kernel(q_ref, keys_ref, g_ref, i1_ref, i2_ref):
    nt = (((1,), (1,)), ((), ()))
    half = PEER_KEY_DIM // 2
    q = q_ref[...]
    s1 = lax.dot_general(keys_ref[0, :PEER_NKEYS, :], q[:, :half], nt, preferred_element_type=F32)
    s2 = lax.dot_general(keys_ref[0, PEER_NKEYS:, :], q[:, half:], nt, preferred_element_type=F32)
    v1, j1 = _topk_cols(s1, PEER_TOPK)
    v2, j2 = _topk_cols(s2, PEER_TOPK)
    sub = lax.broadcasted_iota(jnp.int32, (SUBLANES, q.shape[0]), 0)
    cand, flat = [], []
    for a0, a1, b0, b1 in _candidate_rows():
        cand.append(v1[a0:a1, :] + v2[b0:b1, :])
        flat.append(sub * (1 if a1 - a0 == 1 else PEER_TOPK) + (a0 * PEER_TOPK + b0))
    top_s, pos = _topk_cols(jnp.concatenate(cand, axis=0), PEER_TOPK, jnp.concatenate(flat, axis=0))
    i1_ref[...] = _take_rows(j1, jnp.right_shift(pos, PEER_TOPK.bit_length() - 1))
    i2_ref[...] = _take_rows(j2, jnp.bitwise_and(pos, PEER_TOPK - 1))
    e = jnp.exp(top_s - top_s[0:1, :])
    g_ref[...] = e / jnp.sum(e, axis=0, keepdims=True)


def _peer_topk(q, keys, *, skip):
    n = q.shape[0]
    t = 256
    off = skip * (ROW_BLOCK // t)
    rows = PEER_HEADS * PEER_TOPK
    out_spec = pl.BlockSpec((PEER_TOPK, t), lambda i, h: (h, i))
    vm = 2 * (_nbytes((t, PEER_KEY_DIM), MXU_DTYPE) + _nbytes(keys.shape[1:], MXU_DTYPE)) \
        + 8 * _nbytes((PEER_TOPK * PEER_TOPK, t), F32)
    return pl.pallas_call(
        _peer_topk_kernel, name="peer_topk",
        out_shape=(jax.ShapeDtypeStruct((rows, n), F32), jax.ShapeDtypeStruct((rows, n), jnp.int32),
                   jax.ShapeDtypeStruct((rows, n), jnp.int32)),
        grid=(n // t - off, PEER_HEADS),
        in_specs=[pl.BlockSpec((t, PEER_KEY_DIM), lambda i, h: (i, h)),
                  pl.BlockSpec((1,) + keys.shape[1:], lambda i, h: (h, 0, 0))],
        out_specs=(out_spec, out_spec, out_spec),
        compiler_params=_cparams(2, vm),
    )(q, keys)


def _peer_gates_kernel(i1_ref, i2_ref, g_ref, o_ref):
    n_sel = i1_ref.shape[1]
    iota = lax.broadcasted_iota(jnp.int32, (PEER_NKEYS, n_sel), 0)
    nt = (((1,), (1,)), ((), ()))

    def body(t, carry):
        i1b = i1_ref[pl.ds(t, 1), :]
        i2b = i2_ref[pl.ds(t, 1), :]
        g = g_ref[pl.ds(t, 1), :]
        g_hi = g.astype(MXU_DTYPE).astype(F32)
        g_lo = g - g_hi
        m1 = iota == i1b
        lhs = jnp.concatenate([jnp.where(m1, g_hi, 0.0), jnp.where(m1, g_lo, 0.0)], axis=1).astype(MXU_DTYPE)
        hot2 = jnp.where(iota == i2b, 1.0, 0.0).astype(MXU_DTYPE)
        rhs = jnp.concatenate([hot2, hot2], axis=1)
        o_ref[t] = lax.dot_general(lhs, rhs, nt, preferred_element_type=F32)
        return carry

    lax.fori_loop(0, i1_ref.shape[0], body, 0, unroll=32)


def _peer_gates(i1, i2, g, *, skip):
    n, n_sel = i1.shape
    t = 128
    off = skip * (ROW_BLOCK // t)
    spec = pl.BlockSpec((t, n_sel), lambda i: (i, 0))
    vm = 2 * (3 * _nbytes((t, n_sel), F32) + _nbytes((t, PEER_NKEYS, PEER_NKEYS), F32))
    return pl.pallas_call(
        _peer_gates_kernel, name="peer_gates",
        out_shape=jax.ShapeDtypeStruct((n, PEER_NKEYS, PEER_NKEYS), F32),
        grid=(n // t - off,),
        in_specs=[spec, spec, spec],
        out_specs=pl.BlockSpec((t, PEER_NKEYS, PEER_NKEYS), lambda i: (i, 0, 0)),
        compiler_params=_cparams(1, vm),
    )(i1, i2, g)


def _peer_up_kernel(h_ref, u_ref, g_ref, o_ref):
    acc = lax.dot_general(h_ref[...], u_ref[...], (((1,), (1,)), ((), ())), preferred_element_type=F32)
    for jj in range(g_ref.shape[1]):
        a = acc[:, jj * PEER_NKEYS:(jj + 1) * PEER_NKEYS]
        act = 0.5 * a * (1.0 + lax.erf(a * (2.0 ** -0.5)))
        o_ref[:, jj * PEER_NKEYS:(jj + 1) * PEER_NKEYS] = (act * g_ref[:, jj, :]).astype(o_ref.dtype)


def _peer_up(h, u, layer, gmat, *, skip):
    n, d = h.shape
    e = u.shape[1]
    tm = ROW_BLOCK
    ti = SUBLANES
    tn = ti * PEER_NKEYS
    vm = 2 * (_nbytes((tm, d), MXU_DTYPE) + _nbytes((d, tn), MXU_DTYPE) + _nbytes((tm, ti, PEER_NKEYS), F32)
              + _nbytes((tm, tn), MXU_DTYPE)) + 3 * _nbytes((tm, tn), F32)
    return pl.pallas_call(
        _peer_up_kernel, name="peer_up",
        out_shape=jax.ShapeDtypeStruct((n, e), MXU_DTYPE),
        grid=(n // tm - skip, e // tn),
        in_specs=[pl.BlockSpec((tm, d), lambda i, j: (i, 0)),
                  pl.BlockSpec((None, tn, d), lambda i, j: (layer, j, 0)),
                  pl.BlockSpec((tm, ti, PEER_NKEYS), lambda i, j: (i, j, 0))],
        out_specs=pl.BlockSpec((tm, tn), lambda i, j: (i, j)),
        compiler_params=_cparams(2, vm),
    )(h, u, gmat)


def _final_norm_kernel(x_ref, g_ref, o_ref):
    x = x_ref[...]
    o_ref[...] = x * lax.rsqrt(jnp.mean(x * x, axis=-1, keepdims=True) + EPS) * g_ref[...]


def _final_norm(x, gain, *, nl):
    d = x.shape[1]
    tb = 256
    vm = 4 * _nbytes((tb, d), F32) + 2 * _nbytes((tb, d), F32)
    return pl.pallas_call(
        _final_norm_kernel, name="final_norm",
        out_shape=jax.ShapeDtypeStruct((nl, d), F32),
        grid=(nl // tb,),
        in_specs=[pl.BlockSpec((tb, d), lambda i: (i, 0)), pl.BlockSpec((1, d), lambda i: (0, 0))],
        out_specs=pl.BlockSpec((tb, d), lambda i: (i, 0)),
        compiler_params=_cparams(1, vm),
    )(x, gain.reshape(1, d))


def _regroup_kernel(w_ref, o_ref, *, copies, rope_src, rope_dst, zero_from, zero_to):
    for dst, src, width in copies:
        o_ref[:, dst:dst + width] = w_ref[:, src:src + width].astype(o_ref.dtype)
    half = MLA_ROPE // 2
    r = w_ref[:, rope_src:rope_src + MLA_ROPE].astype(o_ref.dtype)
    z = jnp.zeros((r.shape[0], LANES // 2 - half), o_ref.dtype)
    o_ref[:, rope_dst:rope_dst + LANES] = jnp.concatenate([r[:, :half], z, r[:, half:], z], axis=1)
    o_ref[:, zero_from:zero_to] = jnp.zeros((r.shape[0], zero_to - zero_from), o_ref.dtype)


def _regroup_in_proj(w_in, *, wide, kv_rank, d):
    n_layer, rows, cols = w_in.shape
    src_ckv = 4 * wide
    src_rope = src_ckv + kv_rank
    src_conv = src_rope + MLA_ROPE
    src_gate = src_conv + 3 * wide
    assert src_gate + N_BRANCH * d == cols
    dst_ckv = 7 * wide
    dst_rope = dst_ckv + kv_rank
    dst_gate = 8 * wide
    out_cols = dst_gate + N_BRANCH * d
    copies = ((0, 0, 4 * wide), (4 * wide, src_conv, 3 * wide), (dst_ckv, src_ckv, kv_rank),
              (dst_gate, src_gate, N_BRANCH * d))
    tr = 128
    vm = 2 * (_nbytes((tr, cols), F32) + _nbytes((tr, out_cols), MXU_DTYPE)) + _nbytes((tr, cols), F32)
    return pl.pallas_call(
        functools.partial(_regroup_kernel, copies=copies, rope_src=src_rope, rope_dst=dst_rope,
                          zero_from=dst_rope + LANES, zero_to=dst_gate),
        name="regroup_in_proj",
        out_shape=jax.ShapeDtypeStruct((n_layer, rows, out_cols), MXU_DTYPE),
        grid=(n_layer, rows // tr),
        in_specs=[pl.BlockSpec((None, tr, cols), lambda l, i: (l, i, 0))],
        out_specs=pl.BlockSpec((None, tr, out_cols), lambda l, i: (l, i, 0)),
        compiler_params=_cparams(2, vm),
    )(w_in)


def _rope_lane_layout(w):
    half = MLA_ROPE // 2
    z = jnp.zeros(w.shape[:-1] + (LANES // 2 - half,), w.dtype)
    return jnp.concatenate([w[..., :half], z, w[..., half:], z], axis=-1)


def _rope_tables(seq, nc):
    pos = jnp.arange(seq, dtype=jnp.int32)
    n_freq = MLA_ROPE // 4
    inv_freq = jnp.power(ROPE_BASE, -jnp.arange(n_freq, dtype=F32) / n_freq)
    row = (pos // GRID_W).astype(F32)
    col = (pos % GRID_W).astype(F32)
    ang = jnp.concatenate([row[:, None] * inv_freq, col[:, None] * inv_freq], axis=-1)
    cos, sin = jnp.cos(ang), jnp.sin(ang)
    cos_l = _rope_lane_layout(jnp.concatenate([cos, cos], axis=-1))
    sin_l = _rope_lane_layout(jnp.concatenate([-sin, sin], axis=-1))
    one = _rope_lane_layout(jnp.ones((nc, MLA_ROPE), F32))
    return one, jnp.zeros((nc, LANES), F32), cos_l, sin_l


def kernel(x, c, ctx, c_ctx, w_ada, b_ada, norm_mix, norm_ffn, w_in, na_rpb, mla_q_norm, mla_kv_norm, mla_w_uq, mla_w_ukv, conv_w, w_branch_a, w_branch_b, w_branch_c, w_out, peer_w_q, peer_sub_keys, peer_u, peer_v, final_norm):
    n_batch, seq, d = x.shape
    lc = ctx.shape[1]
    depth = w_ada.shape[0]
    nc = n_batch * lc
    nl = n_batch * seq
    na_w = NA_HEADS * HEAD_DIM
    q_rank, kv_rank = mla_w_uq.shape[1], mla_w_ukv.shape[1]
    sc_w = conv_w.shape[2]
    rows = seq // GRID_W
    assert nc == ROW_BLOCK and seq % ROW_BLOCK == 0 and d % LANES == 0
    assert rows % NA_Q_ROWS == 0 and rows >= NA_K_ROWS
    assert na_w == q_rank == sc_w == MLA_HEADS * MLA_V and kv_rank * 2 == na_w
    kw = dict(nl=nl, seq=seq, n_batch=n_batch)

    wide = na_w
    col_gate = 8 * wide
    proj_cols = col_gate + N_BRANCH * d

    cmat = jnp.zeros((SUBLANES, d), F32).at[:n_batch].set(c).at[n_batch].set(c_ctx)
    xs = jnp.concatenate([x.reshape(nl, d), ctx.reshape(nc, d)], axis=0)

    dr_of, case_of_blk, dc_hot, col_ok = _na_bias_tables(rows)
    one_c, zero_c, cos_l, sin_l = _rope_tables(seq, nc)
    cos_t = jnp.concatenate([cos_l] * n_batch + [one_c], axis=0)
    sin_t = jnp.concatenate([sin_l] * n_batch + [zero_c], axis=0)

    w_proj = _regroup_in_proj(w_in, wide=wide, kv_rank=kv_rank, d=d)
    b_ada3 = b_ada.reshape(depth, 1, N_MOD * d)
    wa_c, wb_c, wc_c = (w.astype(MXU_DTYPE) for w in (w_branch_a, w_branch_b, w_branch_c))
    w_out_c = w_out.astype(MXU_DTYPE)
    w_q_c = peer_w_q.astype(MXU_DTYPE)
    u_c = peer_u.astype(MXU_DTYPE)
    v_c = peer_v.astype(MXU_DTYPE)

    for l in range(depth):
        last = l == depth - 1
        skip = 1 if last else 0
        wq = mla_w_uq[l].reshape(q_rank, MLA_HEADS, MLA_NOPE + MLA_ROPE)
        wq = jnp.concatenate([wq[..., :MLA_NOPE], _rope_lane_layout(wq[..., MLA_NOPE:])], axis=-1)
        wq = wq.reshape(q_rank, MLA_HEADS * MLA_HEAD_PAD).astype(MXU_DTYPE)
        wkv = mla_w_ukv[l].reshape(kv_rank, MLA_HEADS, MLA_NOPE + MLA_V)
        wk = wkv[..., :MLA_NOPE].reshape(kv_rank, MLA_HEADS * MLA_NOPE).astype(MXU_DTYPE)
        wv = wkv[..., MLA_NOPE:].reshape(kv_rank, MLA_HEADS * MLA_V).astype(MXU_DTYPE)
        bias = _na_bias(na_rpb[l], dr_of, dc_hot, col_ok)

        mod = _ada_table(cmat, w_ada, b_ada3, l)
        modtab = mod[:n_batch + 1].reshape((n_batch + 1) * N_MOD, 1, d)

        h = _normmod(xs, norm_mix[l], modtab, 0, skip=0, **kw)
        proj = _matmul(h, w_proj, l, tn=_tile(proj_cols, (1024, 512, 256)), out_dtype=MXU_DTYPE, name="in_proj")
        hd = wide // HEAD_DIM
        ya = _na_attention(proj, bias, jnp.asarray(case_of_blk), lc=lc, col_q=0, col_k=hd, col_v=2 * hd, **kw)
        qf, kf, vf = _mla_up(proj, mla_q_norm[l], mla_kv_norm[l], wq, wk, wv, cos_t, sin_t,
                             col_cq=3, col_ckv=7 * wide // kv_rank, col_kr=(7 * wide + kv_rank) // LANES)
        yb = _mla_attention(qf, kf, vf, lc=lc, **kw)
        if not last:
            ya = _ctx_attention(proj, proj, proj, ya, nl=nl, lc=lc, n_batch=n_batch, n_heads=NA_HEADS, dk=HEAD_DIM,
                                dv=HEAD_DIM, col_q=0, col_k=hd, col_v=2 * hd, scale=HEAD_DIM ** -0.5)
            yb = _ctx_attention(qf, kf, vf, yb, nl=nl, lc=lc, n_batch=n_batch, n_heads=MLA_HEADS, dk=MLA_HEAD_PAD,
                                dv=MLA_V, col_q=0, col_k=0, col_v=0, scale=(MLA_NOPE + MLA_ROPE) ** -0.5)
        tn_m = 1024 if d % 1024 == 0 else d
        m = _merge(ya, yb, proj, conv_w, wa_c, wb_c, wc_c, l, d=d, nl=nl, seq=seq, lc=lc, skip=skip,
                   col_gb=4, col_gc=5, col_xin=6, col_gate=col_gate // tn_m)
        xs = _matmul_resid(m, w_out_c, l, xs, modtab, 2, tn=tn_m, tk=d, skip=skip, name="out_proj", **kw)

        h2 = _normmod(xs, norm_ffn[l], modtab, 3, skip=skip, **kw)
        pq = _matmul(h2, w_q_c, l, tn=1024, out_dtype=MXU_DTYPE, skip=skip, name="peer_query")
        keys = peer_sub_keys[l].reshape(PEER_HEADS, 2 * PEER_NKEYS, PEER_KEY_DIM // 2).astype(MXU_DTYPE)
        gates_t, i1_t, i2_t = _peer_topk(pq, keys, skip=skip)
        gmat = _peer_gates(i1_t.T, i2_t.T, gates_t.T, skip=skip)
        hexp = _peer_up(h2, u_c, l, gmat, skip=skip)
        xs = _matmul_resid(hexp, v_c, l, xs, modtab, 5, tn=tn_m, tk=4096, skip=skip, name="peer_down", **kw)

    out = _final_norm(xs, final_norm, nl=nl)
    return out.reshape(n_batch, seq, d)
```

```python
import functools
import math

import numpy as np
import jax
import jax.numpy as jnp
from jax import lax
from jax.experimental import pallas as pl
from jax.experimental.pallas import tpu as pltpu

F32 = jnp.float32
MXU_DTYPE = jnp.bfloat16

GRID_W = 64
EPS = 1e-6
N_MOD = 6
NEG_INF = -1e30
HEAD_DIM = 128
NA_HEADS = 8
NA_WIN_H = 8
NA_WIN_W = 16
MLA_HEADS = 8
MLA_NOPE = 128
MLA_ROPE = 64
MLA_V = 128
ROPE_BASE = 10000.0
SC_TAPS = 3
N_BRANCH = 3
PEER_HEADS = 8
PEER_NKEYS = 128
PEER_KEY_DIM = 256
PEER_TOPK = 16

V7X_VMEM_BYTES = 64 * 1024 * 1024
LANES = 128
SUBLANES = 8

ROW_BLOCK = 512
MLA_HEAD_PAD = 256
NA_Q_ROWS = 8
NA_K_ROWS = 16
NA_K_CHUNK_ROWS = 4


def _cparams(n_axes, vmem_bytes):
    limit = min(int(vmem_bytes) + (4 << 20), V7X_VMEM_BYTES - (6 << 20))
    return pltpu.CompilerParams(dimension_semantics=("arbitrary",) * n_axes,
                                vmem_limit_bytes=limit)


def _nbytes(shape, dtype):
    return int(np.prod(shape)) * jnp.dtype(dtype).itemsize


def _tile(n, prefs):
    return next(t for t in prefs if n % t == 0)


def _mod_row(row0, nl, seq, n_batch):
    return jnp.where(row0 >= nl, n_batch, row0 // seq)


def _ada_kernel(c_ref, w_ref, b_ref, o_ref):
    c = c_ref[...]
    s = (c * jax.nn.sigmoid(c)).astype(MXU_DTYPE)
    acc = jnp.dot(s, w_ref[...].astype(MXU_DTYPE), preferred_element_type=F32)
    o_ref[...] = acc + b_ref[...]


def _ada_table(cmat, w_ada, b_ada, layer):
    rows, d = cmat.shape
    n = w_ada.shape[2]
    tn = 512
    vm = 2 * (_nbytes((d, tn), F32) + _nbytes((rows, tn), F32) * 2) + _nbytes((rows, d), F32) * 2 \
        + _nbytes((d, tn), MXU_DTYPE)
    return pl.pallas_call(
        _ada_kernel, name="ada_table",
        out_shape=jax.ShapeDtypeStruct((rows, n), F32),
        grid=(n // tn,),
        in_specs=[pl.BlockSpec((rows, d), lambda j: (0, 0)),
                  pl.BlockSpec((None, d, tn), lambda j: (layer, 0, j)),
                  pl.BlockSpec((None, 1, tn), lambda j: (layer, 0, j))],
        out_specs=pl.BlockSpec((rows, tn), lambda j: (0, j)),
        compiler_params=_cparams(1, vm),
    )(cmat, w_ada, b_ada)


def _normmod_kernel(x_ref, g_ref, sh_ref, sc_ref, o_ref):
    x = x_ref[...]
    y = x * lax.rsqrt(jnp.mean(x * x, axis=-1, keepdims=True) + EPS)
    o_ref[...] = ((y * g_ref[...]) * (1.0 + sc_ref[0]) + sh_ref[0]).astype(o_ref.dtype)


def _normmod(x, gain, modtab, m_shift, *, nl, seq, n_batch, skip):
    n, d = x.shape
    tb = 256
    off = skip * (ROW_BLOCK // tb)

    def mrow(i, m):
        return (_mod_row(i * tb, nl, seq, n_batch) * N_MOD + m, 0, 0)

    vm = 2 * (_nbytes((tb, d), F32) + _nbytes((tb, d), MXU_DTYPE)) + 3 * _nbytes((tb, d), F32)
    return pl.pallas_call(
        _normmod_kernel, name="normmod",
        out_shape=jax.ShapeDtypeStruct((n, d), MXU_DTYPE),
        grid=(n // tb - off,),
        in_specs=[pl.BlockSpec((tb, d), lambda i: (i, 0)),
                  pl.BlockSpec((1, d), lambda i: (0, 0)),
                  pl.BlockSpec((1, 1, d), lambda i: mrow(i, m_shift)),
                  pl.BlockSpec((1, 1, d), lambda i: mrow(i, m_shift + 1))],
        out_specs=pl.BlockSpec((tb, d), lambda i: (i, 0)),
        compiler_params=_cparams(1, vm),
    )(x, gain.reshape(1, d), modtab, modtab)


def _mm_kernel(x_ref, w_ref, o_ref):
    o_ref[...] = jnp.dot(x_ref[...], w_ref[...], preferred_element_type=F32).astype(o_ref.dtype)


def _mm_nt_kernel(x_ref, w_ref, o_ref):
    o_ref[...] = lax.dot_general(x_ref[...], w_ref[...], (((1,), (1,)), ((), ())),
                                 preferred_element_type=F32).astype(o_ref.dtype)


def _matmul(x, w, layer, *, tn, out_dtype, skip=0, w_rows=False, name="matmul"):
    n, k = x.shape
    nn = w.shape[1] if w_rows else w.shape[2]
    tm = ROW_BLOCK
    vm = 2 * (_nbytes((tm, k), x.dtype) + _nbytes((k, tn), w.dtype) + _nbytes((tm, tn), out_dtype)) \
        + _nbytes((tm, tn), F32)
    w_spec = (pl.BlockSpec((None, tn, k), lambda j, i: (layer, j, 0)) if w_rows
              else pl.BlockSpec((None, k, tn), lambda j, i: (layer, 0, j)))
    return pl.pallas_call(
        _mm_nt_kernel if w_rows else _mm_kernel, name=name,
        out_shape=jax.ShapeDtypeStruct((n, nn), out_dtype),
        grid=(nn // tn, n // tm - skip),
        in_specs=[pl.BlockSpec((tm, k), lambda j, i: (i, 0)), w_spec],
        out_specs=pl.BlockSpec((tm, tn), lambda j, i: (i, j)),
        compiler_params=_cparams(2, vm),
    )(x, w)


def _mm_resid_kernel(x_ref, w_ref, r_ref, g_ref, o_ref, acc_ref):
    k = pl.program_id(2)

    @pl.when(k == 0)
    def _():
        acc_ref[...] = jnp.zeros_like(acc_ref)

    acc_ref[...] += jnp.dot(x_ref[...], w_ref[...], preferred_element_type=F32)

    @pl.when(k == pl.num_programs(2) - 1)
    def _():
        o_ref[...] = r_ref[...] + g_ref[0] * acc_ref[...]


def _matmul_resid(x, w, layer, resid, modtab, m_gate, *, tn, tk, nl, seq, n_batch, skip, name):
    n, kk = x.shape
    nn = w.shape[2]
    tm = ROW_BLOCK

    def mrow(i, j, k):
        return (_mod_row(i * tm, nl, seq, n_batch) * N_MOD + m_gate, 0, j)

    vm = 2 * (_nbytes((tm, tk), x.dtype) + _nbytes((tk, tn), w.dtype) + 2 * _nbytes((tm, tn), F32)) \
        + 2 * _nbytes((tm, tn), F32)
    return pl.pallas_call(
        _mm_resid_kernel, name=name,
        out_shape=jax.ShapeDtypeStruct((n, nn), F32),
        grid=(n // tm - skip, nn // tn, kk // tk),
        in_specs=[pl.BlockSpec((tm, tk), lambda i, j, k: (i, k)),
                  pl.BlockSpec((None, tk, tn), lambda i, j, k: (layer, k, j)),
                  pl.BlockSpec((tm, tn), lambda i, j, k: (i, j)),
                  pl.BlockSpec((1, 1, tn), mrow)],
        out_specs=pl.BlockSpec((tm, tn), lambda i, j, k: (i, j)),
        scratch_shapes=[pltpu.VMEM((tm, tn), F32)],
        compiler_params=_cparams(3, vm),
    )(x, w, resid, modtab)


def _na_bias_tables(rows):
    assert 2 * GRID_W == LANES and NA_K_ROWS % 2 == 0
    n_dr = 2 * NA_WIN_H - 1
    n_blk = rows // NA_Q_ROWS
    rl = np.arange(NA_Q_ROWS)[:, None]
    krl = np.arange(NA_K_ROWS)[None, :]
    cases, case_of_blk = [], []
    for i in range(n_blk):
        r0 = i * NA_Q_ROWS
        k0 = int(np.clip(r0 - NA_K_CHUNK_ROWS, 0, rows - NA_K_ROWS))
        r = r0 + rl
        kr = k0 + krl
        start = np.clip(r - NA_WIN_H // 2, 0, rows - NA_WIN_H)
        ok = (kr >= start) & (kr < start + NA_WIN_H)
        dr = np.where(ok, kr - r + NA_WIN_H - 1, n_dr).astype(np.int32)
        assert dr.min() >= 0
        for ci, known in enumerate(cases):
            if np.array_equal(known, dr):
                case_of_blk.append(ci)
                break
        else:
            case_of_blk.append(len(cases))
            cases.append(dr)
    c = np.arange(GRID_W)[:, None]
    kc = np.arange(GRID_W)[None, :]
    ws = np.clip(c - NA_WIN_W // 2, 0, GRID_W - NA_WIN_W)
    col_ok = (kc >= ws) & (kc < ws + NA_WIN_W)
    dc = np.clip(kc - c + NA_WIN_W - 1, 0, 2 * NA_WIN_W - 2)
    dc_hot = (dc[None] == np.arange(2 * NA_WIN_W - 1)[:, None, None]).astype(np.float32)
    dr_of = np.stack(cases).reshape(len(cases), NA_Q_ROWS, NA_K_ROWS // 2, 2)
    pairs, pair_of = np.unique(dr_of.reshape(-1, 2), axis=0, return_inverse=True)
    return (pairs.astype(np.int32), pair_of.reshape(-1).astype(np.int32), np.asarray(case_of_blk, np.int32),
            dc_hot, col_ok)


def _na_pair_table(rpb, pairs, dc_hot, col_ok):
    n_h = rpb.shape[0]
    tb = jnp.einsum('hdk,kcq->hdcq', rpb.astype(F32), dc_hot, precision=lax.Precision.HIGHEST)
    tb = jnp.where(col_ok, tb, NEG_INF)
    tb = jnp.concatenate([tb, jnp.full((n_h, 1, GRID_W, GRID_W), NEG_INF, F32)], axis=1)
    return jnp.concatenate([tb[:, pairs[:, 0]], tb[:, pairs[:, 1]]], axis=-1)


def _na_kernel(case_ref, pair_ref, q_ref, k0_ref, k1_ref, k2_ref, k3_ref, v0_ref, v1_ref, v2_ref, v3_ref,
               kc_ref, vc_ref, b_ref, o_ref):
    scale = HEAD_DIM ** -0.5
    nt = (((1,), (1,)), ((), ()))
    n_pair = NA_K_ROWS // 2
    q = q_ref[...]
    k_loc = jnp.concatenate([k0_ref[...], k1_ref[...], k2_ref[...], k3_ref[...]], axis=0)
    v_loc = jnp.concatenate([v0_ref[...], v1_ref[...], v2_ref[...], v3_ref[...]], axis=0)
    s_loc = lax.dot_general(q, k_loc, nt, preferred_element_type=F32) * scale
    base = case_ref[pl.program_id(2)] * (NA_Q_ROWS * n_pair)
    biased = []
    for rl in range(NA_Q_ROWS):
        blocks = [b_ref[0, pair_ref[base + rl * n_pair + p]] for p in range(n_pair)]
        biased.append(s_loc[rl * GRID_W:(rl + 1) * GRID_W, :] + jnp.concatenate(blocks, axis=1))
    s_loc = jnp.concatenate(biased, axis=0)
    s_ctx = lax.dot_general(q, kc_ref[...], nt, preferred_element_type=F32) * scale
    m = jnp.maximum(jnp.max(s_loc, axis=-1, keepdims=True), jnp.max(s_ctx, axis=-1, keepdims=True))
    p_loc = jnp.exp(s_loc - m)
    p_ctx = jnp.exp(s_ctx - m)
    denom = jnp.sum(p_loc, axis=-1, keepdims=True) + jnp.sum(p_ctx, axis=-1, keepdims=True)
    o = (jnp.dot(p_loc.astype(MXU_DTYPE), v_loc, preferred_element_type=F32)
         + jnp.dot(p_ctx.astype(MXU_DTYPE), vc_ref[...], preferred_element_type=F32))
    o_ref[...] = (o / denom).astype(o_ref.dtype)


def _na_attention(proj, pair_table, case_of_blk, pair_of, *, nl, seq, lc, n_batch, col_q, col_k, col_v):
    n = proj.shape[0]
    rows = seq // GRID_W
    tq = NA_Q_ROWS * GRID_W
    tkc = NA_K_CHUNK_ROWS * GRID_W
    n_blk = rows // NA_Q_ROWS
    n_chunk = NA_K_ROWS // NA_K_CHUNK_ROWS
    max_c0 = rows // NA_K_CHUNK_ROWS - n_chunk

    def qmap(h, b, i, case, pair):
        return (b * (seq // tq) + i, col_q + h)

    def kvmap(col, j):
        def f(h, b, i, case, pair):
            c0 = jnp.clip(i * (NA_Q_ROWS // NA_K_CHUNK_ROWS) - 1, 0, max_c0)
            return (b * (seq // tkc) + c0 + j, col + h)
        return f

    def ctxmap(col):
        return lambda h, b, i, case, pair: (nl // lc + b, col + h)

    in_specs = [pl.BlockSpec((tq, HEAD_DIM), qmap)]
    in_specs += [pl.BlockSpec((tkc, HEAD_DIM), kvmap(col_k, j)) for j in range(n_chunk)]
    in_specs += [pl.BlockSpec((tkc, HEAD_DIM), kvmap(col_v, j)) for j in range(n_chunk)]
    in_specs += [pl.BlockSpec((lc, HEAD_DIM), ctxmap(col_k)), pl.BlockSpec((lc, HEAD_DIM), ctxmap(col_v))]
    in_specs += [pl.BlockSpec((1,) + pair_table.shape[1:], lambda h, b, i, case, pair: (h, 0, 0, 0))]
    vm = 2 * (_nbytes(pair_table.shape[1:], F32) + 12 * _nbytes((tq, HEAD_DIM), MXU_DTYPE)) \
        + 7 * _nbytes((tq, NA_K_ROWS * GRID_W + lc), F32)
    grid_spec = pltpu.PrefetchScalarGridSpec(
        num_scalar_prefetch=2, grid=(NA_HEADS, n_batch, n_blk),
        in_specs=in_specs,
        out_specs=pl.BlockSpec((tq, HEAD_DIM), lambda h, b, i, case, pair: (b * (seq // tq) + i, h)))
    return pl.pallas_call(
        _na_kernel, name="na_attention",
        out_shape=jax.ShapeDtypeStruct((n, NA_HEADS * HEAD_DIM), MXU_DTYPE),
        grid_spec=grid_spec,
        compiler_params=_cparams(3, vm),
    )(case_of_blk, pair_of, *([proj] * (1 + 2 * n_chunk + 2)), pair_table)


def _ctx_attn_kernel(q_ref, k_ref, v_ref, prev_ref, o_ref, *, scale):
    del prev_ref
    s = lax.dot_general(q_ref[...], k_ref[...], (((1,), (1,)), ((), ())),
                        preferred_element_type=F32) * scale
    m = jnp.max(s, axis=-1, keepdims=True)
    p = jnp.exp(s - m)
    o = jnp.dot(p.astype(MXU_DTYPE), v_ref[...], preferred_element_type=F32)
    o_ref[...] = (o / jnp.sum(p, axis=-1, keepdims=True)).astype(o_ref.dtype)


def _ctx_attention(qa, ka, va, y, *, nl, lc, n_batch, n_heads, dk, dv, col_q, col_k, col_v, scale):
    vm = 2 * (2 * _nbytes((lc, dk), MXU_DTYPE) + 2 * _nbytes((lc, dv), MXU_DTYPE)) + 4 * _nbytes((lc, lc), F32)
    return pl.pallas_call(
        functools.partial(_ctx_attn_kernel, scale=scale), name="ctx_attention",
        out_shape=jax.ShapeDtypeStruct(y.shape, y.dtype),
        grid=(n_batch, n_heads),
        in_specs=[pl.BlockSpec((lc, dk), lambda b, h: (nl // lc + b, col_q + h)),
                  pl.BlockSpec((lc, dk), lambda b, h: (nl // lc + b, col_k + h)),
                  pl.BlockSpec((lc, dv), lambda b, h: (nl // lc + b, col_v + h)),
                  pl.BlockSpec(memory_space=pl.ANY)],
        out_specs=pl.BlockSpec((lc, dv), lambda b, h: (nl // lc + b, h)),
        input_output_aliases={3: 0},
        compiler_params=_cparams(2, vm),
    )(qa, ka, va, y)


def _mla_up_kernel(cq_ref, ckv_ref, kr_ref, qn_ref, kvn_ref, wq_ref, wk_ref, wv_ref, cos_ref, sin_ref,
                   q_ref, k_ref, v_ref):
    def rms(x, g):
        return (x * lax.rsqrt(jnp.mean(x * x, axis=-1, keepdims=True) + EPS) * g).astype(MXU_DTYPE)

    cos = cos_ref[...]
    sin = sin_ref[...]

    def rope(x):
        return x * cos + pltpu.roll(x, 2 * (MLA_ROPE // 2), axis=1) * sin

    cqn = rms(cq_ref[...].astype(F32), qn_ref[...])
    kvn = rms(ckv_ref[...].astype(F32), kvn_ref[...])
    q = jnp.dot(cqn, wq_ref[...], preferred_element_type=F32)
    kn = jnp.dot(kvn, wk_ref[...], preferred_element_type=F32)
    v_ref[...] = jnp.dot(kvn, wv_ref[...], preferred_element_type=F32).astype(v_ref.dtype)
    kr = rope(kr_ref[...].astype(F32)).astype(k_ref.dtype)
    for h in range(MLA_HEADS):
        c0 = h * MLA_HEAD_PAD
        q_ref[:, c0:c0 + MLA_NOPE] = q[:, c0:c0 + MLA_NOPE].astype(q_ref.dtype)
        q_ref[:, c0 + MLA_NOPE:c0 + MLA_HEAD_PAD] = rope(q[:, c0 + MLA_NOPE:c0 + MLA_HEAD_PAD]).astype(q_ref.dtype)
        k_ref[:, c0:c0 + MLA_NOPE] = kn[:, h * MLA_NOPE:(h + 1) * MLA_NOPE].astype(k_ref.dtype)
        k_ref[:, c0 + MLA_NOPE:c0 + MLA_HEAD_PAD] = kr


def _mla_up(proj, q_norm, kv_norm, wq, wk, wv, cos_t, sin_t, *, col_cq, col_ckv, col_kr):
    n = proj.shape[0]
    tm = 256
    rq, rkv = wq.shape[0], wk.shape[0]
    wq_cols = wq.shape[1]
    vm = 2 * (_nbytes(wq.shape, MXU_DTYPE) + _nbytes(wk.shape, MXU_DTYPE) + _nbytes(wv.shape, MXU_DTYPE)
              + _nbytes((tm, rq + rkv + LANES), MXU_DTYPE) + 2 * _nbytes((tm, LANES), F32)
              + _nbytes((tm, 2 * wq_cols + wv.shape[1]), MXU_DTYPE)) + 3 * _nbytes((tm, wq_cols), F32)
    return pl.pallas_call(
        _mla_up_kernel, name="mla_up",
        out_shape=(jax.ShapeDtypeStruct((n, wq_cols), MXU_DTYPE),
                   jax.ShapeDtypeStruct((n, wq_cols), MXU_DTYPE),
                   jax.ShapeDtypeStruct((n, wv.shape[1]), MXU_DTYPE)),
        grid=(n // tm,),
        in_specs=[pl.BlockSpec((tm, rq), lambda i: (i, col_cq)),
                  pl.BlockSpec((tm, rkv), lambda i: (i, col_ckv)),
                  pl.BlockSpec((tm, LANES), lambda i: (i, col_kr)),
                  pl.BlockSpec((1, rq), lambda i: (0, 0)),
                  pl.BlockSpec((1, rkv), lambda i: (0, 0)),
                  pl.BlockSpec(wq.shape, lambda i: (0, 0)),
                  pl.BlockSpec(wk.shape, lambda i: (0, 0)),
                  pl.BlockSpec(wv.shape, lambda i: (0, 0)),
                  pl.BlockSpec((tm, LANES), lambda i: (i, 0)),
                  pl.BlockSpec((tm, LANES), lambda i: (i, 0))],
        out_specs=(pl.BlockSpec((tm, wq_cols), lambda i: (i, 0)),
                   pl.BlockSpec((tm, wq_cols), lambda i: (i, 0)),
                   pl.BlockSpec((tm, wv.shape[1]), lambda i: (i, 0))),
        compiler_params=_cparams(1, vm),
    )(proj, proj, proj, q_norm.reshape(1, rq), kv_norm.reshape(1, rkv), wq, wk, wv, cos_t, sin_t)


def _flash_kernel(q_ref, kc_ref, kl_ref, vc_ref, vl_ref, o_ref, m_ref, l_ref, acc_ref, *, scale, tk):
    nt = (((1,), (1,)), ((), ()))
    q = q_ref[...]
    c = scale * math.log2(math.e)

    def chunk(k, v):
        s = lax.dot_general(q, k, nt, preferred_element_type=F32)
        m_old = m_ref[...]
        m_new = jnp.maximum(m_old, jnp.max(s, axis=-1, keepdims=True) * c)
        alpha = jnp.exp2(m_old - m_new)
        p = jnp.exp2(s * c - jnp.tile(m_new, (1, s.shape[1] // LANES)))
        l_ref[...] = alpha * l_ref[...] + jnp.sum(p, axis=-1, keepdims=True)
        acc_ref[...] = alpha * acc_ref[...] + jnp.dot(p.astype(MXU_DTYPE), v, preferred_element_type=F32)
        m_ref[...] = m_new

    m_ref[...] = jnp.full_like(m_ref, -jnp.inf)
    l_ref[...] = jnp.zeros_like(l_ref)
    acc_ref[...] = jnp.zeros_like(acc_ref)
    chunk(kc_ref[...], vc_ref[...])

    def body(j, carry):
        r0 = pl.multiple_of(j * tk, tk)
        chunk(kl_ref[pl.ds(r0, tk), :], vl_ref[pl.ds(r0, tk), :])
        return carry

    lax.fori_loop(0, kl_ref.shape[0] // tk, body, 0)
    o_ref[...] = (acc_ref[...] / l_ref[...]).astype(o_ref.dtype)


def _mla_attention(qf, kf, vf, *, nl, seq, lc, n_batch):
    n = qf.shape[0]
    tq, tk = _tile(seq, (1024, 512)), 512
    dk, dv = MLA_HEAD_PAD, MLA_V
    assert dv == LANES
    scale = (MLA_NOPE + MLA_ROPE) ** -0.5
    vm = 2 * (_nbytes((tq, dk), MXU_DTYPE) + _nbytes((lc + seq, dk + dv), MXU_DTYPE) + _nbytes((tq, dv), MXU_DTYPE)) \
        + 6 * _nbytes((tq, tk), F32) + 2 * _nbytes((tq, dv), F32)
    return pl.pallas_call(
        functools.partial(_flash_kernel, scale=scale, tk=tk), name="mla_attention",
        out_shape=jax.ShapeDtypeStruct((n, MLA_HEADS * dv), MXU_DTYPE),
        grid=(n_batch, MLA_HEADS, seq // tq),
        in_specs=[pl.BlockSpec((tq, dk), lambda b, h, i: (b * (seq // tq) + i, h)),
                  pl.BlockSpec((lc, dk), lambda b, h, i: (nl // lc + b, h)),
                  pl.BlockSpec((seq, dk), lambda b, h, i: (b, h)),
                  pl.BlockSpec((lc, dv), lambda b, h, i: (nl // lc + b, h)),
                  pl.BlockSpec((seq, dv), lambda b, h, i: (b, h))],
        out_specs=pl.BlockSpec((tq, dv), lambda b, h, i: (b * (seq // tq) + i, h)),
        scratch_shapes=[pltpu.VMEM((tq, LANES), F32), pltpu.VMEM((tq, LANES), F32), pltpu.VMEM((tq, dv), F32)],
        compiler_params=_cparams(3, vm),
    )(qf, kf, kf, vf, vf)


def _merge_kernel(ya_ref, yb_ref, gb_ref, gc_ref, xin_ref, gcp_ref, xinp_ref, gcn_ref, xinn_ref, cw_ref,
                  wa_ref, wb_ref, wc_ref, g0_ref, g1_ref, g2_ref, o_ref, yc_ref, *, nl, seq, lc):
    i = pl.program_id(0)
    j = pl.program_id(1)
    tm = ya_ref.shape[0]

    @pl.when(j == 0)
    def _():
        u = gc_ref[...].astype(F32) * xin_ref[...].astype(F32)
        u_prev_halo = (gcp_ref[...].astype(F32) * xinp_ref[...].astype(F32))[SUBLANES - 1:SUBLANES, :]
        u_next_halo = (gcn_ref[...].astype(F32) * xinn_ref[...].astype(F32))[0:1, :]
        row = lax.broadcasted_iota(jnp.int32, (tm, 1), 0)
        g = i * tm + row
        pos = jnp.where(g < nl, g % seq, (g - nl) % lc)
        seq_len = jnp.where(g < nl, seq, lc)
        u_dn = jnp.where(row == 0, u_prev_halo, pltpu.roll(u, 1, axis=0))
        u_up = jnp.where(row == tm - 1, u_next_halo, pltpu.roll(u, tm - 1, axis=0))
        u_dn = jnp.where(pos == 0, 0.0, u_dn)
        u_up = jnp.where(pos == seq_len - 1, 0.0, u_up)
        cw = cw_ref[...]
        y = cw[0:1, :] * u_dn + cw[1:2, :] * u + cw[2:3, :] * u_up
        yc_ref[...] = (gb_ref[...].astype(F32) * y).astype(yc_ref.dtype)

    def branch(y, w_ref, g_ref):
        gate = jax.nn.sigmoid(g_ref[...].astype(F32))
        return gate * jnp.dot(y, w_ref[...], preferred_element_type=F32)

    m = (branch(ya_ref[...], wa_ref, g0_ref) + branch(yb_ref[...], wb_ref, g1_ref)
         + branch(yc_ref[...], wc_ref, g2_ref))
    o_ref[...] = m.astype(o_ref.dtype)


def _merge(ya, yb, proj, conv_w, wa, wb, wc, layer, *, d, nl, seq, lc, skip, col_gb, col_gc, col_xin, col_gate):
    n = ya.shape[0]
    tm, tn = ROW_BLOCK, 1024 if d % 1024 == 0 else d
    w = ya.shape[1]
    hb = tm // SUBLANES
    last_halo = n // SUBLANES - 1
    n_col = d // tn

    def row(i, j):
        return (i, 0)

    def cur(col):
        return lambda i, j: (i, col)

    def prev(col):
        return lambda i, j: (jnp.maximum(i * hb - 1, 0), col)

    def nxt(col):
        return lambda i, j: (jnp.minimum((i + 1) * hb, last_halo), col)

    def gate(br):
        return lambda i, j: (i, col_gate + br * n_col + j)

    in_specs = [pl.BlockSpec((tm, w), row), pl.BlockSpec((tm, w), row),
                pl.BlockSpec((tm, w), cur(col_gb)), pl.BlockSpec((tm, w), cur(col_gc)),
                pl.BlockSpec((tm, w), cur(col_xin)),
                pl.BlockSpec((SUBLANES, w), prev(col_gc)), pl.BlockSpec((SUBLANES, w), prev(col_xin)),
                pl.BlockSpec((SUBLANES, w), nxt(col_gc)), pl.BlockSpec((SUBLANES, w), nxt(col_xin)),
                pl.BlockSpec((None, SC_TAPS, w), lambda i, j: (layer, 0, 0)),
                pl.BlockSpec((None, w, tn), lambda i, j: (layer, 0, j)),
                pl.BlockSpec((None, w, tn), lambda i, j: (layer, 0, j)),
                pl.BlockSpec((None, w, tn), lambda i, j: (layer, 0, j)),
                pl.BlockSpec((tm, tn), gate(0)), pl.BlockSpec((tm, tn), gate(1)), pl.BlockSpec((tm, tn), gate(2))]
    vm = 2 * (5 * _nbytes((tm, w), MXU_DTYPE) + 3 * _nbytes((w, tn), MXU_DTYPE) + 4 * _nbytes((tm, tn), MXU_DTYPE)) \
        + _nbytes((tm, w), MXU_DTYPE) + 6 * _nbytes((tm, max(w, tn)), F32)
    return pl.pallas_call(
        functools.partial(_merge_kernel, nl=nl, seq=seq, lc=lc), name="conv_merge",
        out_shape=jax.ShapeDtypeStruct((n, d), MXU_DTYPE),
        grid=(n // tm - skip, n_col),
        in_specs=in_specs,
        out_specs=pl.BlockSpec((tm, tn), lambda i, j: (i, j)),
        scratch_shapes=[pltpu.VMEM((tm, w), MXU_DTYPE)],
        compiler_params=_cparams(2, vm),
    )(ya, yb, proj, proj, proj, proj, proj, proj, proj, conv_w, wa, wb, wc, proj, proj, proj)


def _topk_cols(s, k, ids=None):
    if ids is None:
        ids = lax.broadcasted_iota(jnp.int32, s.shape, 0)
    big = jnp.iinfo(jnp.int32).max
    vals, idxs = [], []
    for _ in range(k):
        m = jnp.max(s, axis=0, keepdims=True)
        idx = jnp.min(jnp.where(s == m, ids, big), axis=0, keepdims=True)
        vals.append(m)
        idxs.append(idx)
        s = jnp.where(ids == idx, -jnp.inf, s)
    return jnp.concatenate(vals, axis=0), jnp.concatenate(idxs, axis=0)


def _candidate_rows():
    k = PEER_TOPK
    groups = []
    for b0 in range(0, k, SUBLANES):
        groups.append((0, 1, b0, b0 + SUBLANES))
    a = 1
    while a < k and k // (a + 1) > 1:
        assert k // (a + 1) <= SUBLANES
        groups.append((a, a + 1, 0, SUBLANES))
        a += 1
    while a < k:
        groups.append((a, a + SUBLANES, 0, 1))
        a += SUBLANES
    covered = {(aa, bb) for a0, a1, b0, b1 in groups for aa in range(a0, a1) for bb in range(b0, b1)}
    assert all((aa, bb) in covered for aa in range(k) for bb in range(k) if (aa + 1) * (bb + 1) <= k)
    return groups


def _take_rows(table, idx):
    iota = lax.broadcasted_iota(jnp.int32, table.shape, 0)
    rows = [jnp.sum(jnp.where(iota == idx[j:j + 1, :], table, 0), axis=0, keepdims=True)
            for j in range(idx.shape[0])]
    return jnp.concatenate(rows, axis=0)


def _peer_topk_kernel(q_ref, keys_ref, g_ref, i1_ref, i2_ref):
    nt = (((1,), (1,)), ((), ()))
    half = PEER_KEY_DIM // 2
    q = q_ref[...]
    s1 = lax.dot_general(keys_ref[0, :PEER_NKEYS, :], q[:, :half], nt, preferred_element_type=F32)
    s2 = lax.dot_general(keys_ref[0, PEER_NKEYS:, :], q[:, half:], nt, preferred_element_type=F32)
    v1, j1 = _topk_cols(s1, PEER_TOPK)
    v2, j2 = _topk_cols(s2, PEER_TOPK)
    sub = lax.broadcasted_iota(jnp.int32, (SUBLANES, q.shape[0]), 0)
    cand, flat = [], []
    for a0, a1, b0, b1 in _candidate_rows():
        cand.append(v1[a0:a1, :] + v2[b0:b1, :])
        flat.append(sub * (1 if a1 - a0 == 1 else PEER_TOPK) + (a0 * PEER_TOPK + b0))
    top_s, pos = _topk_cols(jnp.concatenate(cand, axis=0), PEER_TOPK, jnp.concatenate(flat, axis=0))
    i1_ref[...] = _take_rows(j1, jnp.right_shift(pos, PEER_TOPK.bit_length() - 1))
    i2_ref[...] = _take_rows(j2, jnp.bitwise_and(pos, PEER_TOPK - 1))
    e = jnp.exp(top_s - top_s[0:1, :])
    g_ref[...] = e / jnp.sum(e, axis=0, keepdims=True)


def _peer_topk(q, keys, *, skip):
    n = q.shape[0]
    t = 256
    off = skip * (ROW_BLOCK // t)
    rows = PEER_HEADS * PEER_TOPK
    out_spec = pl.BlockSpec((PEER_TOPK, t), lambda i, h: (h, i))
    vm = 2 * (_nbytes((t, PEER_KEY_DIM), MXU_DTYPE) + _nbytes(keys.shape[1:], MXU_DTYPE)) \
        + 8 * _nbytes((PEER_TOPK * PEER_TOPK, t), F32)
    return pl.pallas_call(
        _peer_topk_kernel, name="peer_topk",
        out_shape=(jax.ShapeDtypeStruct((rows, n), F32), jax.ShapeDtypeStruct((rows, n), jnp.int32),
                   jax.ShapeDtypeStruct((rows, n), jnp.int32)),
        grid=(n // t - off, PEER_HEADS),
        in_specs=[pl.BlockSpec((t, PEER_KEY_DIM), lambda i, h: (i, h)),
                  pl.BlockSpec((1,) + keys.shape[1:], lambda i, h: (h, 0, 0))],
        out_specs=(out_spec, out_spec, out_spec),
        compiler_params=_cparams(2, vm),
    )(q, keys)


def _peer_gates_kernel(i1_ref, i2_ref, g_ref, o_ref):
    n_sel = i1_ref.shape[1]
    iota = lax.broadcasted_iota(jnp.int32, (PEER_NKEYS, n_sel), 0)
    nt = (((1,), (1,)), ((), ()))

    def body(t, carry):
        i1b = i1_ref[pl.ds(t, 1), :]
        i2b = i2_ref[pl.ds(t, 1), :]
        g = g_ref[pl.ds(t, 1), :]
        g_hi = g.astype(MXU_DTYPE).astype(F32)
        g_lo = g - g_hi
        m1 = iota == i1b
        lhs = jnp.concatenate([jnp.where(m1, g_hi, 0.0), jnp.where(m1, g_lo, 0.0)], axis=1).astype(MXU_DTYPE)
        hot2 = jnp.where(iota == i2b, 1.0, 0.0).astype(MXU_DTYPE)
        rhs = jnp.concatenate([hot2, hot2], axis=1)
        o_ref[t] = lax.dot_general(lhs, rhs, nt, preferred_element_type=F32)
        return carry

    lax.fori_loop(0, i1_ref.shape[0], body, 0, unroll=32)


def _peer_gates(i1, i2, g, *, skip):
    n, n_sel = i1.shape
    t = 128
    off = skip * (ROW_BLOCK // t)
    spec = pl.BlockSpec((t, n_sel), lambda i: (i, 0))
    vm = 2 * (3 * _nbytes((t, n_sel), F32) + _nbytes((t, PEER_NKEYS, PEER_NKEYS), F32))
    return pl.pallas_call(
        _peer_gates_kernel, name="peer_gates",
        out_shape=jax.ShapeDtypeStruct((n, PEER_NKEYS, PEER_NKEYS), F32),
        grid=(n // t - off,),
        in_specs=[spec, spec, spec],
        out_specs=pl.BlockSpec((t, PEER_NKEYS, PEER_NKEYS), lambda i: (i, 0, 0)),
        compiler_params=_cparams(1, vm),
    )(i1, i2, g)


def _peer_up_kernel(h_ref, u_ref, g_ref, o_ref):
    acc = lax.dot_general(h_ref[...], u_ref[...], (((1,), (1,)), ((), ())), preferred_element_type=F32)
    for jj in range(g_ref.shape[1]):
        a = acc[:, jj * PEER_NKEYS:(jj + 1) * PEER_NKEYS]
        act = 0.5 * a * (1.0 + lax.erf(a * (2.0 ** -0.5)))
        o_ref[:, jj * PEER_NKEYS:(jj + 1) * PEER_NKEYS] = (act * g_ref[:, jj, :]).astype(o_ref.dtype)


def _peer_up(h, u, layer, gmat, *, skip):
    n, d = h.shape
    e = u.shape[1]
    tm = ROW_BLOCK
    ti = SUBLANES
    tn = ti * PEER_NKEYS
    vm = 2 * (_nbytes((tm, d), MXU_DTYPE) + _nbytes((d, tn), MXU_DTYPE) + _nbytes((tm, ti, PEER_NKEYS), F32)
              + _nbytes((tm, tn), MXU_DTYPE)) + 3 * _nbytes((tm, tn), F32)
    return pl.pallas_call(
        _peer_up_kernel, name="peer_up",
        out_shape=jax.ShapeDtypeStruct((n, e), MXU_DTYPE),
        grid=(n // tm - skip, e // tn),
        in_specs=[pl.BlockSpec((tm, d), lambda i, j: (i, 0)),
                  pl.BlockSpec((None, tn, d), lambda i, j: (layer, j, 0)),
                  pl.BlockSpec((tm, ti, PEER_NKEYS), lambda i, j: (i, j, 0))],
        out_specs=pl.BlockSpec((tm, tn), lambda i, j: (i, j)),
        compiler_params=_cparams(2, vm),
    )(h, u, gmat)


def _final_norm_kernel(x_ref, g_ref, o_ref):
    x = x_ref[...]
    o_ref[...] = x * lax.rsqrt(jnp.mean(x * x, axis=-1, keepdims=True) + EPS) * g_ref[...]


def _final_norm(x, gain, *, nl):
    d = x.shape[1]
    tb = 256
    vm = 4 * _nbytes((tb, d), F32) + 2 * _nbytes((tb, d), F32)
    return pl.pallas_call(
        _final_norm_kernel, name="final_norm",
        out_shape=jax.ShapeDtypeStruct((nl, d), F32),
        grid=(nl // tb,),
        in_specs=[pl.BlockSpec((tb, d), lambda i: (i, 0)), pl.BlockSpec((1, d), lambda i: (0, 0))],
        out_specs=pl.BlockSpec((tb, d), lambda i: (i, 0)),
        compiler_params=_cparams(1, vm),
    )(x, gain.reshape(1, d))


def _regroup_kernel(w_ref, o_ref, *, pad_from, pad_to):
    half = MLA_ROPE // 2
    r0 = pl.program_id(1) * o_ref.shape[0]
    in_pad = (r0 >= pad_from) & (r0 < pad_to)

    @pl.when(jnp.logical_not(in_pad))
    def _():
        o_ref[...] = w_ref[0].astype(o_ref.dtype)

    @pl.when(in_pad)
    def _():
        o_ref[...] = jnp.zeros_like(o_ref)

    @pl.when(r0 == pad_from)
    def _():
        o_ref[0:half, :] = w_ref[0, 0:half, :].astype(o_ref.dtype)
        o_ref[LANES // 2:LANES // 2 + half, :] = w_ref[0, half:2 * half, :].astype(o_ref.dtype)


def _regroup_in_proj(w_in_t, *, wide, kv_rank, d):
    n_layer, cols, _ = w_in_t.shape
    src_ckv = 4 * wide
    src_rope = src_ckv + kv_rank
    src_conv = src_rope + MLA_ROPE
    src_gate = src_conv + 3 * wide
    assert src_gate + N_BRANCH * d == cols
    dst_ckv = 7 * wide
    dst_rope = dst_ckv + kv_rank
    dst_gate = 8 * wide
    out_rows = dst_gate + N_BRANCH * d
    tr = 256
    assert dst_rope % tr == 0 and kv_rank % tr == 0 and (N_BRANCH * d) % tr == 0 and tr >= LANES

    def src_row(l, i):
        r = i * tr
        src = jnp.where(r < 4 * wide, r,
                        jnp.where(r < dst_ckv, r - 4 * wide + src_conv,
                                  jnp.where(r < dst_rope, r - dst_ckv + src_ckv,
                                            jnp.where(r < dst_gate, src_rope, r - dst_gate + src_gate))))
        return (l, pl.multiple_of(src, MLA_ROPE), 0)

    vm = 2 * (_nbytes((tr, d), F32) + _nbytes((tr, d), MXU_DTYPE)) + _nbytes((tr, d), F32)
    return pl.pallas_call(
        functools.partial(_regroup_kernel, pad_from=dst_rope, pad_to=dst_gate),
        name="regroup_in_proj",
        out_shape=jax.ShapeDtypeStruct((n_layer, out_rows, d), MXU_DTYPE),
        grid=(n_layer, out_rows // tr),
        in_specs=[pl.BlockSpec((pl.Element(1), pl.Element(tr), pl.Element(d)), src_row)],
        out_specs=pl.BlockSpec((None, tr, d), lambda l, i: (l, i, 0)),
        compiler_params=_cparams(2, vm),
    )(w_in_t)


def _rope_lane_layout(w):
    half = MLA_ROPE // 2
    z = jnp.zeros(w.shape[:-1] + (LANES // 2 - half,), w.dtype)
    return jnp.concatenate([w[..., :half], z, w[..., half:], z], axis=-1)


def _rope_tables(seq, nc):
    pos = jnp.arange(seq, dtype=jnp.int32)
    n_freq = MLA_ROPE // 4
    inv_freq = jnp.power(ROPE_BASE, -jnp.arange(n_freq, dtype=F32) / n_freq)
    row = (pos // GRID_W).astype(F32)
    col = (pos % GRID_W).astype(F32)
    ang = jnp.concatenate([row[:, None] * inv_freq, col[:, None] * inv_freq], axis=-1)
    cos, sin = jnp.cos(ang), jnp.sin(ang)
    cos_l = _rope_lane_layout(jnp.concatenate([cos, cos], axis=-1))
    sin_l = _rope_lane_layout(jnp.concatenate([-sin, sin], axis=-1))
    one = _rope_lane_layout(jnp.ones((nc, MLA_ROPE), F32))
    return one, jnp.zeros((nc, LANES), F32), cos_l, sin_l


def kernel(x, c, ctx, c_ctx, w_ada, b_ada, norm_mix, norm_ffn, w_in, na_rpb, mla_q_norm, mla_kv_norm, mla_w_uq, mla_w_ukv, conv_w, w_branch_a, w_branch_b, w_branch_c, w_out, peer_w_q, peer_sub_keys, peer_u, peer_v, final_norm):
    n_batch, seq, d = x.shape
    lc = ctx.shape[1]
    depth = w_ada.shape[0]
    nc = n_batch * lc
    nl = n_batch * seq
    na_w = NA_HEADS * HEAD_DIM
    q_rank, kv_rank = mla_w_uq.shape[1], mla_w_ukv.shape[1]
    sc_w = conv_w.shape[2]
    rows = seq // GRID_W
    assert nc == ROW_BLOCK and seq % ROW_BLOCK == 0 and d % LANES == 0
    assert rows % NA_Q_ROWS == 0 and rows >= NA_K_ROWS
    assert na_w == q_rank == sc_w == MLA_HEADS * MLA_V and kv_rank * 2 == na_w
    kw = dict(nl=nl, seq=seq, n_batch=n_batch)

    wide = na_w
    col_gate = 8 * wide
    proj_cols = col_gate + N_BRANCH * d

    cmat = jnp.zeros((SUBLANES, d), F32).at[:n_batch].set(c).at[n_batch].set(c_ctx)
    xs = jnp.concatenate([x.reshape(nl, d), ctx.reshape(nc, d)], axis=0)

    na_pairs, na_pair_of, case_of_blk, dc_hot, col_ok = _na_bias_tables(rows)
    one_c, zero_c, cos_l, sin_l = _rope_tables(seq, nc)
    cos_t = jnp.concatenate([cos_l] * n_batch + [one_c], axis=0)
    sin_t = jnp.concatenate([sin_l] * n_batch + [zero_c], axis=0)

    w_proj = _regroup_in_proj(jnp.swapaxes(w_in, 1, 2), wide=wide, kv_rank=kv_rank, d=d)
    b_ada3 = b_ada.reshape(depth, 1, N_MOD * d)
    wa_c, wb_c, wc_c = (w.astype(MXU_DTYPE) for w in (w_branch_a, w_branch_b, w_branch_c))
    w_out_c = w_out.astype(MXU_DTYPE)
    w_q_c = peer_w_q.astype(MXU_DTYPE)
    u_c = peer_u.astype(MXU_DTYPE)
    v_c = peer_v.astype(MXU_DTYPE)

    for l in range(depth):
        last = l == depth - 1
        skip = 1 if last else 0
        wq = mla_w_uq[l].reshape(q_rank, MLA_HEADS, MLA_NOPE + MLA_ROPE)
        wq = jnp.concatenate([wq[..., :MLA_NOPE], _rope_lane_layout(wq[..., MLA_NOPE:])], axis=-1)
        wq = wq.reshape(q_rank, MLA_HEADS * MLA_HEAD_PAD).astype(MXU_DTYPE)
        wkv = mla_w_ukv[l].reshape(kv_rank, MLA_HEADS, MLA_NOPE + MLA_V)
        wk = wkv[..., :MLA_NOPE].reshape(kv_rank, MLA_HEADS * MLA_NOPE).astype(MXU_DTYPE)
        wv = wkv[..., MLA_NOPE:].reshape(kv_rank, MLA_HEADS * MLA_V).astype(MXU_DTYPE)
        pair_table = _na_pair_table(na_rpb[l], na_pairs, dc_hot, col_ok)

        mod = _ada_table(cmat, w_ada, b_ada3, l)
        modtab = mod[:n_batch + 1].reshape((n_batch + 1) * N_MOD, 1, d)

        h = _normmod(xs, norm_mix[l], modtab, 0, skip=0, **kw)
        proj = _matmul(h, w_proj, l, tn=_tile(proj_cols, (1024, 512, 256)), out_dtype=MXU_DTYPE, w_rows=True,
                       name="in_proj")
        hd = wide // HEAD_DIM
        ya = _na_attention(proj, pair_table, jnp.asarray(case_of_blk), jnp.asarray(na_pair_of), lc=lc,
                           col_q=0, col_k=hd, col_v=2 * hd, **kw)
        qf, kf, vf = _mla_up(proj, mla_q_norm[l], mla_kv_norm[l], wq, wk, wv, cos_t, sin_t,
                             col_cq=3, col_ckv=7 * wide // kv_rank, col_kr=(7 * wide + kv_rank) // LANES)
        yb = _mla_attention(qf, kf, vf, lc=lc, **kw)
        if not last:
            ya = _ctx_attention(proj, proj, proj, ya, nl=nl, lc=lc, n_batch=n_batch, n_heads=NA_HEADS, dk=HEAD_DIM,
                                dv=HEAD_DIM, col_q=0, col_k=hd, col_v=2 * hd, scale=HEAD_DIM ** -0.5)
            yb = _ctx_attention(qf, kf, vf, yb, nl=nl, lc=lc, n_batch=n_batch, n_heads=MLA_HEADS, dk=MLA_HEAD_PAD,
                                dv=MLA_V, col_q=0, col_k=0, col_v=0, scale=(MLA_NOPE + MLA_ROPE) ** -0.5)
        tn_m = 1024 if d % 1024 == 0 else d
        m = _merge(ya, yb, proj, conv_w, wa_c, wb_c, wc_c, l, d=d, nl=nl, seq=seq, lc=lc, skip=skip,
                   col_gb=4, col_gc=5, col_xin=6, col_gate=col_gate // tn_m)
        xs = _matmul_resid(m, w_out_c, l, xs, modtab, 2, tn=tn_m, tk=d, skip=skip, name="out_proj", **kw)

        h2 = _normmod(xs, norm_ffn[l], modtab, 3, skip=skip, **kw)
        pq = _matmul(h2, w_q_c, l, tn=1024, out_dtype=MXU_DTYPE, skip=skip, name="peer_query")
        keys = peer_sub_keys[l].reshape(PEER_HEADS, 2 * PEER_NKEYS, PEER_KEY_DIM // 2).astype(MXU_DTYPE)
        gates_t, i1_t, i2_t = _peer_topk(pq, keys, skip=skip)
        gmat = _peer_gates(i1_t.T, i2_t.T, gates_t.T, skip=skip)
        hexp = _peer_up(h2, u_c, l, gmat, skip=skip)
        xs = _matmul_resid(hexp, v_c, l, xs, modtab, 5, tn=min(d, 2048), tk=2048, skip=skip, name="peer_down",
                           **kw)

    out = _final_norm(xs, final_norm, nl=nl)
    return out.reshape(n_batch, seq, d)
```

```python
import functools
import math

import numpy as np
import jax
import jax.numpy as jnp
from jax import lax
from jax.experimental import pallas as pl
from jax.experimental.pallas import tpu as pltpu

F32 = jnp.float32
MXU_DTYPE = jnp.bfloat16

GRID_W = 64
EPS = 1e-6
N_MOD = 6
NEG_INF = -1e30
HEAD_DIM = 128
NA_HEADS = 8
NA_WIN_H = 8
NA_WIN_W = 16
MLA_HEADS = 8
MLA_NOPE = 128
MLA_ROPE = 64
MLA_V = 128
ROPE_BASE = 10000.0
SC_TAPS = 3
N_BRANCH = 3
PEER_HEADS = 8
PEER_NKEYS = 128
PEER_KEY_DIM = 256
PEER_TOPK = 16

V7X_VMEM_BYTES = 64 * 1024 * 1024
LANES = 128
SUBLANES = 8

ROW_BLOCK = 512
MLA_HEAD_PAD = 256
NA_Q_ROWS = 8
NA_K_ROWS = 16
NA_K_CHUNK_ROWS = 4


def _cparams(n_axes, vmem_bytes):
    limit = min(int(vmem_bytes) + (4 << 20), V7X_VMEM_BYTES - (6 << 20))
    return pltpu.CompilerParams(dimension_semantics=("arbitrary",) * n_axes,
                                vmem_limit_bytes=limit)


def _nbytes(shape, dtype):
    return int(np.prod(shape)) * jnp.dtype(dtype).itemsize


def _tile(n, prefs):
    return next(t for t in prefs if n % t == 0)


def _mod_row(row0, nl, seq, n_batch):
    return jnp.where(row0 >= nl, n_batch, row0 // seq)


def _ada_kernel(c_ref, w_ref, b_ref, o_ref):
    c = c_ref[...]
    s = (c * jax.nn.sigmoid(c)).astype(MXU_DTYPE)
    acc = jnp.dot(s, w_ref[...].astype(MXU_DTYPE), preferred_element_type=F32)
    o_ref[...] = acc + b_ref[...]


def _ada_table(cmat, w_ada, b_ada, layer):
    rows, d = cmat.shape
    n = w_ada.shape[2]
    tn = 512
    vm = 2 * (_nbytes((d, tn), F32) + _nbytes((rows, tn), F32) * 2) + _nbytes((rows, d), F32) * 2 \
        + _nbytes((d, tn), MXU_DTYPE)
    return pl.pallas_call(
        _ada_kernel, name="ada_table",
        out_shape=jax.ShapeDtypeStruct((rows, n), F32),
        grid=(n // tn,),
        in_specs=[pl.BlockSpec((rows, d), lambda j: (0, 0)),
                  pl.BlockSpec((None, d, tn), lambda j: (layer, 0, j)),
                  pl.BlockSpec((None, 1, tn), lambda j: (layer, 0, j))],
        out_specs=pl.BlockSpec((rows, tn), lambda j: (0, j)),
        compiler_params=_cparams(1, vm),
    )(cmat, w_ada, b_ada)


def _normmod_kernel(x_ref, g_ref, sh_ref, sc_ref, o_ref):
    x = x_ref[...]
    y = x * lax.rsqrt(jnp.mean(x * x, axis=-1, keepdims=True) + EPS)
    o_ref[...] = ((y * g_ref[...]) * (1.0 + sc_ref[0]) + sh_ref[0]).astype(o_ref.dtype)


def _normmod(x, gain, modtab, m_shift, *, nl, seq, n_batch, skip):
    n, d = x.shape
    tb = 256
    off = skip * (ROW_BLOCK // tb)

    def mrow(i, m):
        return (_mod_row(i * tb, nl, seq, n_batch) * N_MOD + m, 0, 0)

    vm = 2 * (_nbytes((tb, d), F32) + _nbytes((tb, d), MXU_DTYPE)) + 3 * _nbytes((tb, d), F32)
    return pl.pallas_call(
        _normmod_kernel, name="normmod",
        out_shape=jax.ShapeDtypeStruct((n, d), MXU_DTYPE),
        grid=(n // tb - off,),
        in_specs=[pl.BlockSpec((tb, d), lambda i: (i, 0)),
                  pl.BlockSpec((1, d), lambda i: (0, 0)),
                  pl.BlockSpec((1, 1, d), lambda i: mrow(i, m_shift)),
                  pl.BlockSpec((1, 1, d), lambda i: mrow(i, m_shift + 1))],
        out_specs=pl.BlockSpec((tb, d), lambda i: (i, 0)),
        compiler_params=_cparams(1, vm),
    )(x, gain.reshape(1, d), modtab, modtab)


def _mm_kernel(x_ref, w_ref, o_ref):
    o_ref[...] = jnp.dot(x_ref[...], w_ref[...], preferred_element_type=F32).astype(o_ref.dtype)


def _mm_nt_kernel(x_ref, w_ref, o_ref):
    o_ref[...] = lax.dot_general(x_ref[...], w_ref[...], (((1,), (1,)), ((), ())),
                                 preferred_element_type=F32).astype(o_ref.dtype)


def _matmul(x, w, layer, *, tn, out_dtype, skip=0, w_rows=False, name="matmul"):
    n, k = x.shape
    nn = w.shape[1] if w_rows else w.shape[2]
    tm = ROW_BLOCK
    vm = 2 * (_nbytes((tm, k), x.dtype) + _nbytes((k, tn), w.dtype) + _nbytes((tm, tn), out_dtype)) \
        + _nbytes((tm, tn), F32)
    w_spec = (pl.BlockSpec((None, tn, k), lambda j, i: (layer, j, 0)) if w_rows
              else pl.BlockSpec((None, k, tn), lambda j, i: (layer, 0, j)))
    return pl.pallas_call(
        _mm_nt_kernel if w_rows else _mm_kernel, name=name,
        out_shape=jax.ShapeDtypeStruct((n, nn), out_dtype),
        grid=(nn // tn, n // tm - skip),
        in_specs=[pl.BlockSpec((tm, k), lambda j, i: (i, 0)), w_spec],
        out_specs=pl.BlockSpec((tm, tn), lambda j, i: (i, j)),
        compiler_params=_cparams(2, vm),
    )(x, w)


def _mm_resid_kernel(x_ref, w_ref, r_ref, g_ref, o_ref, acc_ref):
    k = pl.program_id(2)

    @pl.when(k == 0)
    def _():
        acc_ref[...] = jnp.zeros_like(acc_ref)

    acc_ref[...] += jnp.dot(x_ref[...], w_ref[...], preferred_element_type=F32)

    @pl.when(k == pl.num_programs(2) - 1)
    def _():
        o_ref[...] = r_ref[...] + g_ref[0] * acc_ref[...]


def _matmul_resid(x, w, layer, resid, modtab, m_gate, *, tn, tk, nl, seq, n_batch, skip, name):
    n, kk = x.shape
    nn = w.shape[2]
    tm = ROW_BLOCK

    def mrow(i, j, k):
        return (_mod_row(i * tm, nl, seq, n_batch) * N_MOD + m_gate, 0, j)

    vm = 2 * (_nbytes((tm, tk), x.dtype) + _nbytes((tk, tn), w.dtype) + 2 * _nbytes((tm, tn), F32)) \
        + 2 * _nbytes((tm, tn), F32)
    return pl.pallas_call(
        _mm_resid_kernel, name=name,
        out_shape=jax.ShapeDtypeStruct((n, nn), F32),
        grid=(n // tm - skip, nn // tn, kk // tk),
        in_specs=[pl.BlockSpec((tm, tk), lambda i, j, k: (i, k)),
                  pl.BlockSpec((None, tk, tn), lambda i, j, k: (layer, k, j)),
                  pl.BlockSpec((tm, tn), lambda i, j, k: (i, j)),
                  pl.BlockSpec((1, 1, tn), mrow)],
        out_specs=pl.BlockSpec((tm, tn), lambda i, j, k: (i, j)),
        scratch_shapes=[pltpu.VMEM((tm, tn), F32)],
        compiler_params=_cparams(3, vm),
    )(x, w, resid, modtab)


def _na_bias_tables(rows):
    assert 2 * GRID_W == LANES and NA_K_ROWS % 2 == 0
    n_dr = 2 * NA_WIN_H - 1
    n_blk = rows // NA_Q_ROWS
    rl = np.arange(NA_Q_ROWS)[:, None]
    krl = np.arange(NA_K_ROWS)[None, :]
    cases, case_of_blk = [], []
    for i in range(n_blk):
        r0 = i * NA_Q_ROWS
        k0 = int(np.clip(r0 - NA_K_CHUNK_ROWS, 0, rows - NA_K_ROWS))
        r = r0 + rl
        kr = k0 + krl
        start = np.clip(r - NA_WIN_H // 2, 0, rows - NA_WIN_H)
        ok = (kr >= start) & (kr < start + NA_WIN_H)
        dr = np.where(ok, kr - r + NA_WIN_H - 1, n_dr).astype(np.int32)
        assert dr.min() >= 0
        for ci, known in enumerate(cases):
            if np.array_equal(known, dr):
                case_of_blk.append(ci)
                break
        else:
            case_of_blk.append(len(cases))
            cases.append(dr)
    c = np.arange(GRID_W)[:, None]
    kc = np.arange(GRID_W)[None, :]
    ws = np.clip(c - NA_WIN_W // 2, 0, GRID_W - NA_WIN_W)
    col_ok = (kc >= ws) & (kc < ws + NA_WIN_W)
    dc = np.clip(kc - c + NA_WIN_W - 1, 0, 2 * NA_WIN_W - 2)
    dc_hot = (dc[None] == np.arange(2 * NA_WIN_W - 1)[:, None, None]).astype(np.float32)
    dr_of = np.stack(cases).reshape(len(cases), NA_Q_ROWS, NA_K_ROWS // 2, 2)
    pairs, pair_of = np.unique(dr_of.reshape(-1, 2), axis=0, return_inverse=True)
    return (pairs.astype(np.int32), pair_of.reshape(-1).astype(np.int32), np.asarray(case_of_blk, np.int32),
            dc_hot, col_ok)


def _na_pair_table(rpb, pairs, dc_hot, col_ok):
    n_h = rpb.shape[0]
    tb = jnp.einsum('hdk,kcq->hdcq', rpb.astype(F32), dc_hot, precision=lax.Precision.HIGHEST)
    tb = jnp.where(col_ok, tb, NEG_INF)
    tb = jnp.concatenate([tb, jnp.full((n_h, 1, GRID_W, GRID_W), NEG_INF, F32)], axis=1)
    return jnp.concatenate([tb[:, pairs[:, 0]], tb[:, pairs[:, 1]]], axis=-1)


def _na_kernel(case_ref, pair_ref, q_ref, k0_ref, k1_ref, k2_ref, k3_ref, v0_ref, v1_ref, v2_ref, v3_ref,
               kc_ref, vc_ref, b_ref, o_ref):
    scale = HEAD_DIM ** -0.5
    nt = (((1,), (1,)), ((), ()))
    n_pair = NA_K_ROWS // 2
    q = q_ref[...]
    k_loc = jnp.concatenate([k0_ref[...], k1_ref[...], k2_ref[...], k3_ref[...]], axis=0)
    v_loc = jnp.concatenate([v0_ref[...], v1_ref[...], v2_ref[...], v3_ref[...]], axis=0)
    s_loc = lax.dot_general(q, k_loc, nt, preferred_element_type=F32) * scale
    base = case_ref[pl.program_id(2)] * (NA_Q_ROWS * n_pair)
    biased = []
    for rl in range(NA_Q_ROWS):
        blocks = [b_ref[0, pair_ref[base + rl * n_pair + p]] for p in range(n_pair)]
        biased.append(s_loc[rl * GRID_W:(rl + 1) * GRID_W, :] + jnp.concatenate(blocks, axis=1))
    s_loc = jnp.concatenate(biased, axis=0)
    s_ctx = lax.dot_general(q, kc_ref[...], nt, preferred_element_type=F32) * scale
    m = jnp.maximum(jnp.max(s_loc, axis=-1, keepdims=True), jnp.max(s_ctx, axis=-1, keepdims=True))
    p_loc = jnp.exp(s_loc - m)
    p_ctx = jnp.exp(s_ctx - m)
    denom = jnp.sum(p_loc, axis=-1, keepdims=True) + jnp.sum(p_ctx, axis=-1, keepdims=True)
    o = (jnp.dot(p_loc.astype(MXU_DTYPE), v_loc, preferred_element_type=F32)
         + jnp.dot(p_ctx.astype(MXU_DTYPE), vc_ref[...], preferred_element_type=F32))
    o_ref[...] = (o / denom).astype(o_ref.dtype)


def _na_attention(proj, pair_table, case_of_blk, pair_of, *, nl, seq, lc, n_batch, col_q, col_k, col_v):
    n = proj.shape[0]
    rows = seq // GRID_W
    tq = NA_Q_ROWS * GRID_W
    tkc = NA_K_CHUNK_ROWS * GRID_W
    n_blk = rows // NA_Q_ROWS
    n_chunk = NA_K_ROWS // NA_K_CHUNK_ROWS
    max_c0 = rows // NA_K_CHUNK_ROWS - n_chunk

    def qmap(h, b, i, case, pair):
        return (b * (seq // tq) + i, col_q + h)

    def kvmap(col, j):
        def f(h, b, i, case, pair):
            c0 = jnp.clip(i * (NA_Q_ROWS // NA_K_CHUNK_ROWS) - 1, 0, max_c0)
            return (b * (seq // tkc) + c0 + j, col + h)
        return f

    def ctxmap(col):
        return lambda h, b, i, case, pair: (nl // lc + b, col + h)

    in_specs = [pl.BlockSpec((tq, HEAD_DIM), qmap)]
    in_specs += [pl.BlockSpec((tkc, HEAD_DIM), kvmap(col_k, j)) for j in range(n_chunk)]
    in_specs += [pl.BlockSpec((tkc, HEAD_DIM), kvmap(col_v, j)) for j in range(n_chunk)]
    in_specs += [pl.BlockSpec((lc, HEAD_DIM), ctxmap(col_k)), pl.BlockSpec((lc, HEAD_DIM), ctxmap(col_v))]
    in_specs += [pl.BlockSpec((1,) + pair_table.shape[1:], lambda h, b, i, case, pair: (h, 0, 0, 0))]
    vm = 2 * (_nbytes(pair_table.shape[1:], F32) + 12 * _nbytes((tq, HEAD_DIM), MXU_DTYPE)) \
        + 7 * _nbytes((tq, NA_K_ROWS * GRID_W + lc), F32)
    grid_spec = pltpu.PrefetchScalarGridSpec(
        num_scalar_prefetch=2, grid=(NA_HEADS, n_batch, n_blk),
        in_specs=in_specs,
        out_specs=pl.BlockSpec((tq, HEAD_DIM), lambda h, b, i, case, pair: (b * (seq // tq) + i, h)))
    return pl.pallas_call(
        _na_kernel, name="na_attention",
        out_shape=jax.ShapeDtypeStruct((n, NA_HEADS * HEAD_DIM), MXU_DTYPE),
        grid_spec=grid_spec,
        compiler_params=_cparams(3, vm),
    )(case_of_blk, pair_of, *([proj] * (1 + 2 * n_chunk + 2)), pair_table)


def _ctx_attn_kernel(q_ref, k_ref, v_ref, prev_ref, o_ref, *, scale):
    del prev_ref
    s = lax.dot_general(q_ref[...], k_ref[...], (((1,), (1,)), ((), ())),
                        preferred_element_type=F32) * scale
    m = jnp.max(s, axis=-1, keepdims=True)
    p = jnp.exp(s - m)
    o = jnp.dot(p.astype(MXU_DTYPE), v_ref[...], preferred_element_type=F32)
    o_ref[...] = (o / jnp.sum(p, axis=-1, keepdims=True)).astype(o_ref.dtype)


def _ctx_attention(qa, ka, va, y, *, nl, lc, n_batch, n_heads, dk, dv, col_q, col_k, col_v, scale):
    vm = 2 * (2 * _nbytes((lc, dk), MXU_DTYPE) + 2 * _nbytes((lc, dv), MXU_DTYPE)) + 4 * _nbytes((lc, lc), F32)
    return pl.pallas_call(
        functools.partial(_ctx_attn_kernel, scale=scale), name="ctx_attention",
        out_shape=jax.ShapeDtypeStruct(y.shape, y.dtype),
        grid=(n_batch, n_heads),
        in_specs=[pl.BlockSpec((lc, dk), lambda b, h: (nl // lc + b, col_q + h)),
                  pl.BlockSpec((lc, dk), lambda b, h: (nl // lc + b, col_k + h)),
                  pl.BlockSpec((lc, dv), lambda b, h: (nl // lc + b, col_v + h)),
                  pl.BlockSpec(memory_space=pl.ANY)],
        out_specs=pl.BlockSpec((lc, dv), lambda b, h: (nl // lc + b, h)),
        input_output_aliases={3: 0},
        compiler_params=_cparams(2, vm),
    )(qa, ka, va, y)


def _mla_up_kernel(cq_ref, ckv_ref, kr_ref, qn_ref, kvn_ref, wq_ref, wk_ref, wv_ref, cos_ref, sin_ref,
                   q_ref, k_ref, v_ref):
    def rms(x, g):
        return (x * lax.rsqrt(jnp.mean(x * x, axis=-1, keepdims=True) + EPS) * g).astype(MXU_DTYPE)

    cos = cos_ref[...]
    sin = sin_ref[...]

    def rope(x):
        return x * cos + pltpu.roll(x, 2 * (MLA_ROPE // 2), axis=1) * sin

    cqn = rms(cq_ref[...].astype(F32), qn_ref[...])
    kvn = rms(ckv_ref[...].astype(F32), kvn_ref[...])
    q = jnp.dot(cqn, wq_ref[...], preferred_element_type=F32)
    kn = jnp.dot(kvn, wk_ref[...], preferred_element_type=F32)
    v_ref[...] = jnp.dot(kvn, wv_ref[...], preferred_element_type=F32).astype(v_ref.dtype)
    kr = rope(kr_ref[...].astype(F32)).astype(k_ref.dtype)
    for h in range(MLA_HEADS):
        c0 = h * MLA_HEAD_PAD
        q_ref[:, c0:c0 + MLA_NOPE] = q[:, c0:c0 + MLA_NOPE].astype(q_ref.dtype)
        q_ref[:, c0 + MLA_NOPE:c0 + MLA_HEAD_PAD] = rope(q[:, c0 + MLA_NOPE:c0 + MLA_HEAD_PAD]).astype(q_ref.dtype)
        k_ref[:, c0:c0 + MLA_NOPE] = kn[:, h * MLA_NOPE:(h + 1) * MLA_NOPE].astype(k_ref.dtype)
        k_ref[:, c0 + MLA_NOPE:c0 + MLA_HEAD_PAD] = kr


def _mla_up(proj, q_norm, kv_norm, wq, wk, wv, cos_t, sin_t, *, col_cq, col_ckv, col_kr):
    n = proj.shape[0]
    tm = 256
    rq, rkv = wq.shape[0], wk.shape[0]
    wq_cols = wq.shape[1]
    vm = 2 * (_nbytes(wq.shape, MXU_DTYPE) + _nbytes(wk.shape, MXU_DTYPE) + _nbytes(wv.shape, MXU_DTYPE)
              + _nbytes((tm, rq + rkv + LANES), MXU_DTYPE) + 2 * _nbytes((tm, LANES), F32)
              + _nbytes((tm, 2 * wq_cols + wv.shape[1]), MXU_DTYPE)) + 3 * _nbytes((tm, wq_cols), F32)
    return pl.pallas_call(
        _mla_up_kernel, name="mla_up",
        out_shape=(jax.ShapeDtypeStruct((n, wq_cols), MXU_DTYPE),
                   jax.ShapeDtypeStruct((n, wq_cols), MXU_DTYPE),
                   jax.ShapeDtypeStruct((n, wv.shape[1]), MXU_DTYPE)),
        grid=(n // tm,),
        in_specs=[pl.BlockSpec((tm, rq), lambda i: (i, col_cq)),
                  pl.BlockSpec((tm, rkv), lambda i: (i, col_ckv)),
                  pl.BlockSpec((tm, LANES), lambda i: (i, col_kr)),
                  pl.BlockSpec((1, rq), lambda i: (0, 0)),
                  pl.BlockSpec((1, rkv), lambda i: (0, 0)),
                  pl.BlockSpec(wq.shape, lambda i: (0, 0)),
                  pl.BlockSpec(wk.shape, lambda i: (0, 0)),
                  pl.BlockSpec(wv.shape, lambda i: (0, 0)),
                  pl.BlockSpec((tm, LANES), lambda i: (i, 0)),
                  pl.BlockSpec((tm, LANES), lambda i: (i, 0))],
        out_specs=(pl.BlockSpec((tm, wq_cols), lambda i: (i, 0)),
                   pl.BlockSpec((tm, wq_cols), lambda i: (i, 0)),
                   pl.BlockSpec((tm, wv.shape[1]), lambda i: (i, 0))),
        compiler_params=_cparams(1, vm),
    )(proj, proj, proj, q_norm.reshape(1, rq), kv_norm.reshape(1, rkv), wq, wk, wv, cos_t, sin_t)


def _flash_kernel(q_ref, kc_ref, kl_ref, vc_ref, vl_ref, o_ref, m_ref, l_ref, acc_ref, *, scale, tk):
    nt = (((1,), (1,)), ((), ()))
    q = q_ref[...]
    c = scale * math.log2(math.e)

    def chunk(k, v):
        s = lax.dot_general(q, k, nt, preferred_element_type=F32)
        m_old = m_ref[...]
        m_new = jnp.maximum(m_old, jnp.max(s, axis=-1, keepdims=True) * c)
        alpha = jnp.exp2(m_old - m_new)
        p = jnp.exp2(s * c - jnp.tile(m_new, (1, s.shape[1] // LANES)))
        l_ref[...] = alpha * l_ref[...] + jnp.sum(p, axis=-1, keepdims=True)
        acc_ref[...] = alpha * acc_ref[...] + jnp.dot(p.astype(MXU_DTYPE), v, preferred_element_type=F32)
        m_ref[...] = m_new

    m_ref[...] = jnp.full_like(m_ref, -jnp.inf)
    l_ref[...] = jnp.zeros_like(l_ref)
    acc_ref[...] = jnp.zeros_like(acc_ref)
    chunk(kc_ref[...], vc_ref[...])

    def body(j, carry):
        r0 = pl.multiple_of(j * tk, tk)
        chunk(kl_ref[pl.ds(r0, tk), :], vl_ref[pl.ds(r0, tk), :])
        return carry

    lax.fori_loop(0, kl_ref.shape[0] // tk, body, 0)
    o_ref[...] = (acc_ref[...] / l_ref[...]).astype(o_ref.dtype)


def _mla_attention(qf, kf, vf, *, nl, seq, lc, n_batch):
    n = qf.shape[0]
    tq, tk = _tile(seq, (1024, 512)), 512
    dk, dv = MLA_HEAD_PAD, MLA_V
    assert dv == LANES
    scale = (MLA_NOPE + MLA_ROPE) ** -0.5
    vm = 2 * (_nbytes((tq, dk), MXU_DTYPE) + _nbytes((lc + seq, dk + dv), MXU_DTYPE) + _nbytes((tq, dv), MXU_DTYPE)) \
        + 6 * _nbytes((tq, tk), F32) + 2 * _nbytes((tq, dv), F32)
    return pl.pallas_call(
        functools.partial(_flash_kernel, scale=scale, tk=tk), name="mla_attention",
        out_shape=jax.ShapeDtypeStruct((n, MLA_HEADS * dv), MXU_DTYPE),
        grid=(n_batch, MLA_HEADS, seq // tq),
        in_specs=[pl.BlockSpec((tq, dk), lambda b, h, i: (b * (seq // tq) + i, h)),
                  pl.BlockSpec((lc, dk), lambda b, h, i: (nl // lc + b, h)),
                  pl.BlockSpec((seq, dk), lambda b, h, i: (b, h)),
                  pl.BlockSpec((lc, dv), lambda b, h, i: (nl // lc + b, h)),
                  pl.BlockSpec((seq, dv), lambda b, h, i: (b, h))],
        out_specs=pl.BlockSpec((tq, dv), lambda b, h, i: (b * (seq // tq) + i, h)),
        scratch_shapes=[pltpu.VMEM((tq, LANES), F32), pltpu.VMEM((tq, LANES), F32), pltpu.VMEM((tq, dv), F32)],
        compiler_params=_cparams(3, vm),
    )(qf, kf, kf, vf, vf)


def _merge_kernel(ya_ref, yb_ref, gb_ref, gc_ref, xin_ref, gcp_ref, xinp_ref, gcn_ref, xinn_ref, cw_ref,
                  wa_ref, wb_ref, wc_ref, g0_ref, g1_ref, g2_ref, o_ref, yc_ref, *, nl, seq, lc):
    i = pl.program_id(0)
    j = pl.program_id(1)
    tm = ya_ref.shape[0]

    @pl.when(j == 0)
    def _():
        u = gc_ref[...].astype(F32) * xin_ref[...].astype(F32)
        u_prev_halo = (gcp_ref[...].astype(F32) * xinp_ref[...].astype(F32))[SUBLANES - 1:SUBLANES, :]
        u_next_halo = (gcn_ref[...].astype(F32) * xinn_ref[...].astype(F32))[0:1, :]
        row = lax.broadcasted_iota(jnp.int32, (tm, 1), 0)
        g = i * tm + row
        pos = jnp.where(g < nl, g % seq, (g - nl) % lc)
        seq_len = jnp.where(g < nl, seq, lc)
        u_dn = jnp.where(row == 0, u_prev_halo, pltpu.roll(u, 1, axis=0))
        u_up = jnp.where(row == tm - 1, u_next_halo, pltpu.roll(u, tm - 1, axis=0))
        u_dn = jnp.where(pos == 0, 0.0, u_dn)
        u_up = jnp.where(pos == seq_len - 1, 0.0, u_up)
        cw = cw_ref[...]
        y = cw[0:1, :] * u_dn + cw[1:2, :] * u + cw[2:3, :] * u_up
        yc_ref[...] = (gb_ref[...].astype(F32) * y).astype(yc_ref.dtype)

    def branch(y, w_ref, g_ref):
        gate = jax.nn.sigmoid(g_ref[...].astype(F32))
        return gate * jnp.dot(y, w_ref[...], preferred_element_type=F32)

    m = (branch(ya_ref[...], wa_ref, g0_ref) + branch(yb_ref[...], wb_ref, g1_ref)
         + branch(yc_ref[...], wc_ref, g2_ref))
    o_ref[...] = m.astype(o_ref.dtype)


def _merge(ya, yb, proj, conv_w, wa, wb, wc, layer, *, d, nl, seq, lc, skip, col_gb, col_gc, col_xin, col_gate):
    n = ya.shape[0]
    tm, tn = ROW_BLOCK, 1024 if d % 1024 == 0 else d
    w = ya.shape[1]
    hb = tm // SUBLANES
    last_halo = n // SUBLANES - 1
    n_col = d // tn

    def row(i, j):
        return (i, 0)

    def cur(col):
        return lambda i, j: (i, col)

    def prev(col):
        return lambda i, j: (jnp.maximum(i * hb - 1, 0), col)

    def nxt(col):
        return lambda i, j: (jnp.minimum((i + 1) * hb, last_halo), col)

    def gate(br):
        return lambda i, j: (i, col_gate + br * n_col + j)

    in_specs = [pl.BlockSpec((tm, w), row), pl.BlockSpec((tm, w), row),
                pl.BlockSpec((tm, w), cur(col_gb)), pl.BlockSpec((tm, w), cur(col_gc)),
                pl.BlockSpec((tm, w), cur(col_xin)),
                pl.BlockSpec((SUBLANES, w), prev(col_gc)), pl.BlockSpec((SUBLANES, w), prev(col_xin)),
                pl.BlockSpec((SUBLANES, w), nxt(col_gc)), pl.BlockSpec((SUBLANES, w), nxt(col_xin)),
                pl.BlockSpec((None, SC_TAPS, w), lambda i, j: (layer, 0, 0)),
                pl.BlockSpec((None, w, tn), lambda i, j: (layer, 0, j)),
                pl.BlockSpec((None, w, tn), lambda i, j: (layer, 0, j)),
                pl.BlockSpec((None, w, tn), lambda i, j: (layer, 0, j)),
                pl.BlockSpec((tm, tn), gate(0)), pl.BlockSpec((tm, tn), gate(1)), pl.BlockSpec((tm, tn), gate(2))]
    vm = 2 * (5 * _nbytes((tm, w), MXU_DTYPE) + 3 * _nbytes((w, tn), MXU_DTYPE) + 4 * _nbytes((tm, tn), MXU_DTYPE)) \
        + _nbytes((tm, w), MXU_DTYPE) + 6 * _nbytes((tm, max(w, tn)), F32)
    return pl.pallas_call(
        functools.partial(_merge_kernel, nl=nl, seq=seq, lc=lc), name="conv_merge",
        out_shape=jax.ShapeDtypeStruct((n, d), MXU_DTYPE),
        grid=(n // tm - skip, n_col),
        in_specs=in_specs,
        out_specs=pl.BlockSpec((tm, tn), lambda i, j: (i, j)),
        scratch_shapes=[pltpu.VMEM((tm, w), MXU_DTYPE)],
        compiler_params=_cparams(2, vm),
    )(ya, yb, proj, proj, proj, proj, proj, proj, proj, conv_w, wa, wb, wc, proj, proj, proj)


def _sort_network(n):
    size = 1
    while size < n:
        size *= 2
    net = []

    def merge(lo, hi, r):
        step = r * 2
        if step < hi - lo:
            merge(lo, hi, step)
            merge(lo + r, hi, step)
            net.extend((i, i + r) for i in range(lo + r, hi - r, step))
        else:
            net.append((lo, lo + r))

    def sort(lo, hi):
        if hi - lo >= 1:
            mid = lo + (hi - lo) // 2
            sort(lo, mid)
            sort(mid + 1, hi)
            merge(lo, hi, 1)

    sort(0, size - 1)
    return [(i, j) for i, j in net if j < n]


def _topk_cols(s, k, ids=None):
    n_tile = s.shape[0] // SUBLANES
    v = [s[t * SUBLANES:(t + 1) * SUBLANES, :] for t in range(n_tile)]
    if ids is None:
        sub = lax.broadcasted_iota(jnp.int32, (SUBLANES, s.shape[1]), 0)
        r = [sub + t * SUBLANES for t in range(n_tile)]
    else:
        r = [ids[t * SUBLANES:(t + 1) * SUBLANES, :] for t in range(n_tile)]
    for a, b in _sort_network(n_tile):
        first = (v[a] > v[b]) | ((v[a] == v[b]) & (r[a] < r[b]))
        v[a], v[b] = jnp.where(first, v[a], v[b]), jnp.where(first, v[b], v[a])
        r[a], r[b] = jnp.where(first, r[a], r[b]), jnp.where(first, r[b], r[a])
    depth = min(n_tile, k)
    v, r = v[:depth], r[:depth]
    big = jnp.iinfo(jnp.int32).max
    vals, idxs = [], []
    for it in range(k):
        m = jnp.max(v[0], axis=0, keepdims=True)
        idx = jnp.min(jnp.where(v[0] == m, r[0], big), axis=0, keepdims=True)
        vals.append(m)
        idxs.append(idx)
        win = r[0] == idx
        live = min(depth, k - it)
        for t in range(live - 1):
            v[t] = jnp.where(win, v[t + 1], v[t])
            r[t] = jnp.where(win, r[t + 1], r[t])
        v[live - 1] = jnp.where(win, -jnp.inf, v[live - 1])
    return jnp.concatenate(vals, axis=0), jnp.concatenate(idxs, axis=0)


def _candidate_rows():
    k = PEER_TOPK
    groups = []
    for b0 in range(0, k, SUBLANES):
        groups.append((0, 1, b0, b0 + SUBLANES))
    a = 1
    while a < k and k // (a + 1) > 1:
        assert k // (a + 1) <= SUBLANES
        groups.append((a, a + 1, 0, SUBLANES))
        a += 1
    while a < k:
        groups.append((a, a + SUBLANES, 0, 1))
        a += SUBLANES
    covered = {(aa, bb) for a0, a1, b0, b1 in groups for aa in range(a0, a1) for bb in range(b0, b1)}
    assert all((aa, bb) in covered for aa in range(k) for bb in range(k) if (aa + 1) * (bb + 1) <= k)
    return groups


def _take_rows(table, idx):
    iota = lax.broadcasted_iota(jnp.int32, table.shape, 0)
    rows = [jnp.sum(jnp.where(iota == idx[j:j + 1, :], table, 0), axis=0, keepdims=True)
            for j in range(idx.shape[0])]
    return jnp.concatenate(rows, axis=0)


def _peer_topk_kernel(q_ref, keys_ref, g_ref, i1_ref, i2_ref):
    nt = (((1,), (1,)), ((), ()))
    half = PEER_KEY_DIM // 2
    q = q_ref[...]
    s1 = lax.dot_general(keys_ref[0, :PEER_NKEYS, :], q[:, :half], nt, preferred_element_type=F32)
    s2 = lax.dot_general(keys_ref[0, PEER_NKEYS:, :], q[:, half:], nt, preferred_element_type=F32)
    v1, j1 = _topk_cols(s1, PEER_TOPK)
    v2, j2 = _topk_cols(s2, PEER_TOPK)
    sub = lax.broadcasted_iota(jnp.int32, (SUBLANES, q.shape[0]), 0)
    cand, flat = [], []
    for a0, a1, b0, b1 in _candidate_rows():
        cand.append(v1[a0:a1, :] + v2[b0:b1, :])
        flat.append(sub * (1 if a1 - a0 == 1 else PEER_TOPK) + (a0 * PEER_TOPK + b0))
    top_s, pos = _topk_cols(jnp.concatenate(cand, axis=0), PEER_TOPK, jnp.concatenate(flat, axis=0))
    i1_ref[...] = _take_rows(j1, jnp.right_shift(pos, PEER_TOPK.bit_length() - 1))
    i2_ref[...] = _take_rows(j2, jnp.bitwise_and(pos, PEER_TOPK - 1))
    e = jnp.exp(top_s - top_s[0:1, :])
    g_ref[...] = e / jnp.sum(e, axis=0, keepdims=True)


def _peer_topk(q, keys, *, skip):
    n = q.shape[0]
    t = 256
    off = skip * (ROW_BLOCK // t)
    rows = PEER_HEADS * PEER_TOPK
    out_spec = pl.BlockSpec((PEER_TOPK, t), lambda i, h: (h, i))
    vm = 2 * (_nbytes((t, PEER_KEY_DIM), MXU_DTYPE) + _nbytes(keys.shape[1:], MXU_DTYPE)) \
        + 8 * _nbytes((PEER_TOPK * PEER_TOPK, t), F32)
    return pl.pallas_call(
        _peer_topk_kernel, name="peer_topk",
        out_shape=(jax.ShapeDtypeStruct((rows, n), F32), jax.ShapeDtypeStruct((rows, n), jnp.int32),
                   jax.ShapeDtypeStruct((rows, n), jnp.int32)),
        grid=(n // t - off, PEER_HEADS),
        in_specs=[pl.BlockSpec((t, PEER_KEY_DIM), lambda i, h: (i, h)),
                  pl.BlockSpec((1,) + keys.shape[1:], lambda i, h: (h, 0, 0))],
        out_specs=(out_spec, out_spec, out_spec),
        compiler_params=_cparams(2, vm),
    )(q, keys)


def _peer_gates_kernel(i1_ref, i2_ref, g_ref, o_ref):
    n_sel = i1_ref.shape[1]
    iota = lax.broadcasted_iota(jnp.int32, (PEER_NKEYS, n_sel), 0)
    nt = (((1,), (1,)), ((), ()))

    def body(t, carry):
        i1b = i1_ref[pl.ds(t, 1), :]
        i2b = i2_ref[pl.ds(t, 1), :]
        g = g_ref[pl.ds(t, 1), :]
        g_hi = g.astype(MXU_DTYPE).astype(F32)
        g_lo = g - g_hi
        m1 = iota == i1b
        lhs = jnp.concatenate([jnp.where(m1, g_hi, 0.0), jnp.where(m1, g_lo, 0.0)], axis=1).astype(MXU_DTYPE)
        hot2 = jnp.where(iota == i2b, 1.0, 0.0).astype(MXU_DTYPE)
        rhs = jnp.concatenate([hot2, hot2], axis=1)
        o_ref[t] = lax.dot_general(lhs, rhs, nt, preferred_element_type=F32)
        return carry

    lax.fori_loop(0, i1_ref.shape[0], body, 0, unroll=32)


def _peer_gates(i1, i2, g, *, skip):
    n, n_sel = i1.shape
    t = 128
    off = skip * (ROW_BLOCK // t)
    spec = pl.BlockSpec((t, n_sel), lambda i: (i, 0))
    vm = 2 * (3 * _nbytes((t, n_sel), F32) + _nbytes((t, PEER_NKEYS, PEER_NKEYS), F32))
    return pl.pallas_call(
        _peer_gates_kernel, name="peer_gates",
        out_shape=jax.ShapeDtypeStruct((n, PEER_NKEYS, PEER_NKEYS), F32),
        grid=(n // t - off,),
        in_specs=[spec, spec, spec],
        out_specs=pl.BlockSpec((t, PEER_NKEYS, PEER_NKEYS), lambda i: (i, 0, 0)),
        compiler_params=_cparams(1, vm),
    )(i1, i2, g)


def _peer_up_kernel(h_ref, u_ref, g_ref, o_ref):
    acc = lax.dot_general(h_ref[...], u_ref[...], (((1,), (1,)), ((), ())), preferred_element_type=F32)
    for jj in range(g_ref.shape[1]):
        a = acc[:, jj * PEER_NKEYS:(jj + 1) * PEER_NKEYS]
        act = 0.5 * a * (1.0 + lax.erf(a * (2.0 ** -0.5)))
        o_ref[:, jj * PEER_NKEYS:(jj + 1) * PEER_NKEYS] = (act * g_ref[:, jj, :]).astype(o_ref.dtype)


def _peer_up(h, u, layer, gmat, *, skip):
    n, d = h.shape
    e = u.shape[1]
    tm = ROW_BLOCK
    ti = SUBLANES
    tn = ti * PEER_NKEYS
    vm = 2 * (_nbytes((tm, d), MXU_DTYPE) + _nbytes((d, tn), MXU_DTYPE) + _nbytes((tm, ti, PEER_NKEYS), F32)
              + _nbytes((tm, tn), MXU_DTYPE)) + 3 * _nbytes((tm, tn), F32)
    return pl.pallas_call(
        _peer_up_kernel, name="peer_up",
        out_shape=jax.ShapeDtypeStruct((n, e), MXU_DTYPE),
        grid=(n // tm - skip, e // tn),
        in_specs=[pl.BlockSpec((tm, d), lambda i, j: (i, 0)),
                  pl.BlockSpec((None, tn, d), lambda i, j: (layer, j, 0)),
                  pl.BlockSpec((tm, ti, PEER_NKEYS), lambda i, j: (i, j, 0))],
        out_specs=pl.BlockSpec((tm, tn), lambda i, j: (i, j)),
        compiler_params=_cparams(2, vm),
    )(h, u, gmat)


def _final_norm_kernel(x_ref, g_ref, o_ref):
    x = x_ref[...]
    o_ref[...] = x * lax.rsqrt(jnp.mean(x * x, axis=-1, keepdims=True) + EPS) * g_ref[...]


def _final_norm(x, gain, *, nl):
    d = x.shape[1]
    tb = 256
    vm = 4 * _nbytes((tb, d), F32) + 2 * _nbytes((tb, d), F32)
    return pl.pallas_call(
        _final_norm_kernel, name="final_norm",
        out_shape=jax.ShapeDtypeStruct((nl, d), F32),
        grid=(nl // tb,),
        in_specs=[pl.BlockSpec((tb, d), lambda i: (i, 0)), pl.BlockSpec((1, d), lambda i: (0, 0))],
        out_specs=pl.BlockSpec((tb, d), lambda i: (i, 0)),
        compiler_params=_cparams(1, vm),
    )(x, gain.reshape(1, d))


def _regroup_kernel(w_ref, o_ref, *, pad_from, pad_to):
    half = MLA_ROPE // 2
    r0 = pl.program_id(1) * o_ref.shape[0]
    in_pad = (r0 >= pad_from) & (r0 < pad_to)

    @pl.when(jnp.logical_not(in_pad))
    def _():
        o_ref[...] = w_ref[0].astype(o_ref.dtype)

    @pl.when(in_pad)
    def _():
        o_ref[...] = jnp.zeros_like(o_ref)

    @pl.when(r0 == pad_from)
    def _():
        o_ref[0:half, :] = w_ref[0, 0:half, :].astype(o_ref.dtype)
        o_ref[LANES // 2:LANES // 2 + half, :] = w_ref[0, half:2 * half, :].astype(o_ref.dtype)


def _regroup_in_proj(w_in_t, *, wide, kv_rank, d):
    n_layer, cols, _ = w_in_t.shape
    src_ckv = 4 * wide
    src_rope = src_ckv + kv_rank
    src_conv = src_rope + MLA_ROPE
    src_gate = src_conv + 3 * wide
    assert src_gate + N_BRANCH * d == cols
    dst_ckv = 7 * wide
    dst_rope = dst_ckv + kv_rank
    dst_gate = 8 * wide
    out_rows = dst_gate + N_BRANCH * d
    tr = 256
    assert dst_rope % tr == 0 and kv_rank % tr == 0 and (N_BRANCH * d) % tr == 0 and tr >= LANES

    def src_row(l, i):
        r = i * tr
        src = jnp.where(r < 4 * wide, r,
                        jnp.where(r < dst_ckv, r - 4 * wide + src_conv,
                                  jnp.where(r < dst_rope, r - dst_ckv + src_ckv,
                                            jnp.where(r < dst_gate, src_rope, r - dst_gate + src_gate))))
        return (l, pl.multiple_of(src, MLA_ROPE), 0)

    vm = 2 * (_nbytes((tr, d), F32) + _nbytes((tr, d), MXU_DTYPE)) + _nbytes((tr, d), F32)
    return pl.pallas_call(
        functools.partial(_regroup_kernel, pad_from=dst_rope, pad_to=dst_gate),
        name="regroup_in_proj",
        out_shape=jax.ShapeDtypeStruct((n_layer, out_rows, d), MXU_DTYPE),
        grid=(n_layer, out_rows // tr),
        in_specs=[pl.BlockSpec((pl.Element(1), pl.Element(tr), pl.Element(d)), src_row)],
        out_specs=pl.BlockSpec((None, tr, d), lambda l, i: (l, i, 0)),
        compiler_params=_cparams(2, vm),
    )(w_in_t)


def _rope_lane_layout(w):
    half = MLA_ROPE // 2
    z = jnp.zeros(w.shape[:-1] + (LANES // 2 - half,), w.dtype)
    return jnp.concatenate([w[..., :half], z, w[..., half:], z], axis=-1)


def _rope_tables(seq, nc):
    pos = jnp.arange(seq, dtype=jnp.int32)
    n_freq = MLA_ROPE // 4
    inv_freq = jnp.power(ROPE_BASE, -jnp.arange(n_freq, dtype=F32) / n_freq)
    row = (pos // GRID_W).astype(F32)
    col = (pos % GRID_W).astype(F32)
    ang = jnp.concatenate([row[:, None] * inv_freq, col[:, None] * inv_freq], axis=-1)
    cos, sin = jnp.cos(ang), jnp.sin(ang)
    cos_l = _rope_lane_layout(jnp.concatenate([cos, cos], axis=-1))
    sin_l = _rope_lane_layout(jnp.concatenate([-sin, sin], axis=-1))
    one = _rope_lane_layout(jnp.ones((nc, MLA_ROPE), F32))
    return one, jnp.zeros((nc, LANES), F32), cos_l, sin_l


def kernel(x, c, ctx, c_ctx, w_ada, b_ada, norm_mix, norm_ffn, w_in, na_rpb, mla_q_norm, mla_kv_norm, mla_w_uq, mla_w_ukv, conv_w, w_branch_a, w_branch_b, w_branch_c, w_out, peer_w_q, peer_sub_keys, peer_u, peer_v, final_norm):
    n_batch, seq, d = x.shape
    lc = ctx.shape[1]
    depth = w_ada.shape[0]
    nc = n_batch * lc
    nl = n_batch * seq
    na_w = NA_HEADS * HEAD_DIM
    q_rank, kv_rank = mla_w_uq.shape[1], mla_w_ukv.shape[1]
    sc_w = conv_w.shape[2]
    rows = seq // GRID_W
    assert nc == ROW_BLOCK and seq % ROW_BLOCK == 0 and d % LANES == 0
    assert rows % NA_Q_ROWS == 0 and rows >= NA_K_ROWS
    assert na_w == q_rank == sc_w == MLA_HEADS * MLA_V and kv_rank * 2 == na_w
    kw = dict(nl=nl, seq=seq, n_batch=n_batch)

    wide = na_w
    col_gate = 8 * wide
    proj_cols = col_gate + N_BRANCH * d

    cmat = jnp.zeros((SUBLANES, d), F32).at[:n_batch].set(c).at[n_batch].set(c_ctx)
    xs = jnp.concatenate([x.reshape(nl, d), ctx.reshape(nc, d)], axis=0)

    na_pairs, na_pair_of, case_of_blk, dc_hot, col_ok = _na_bias_tables(rows)
    one_c, zero_c, cos_l, sin_l = _rope_tables(seq, nc)
    cos_t = jnp.concatenate([cos_l] * n_batch + [one_c], axis=0)
    sin_t = jnp.concatenate([sin_l] * n_batch + [zero_c], axis=0)

    w_proj = _regroup_in_proj(jnp.swapaxes(w_in, 1, 2), wide=wide, kv_rank=kv_rank, d=d)
    b_ada3 = b_ada.reshape(depth, 1, N_MOD * d)
    wa_c, wb_c, wc_c = (w.astype(MXU_DTYPE) for w in (w_branch_a, w_branch_b, w_branch_c))
    w_out_c = w_out.astype(MXU_DTYPE)
    w_q_c = peer_w_q.astype(MXU_DTYPE)
    u_c = peer_u.astype(MXU_DTYPE)
    v_c = peer_v.astype(MXU_DTYPE)

    for l in range(depth):
        last = l == depth - 1
        skip = 1 if last else 0
        wq = mla_w_uq[l].reshape(q_rank, MLA_HEADS, MLA_NOPE + MLA_ROPE)
        wq = jnp.concatenate([wq[..., :MLA_NOPE], _rope_lane_layout(wq[..., MLA_NOPE:])], axis=-1)
        wq = wq.reshape(q_rank, MLA_HEADS * MLA_HEAD_PAD).astype(MXU_DTYPE)
        wkv = mla_w_ukv[l].reshape(kv_rank, MLA_HEADS, MLA_NOPE + MLA_V)
        wk = wkv[..., :MLA_NOPE].reshape(kv_rank, MLA_HEADS * MLA_NOPE).astype(MXU_DTYPE)
        wv = wkv[..., MLA_NOPE:].reshape(kv_rank, MLA_HEADS * MLA_V).astype(MXU_DTYPE)
        pair_table = _na_pair_table(na_rpb[l], na_pairs, dc_hot, col_ok)

        mod = _ada_table(cmat, w_ada, b_ada3, l)
        modtab = mod[:n_batch + 1].reshape((n_batch + 1) * N_MOD, 1, d)

        h = _normmod(xs, norm_mix[l], modtab, 0, skip=0, **kw)
        proj = _matmul(h, w_proj, l, tn=_tile(proj_cols, (2048, 1024, 512, 256)), out_dtype=MXU_DTYPE, w_rows=True,
                       name="in_proj")
        hd = wide // HEAD_DIM
        ya = _na_attention(proj, pair_table, jnp.asarray(case_of_blk), jnp.asarray(na_pair_of), lc=lc,
                           col_q=0, col_k=hd, col_v=2 * hd, **kw)
        qf, kf, vf = _mla_up(proj, mla_q_norm[l], mla_kv_norm[l], wq, wk, wv, cos_t, sin_t,
                             col_cq=3, col_ckv=7 * wide // kv_rank, col_kr=(7 * wide + kv_rank) // LANES)
        yb = _mla_attention(qf, kf, vf, lc=lc, **kw)
        if not last:
            ya = _ctx_attention(proj, proj, proj, ya, nl=nl, lc=lc, n_batch=n_batch, n_heads=NA_HEADS, dk=HEAD_DIM,
                                dv=HEAD_DIM, col_q=0, col_k=hd, col_v=2 * hd, scale=HEAD_DIM ** -0.5)
            yb = _ctx_attention(qf, kf, vf, yb, nl=nl, lc=lc, n_batch=n_batch, n_heads=MLA_HEADS, dk=MLA_HEAD_PAD,
                                dv=MLA_V, col_q=0, col_k=0, col_v=0, scale=(MLA_NOPE + MLA_ROPE) ** -0.5)
        tn_m = 1024 if d % 1024 == 0 else d
        m = _merge(ya, yb, proj, conv_w, wa_c, wb_c, wc_c, l, d=d, nl=nl, seq=seq, lc=lc, skip=skip,
                   col_gb=4, col_gc=5, col_xin=6, col_gate=col_gate // tn_m)
        xs = _matmul_resid(m, w_out_c, l, xs, modtab, 2, tn=tn_m, tk=d, skip=skip, name="out_proj", **kw)

        h2 = _normmod(xs, norm_ffn[l], modtab, 3, skip=skip, **kw)
        pq = _matmul(h2, w_q_c, l, tn=2048, out_dtype=MXU_DTYPE, skip=skip, name="peer_query")
        keys = peer_sub_keys[l].reshape(PEER_HEADS, 2 * PEER_NKEYS, PEER_KEY_DIM // 2).astype(MXU_DTYPE)
        gates_t, i1_t, i2_t = _peer_topk(pq, keys, skip=skip)
        gmat = _peer_gates(i1_t.T, i2_t.T, gates_t.T, skip=skip)
        hexp = _peer_up(h2, u_c, l, gmat, skip=skip)
        xs = _matmul_resid(hexp, v_c, l, xs, modtab, 5, tn=min(d, 2048), tk=2048, skip=skip, name="peer_down",
                           **kw)

    out = _final_norm(xs, final_norm, nl=nl)
    return out.reshape(n_batch, seq, d)
```

```python
import functools
import math

import numpy as np
import jax
import jax.numpy as jnp
from jax import lax
from jax.experimental import pallas as pl
from jax.experimental.pallas import tpu as pltpu

F32 = jnp.float32
MXU_DTYPE = jnp.bfloat16

GRID_W = 64
EPS = 1e-6
N_MOD = 6
NEG_INF = -1e30
HEAD_DIM = 128
NA_HEADS = 8
NA_WIN_H = 8
NA_WIN_W = 16
MLA_HEADS = 8
MLA_NOPE = 128
MLA_ROPE = 64
MLA_V = 128
ROPE_BASE = 10000.0
SC_TAPS = 3
N_BRANCH = 3
PEER_HEADS = 8
PEER_NKEYS = 128
PEER_KEY_DIM = 256
PEER_TOPK = 16

V7X_VMEM_BYTES = 64 * 1024 * 1024
LANES = 128
SUBLANES = 8

ROW_BLOCK = 512
MLA_HEAD_PAD = 256
NA_Q_ROWS = 8
NA_K_ROWS = 16
NA_K_CHUNK_ROWS = 4


def _cparams(n_axes, vmem_bytes):
    limit = min(int(vmem_bytes) + (4 << 20), V7X_VMEM_BYTES - (6 << 20))
    return pltpu.CompilerParams(dimension_semantics=("arbitrary",) * n_axes,
                                vmem_limit_bytes=limit)


def _nbytes(shape, dtype):
    return int(np.prod(shape)) * jnp.dtype(dtype).itemsize


def _tile(n, prefs):
    return next(t for t in prefs if n % t == 0)


def _mod_row(row0, nl, seq, n_batch):
    return jnp.where(row0 >= nl, n_batch, row0 // seq)


def _ada_kernel(c_ref, w_ref, b_ref, o_ref):
    c = c_ref[...]
    s = (c * jax.nn.sigmoid(c)).astype(MXU_DTYPE)
    acc = jnp.dot(s, w_ref[...].astype(MXU_DTYPE), preferred_element_type=F32)
    o_ref[...] = acc + b_ref[...]


def _ada_table(cmat, w_ada, b_ada, layer):
    rows, d = cmat.shape
    n = w_ada.shape[2]
    tn = 512
    vm = 2 * (_nbytes((d, tn), F32) + _nbytes((rows, tn), F32) * 2) + _nbytes((rows, d), F32) * 2 \
        + _nbytes((d, tn), MXU_DTYPE)
    return pl.pallas_call(
        _ada_kernel, name="ada_table",
        out_shape=jax.ShapeDtypeStruct((rows, n), F32),
        grid=(n // tn,),
        in_specs=[pl.BlockSpec((rows, d), lambda j: (0, 0)),
                  pl.BlockSpec((None, d, tn), lambda j: (layer, 0, j)),
                  pl.BlockSpec((None, 1, tn), lambda j: (layer, 0, j))],
        out_specs=pl.BlockSpec((rows, tn), lambda j: (0, j)),
        compiler_params=_cparams(1, vm),
    )(cmat, w_ada, b_ada)


def _normmod_kernel(x_ref, g_ref, sh_ref, sc_ref, o_ref):
    x = x_ref[...]
    y = x * lax.rsqrt(jnp.mean(x * x, axis=-1, keepdims=True) + EPS)
    o_ref[...] = ((y * g_ref[...]) * (1.0 + sc_ref[0]) + sh_ref[0]).astype(o_ref.dtype)


def _normmod(x, gain, modtab, m_shift, *, nl, seq, n_batch, skip):
    n, d = x.shape
    tb = 256
    off = skip * (ROW_BLOCK // tb)

    def mrow(i, m):
        return (_mod_row(i * tb, nl, seq, n_batch) * N_MOD + m, 0, 0)

    vm = 2 * (_nbytes((tb, d), F32) + _nbytes((tb, d), MXU_DTYPE)) + 3 * _nbytes((tb, d), F32)
    return pl.pallas_call(
        _normmod_kernel, name="normmod",
        out_shape=jax.ShapeDtypeStruct((n, d), MXU_DTYPE),
        grid=(n // tb - off,),
        in_specs=[pl.BlockSpec((tb, d), lambda i: (i, 0)),
                  pl.BlockSpec((1, d), lambda i: (0, 0)),
                  pl.BlockSpec((1, 1, d), lambda i: mrow(i, m_shift)),
                  pl.BlockSpec((1, 1, d), lambda i: mrow(i, m_shift + 1))],
        out_specs=pl.BlockSpec((tb, d), lambda i: (i, 0)),
        compiler_params=_cparams(1, vm),
    )(x, gain.reshape(1, d), modtab, modtab)


def _mm_kernel(x_ref, w_ref, o_ref):
    o_ref[...] = jnp.dot(x_ref[...], w_ref[...], preferred_element_type=F32).astype(o_ref.dtype)


def _mm_nt_kernel(x_ref, w_ref, o_ref):
    o_ref[...] = lax.dot_general(x_ref[...], w_ref[...], (((1,), (1,)), ((), ())),
                                 preferred_element_type=F32).astype(o_ref.dtype)


def _matmul(x, w, layer, *, tn, out_dtype, skip=0, w_rows=False, name="matmul"):
    n, k = x.shape
    nn = w.shape[1] if w_rows else w.shape[2]
    tm = ROW_BLOCK
    vm = 2 * (_nbytes((tm, k), x.dtype) + _nbytes((k, tn), w.dtype) + _nbytes((tm, tn), out_dtype)) \
        + _nbytes((tm, tn), F32)
    w_spec = (pl.BlockSpec((None, tn, k), lambda j, i: (layer, j, 0)) if w_rows
              else pl.BlockSpec((None, k, tn), lambda j, i: (layer, 0, j)))
    return pl.pallas_call(
        _mm_nt_kernel if w_rows else _mm_kernel, name=name,
        out_shape=jax.ShapeDtypeStruct((n, nn), out_dtype),
        grid=(nn // tn, n // tm - skip),
        in_specs=[pl.BlockSpec((tm, k), lambda j, i: (i, 0)), w_spec],
        out_specs=pl.BlockSpec((tm, tn), lambda j, i: (i, j)),
        compiler_params=_cparams(2, vm),
    )(x, w)


def _mm_resid_kernel(x_ref, w_ref, r_ref, g_ref, o_ref):
    @pl.when(pl.program_id(2) == 0)
    def _():
        o_ref[...] = r_ref[...]

    o_ref[...] += g_ref[0] * jnp.dot(x_ref[...], w_ref[...], preferred_element_type=F32)


def _matmul_resid(x, w, layer, resid, modtab, m_gate, *, tn, tk, nl, seq, n_batch, skip, name):
    n, kk = x.shape
    nn = w.shape[2]
    tm = ROW_BLOCK

    def mrow(i, j, k):
        return (_mod_row(i * tm, nl, seq, n_batch) * N_MOD + m_gate, 0, j)

    vm = 2 * (_nbytes((tm, tk), x.dtype) + _nbytes((tk, tn), w.dtype) + 2 * _nbytes((tm, tn), F32)) \
        + 2 * _nbytes((tm, tn), F32)
    return pl.pallas_call(
        _mm_resid_kernel, name=name,
        out_shape=jax.ShapeDtypeStruct((n, nn), F32),
        grid=(n // tm - skip, nn // tn, kk // tk),
        in_specs=[pl.BlockSpec((tm, tk), lambda i, j, k: (i, k)),
                  pl.BlockSpec((None, tk, tn), lambda i, j, k: (layer, k, j)),
                  pl.BlockSpec((tm, tn), lambda i, j, k: (i, j)),
                  pl.BlockSpec((1, 1, tn), mrow)],
        out_specs=pl.BlockSpec((tm, tn), lambda i, j, k: (i, j)),
        compiler_params=_cparams(3, vm),
    )(x, w, resid, modtab)


def _na_bias_tables(rows):
    assert 2 * GRID_W == LANES and NA_K_ROWS % 2 == 0
    n_dr = 2 * NA_WIN_H - 1
    n_blk = rows // NA_Q_ROWS
    rl = np.arange(NA_Q_ROWS)[:, None]
    krl = np.arange(NA_K_ROWS)[None, :]
    cases, case_of_blk = [], []
    for i in range(n_blk):
        r0 = i * NA_Q_ROWS
        k0 = int(np.clip(r0 - NA_K_CHUNK_ROWS, 0, rows - NA_K_ROWS))
        r = r0 + rl
        kr = k0 + krl
        start = np.clip(r - NA_WIN_H // 2, 0, rows - NA_WIN_H)
        ok = (kr >= start) & (kr < start + NA_WIN_H)
        dr = np.where(ok, kr - r + NA_WIN_H - 1, n_dr).astype(np.int32)
        assert dr.min() >= 0
        for ci, known in enumerate(cases):
            if np.array_equal(known, dr):
                case_of_blk.append(ci)
                break
        else:
            case_of_blk.append(len(cases))
            cases.append(dr)
    c = np.arange(GRID_W)[:, None]
    kc = np.arange(GRID_W)[None, :]
    ws = np.clip(c - NA_WIN_W // 2, 0, GRID_W - NA_WIN_W)
    col_ok = (kc >= ws) & (kc < ws + NA_WIN_W)
    dc = np.clip(kc - c + NA_WIN_W - 1, 0, 2 * NA_WIN_W - 2)
    dc_hot = (dc[None] == np.arange(2 * NA_WIN_W - 1)[:, None, None]).astype(np.float32)
    dr_of = np.stack(cases).reshape(len(cases), NA_Q_ROWS, NA_K_ROWS // 2, 2)
    pairs, pair_of = np.unique(dr_of.reshape(-1, 2), axis=0, return_inverse=True)
    return (pairs.astype(np.int32), pair_of.reshape(-1).astype(np.int32), np.asarray(case_of_blk, np.int32),
            dc_hot, col_ok)


def _na_pair_table(rpb, pairs, dc_hot, col_ok):
    n_h = rpb.shape[0]
    tb = jnp.einsum('hdk,kcq->hdcq', rpb.astype(F32), dc_hot, precision=lax.Precision.HIGHEST)
    tb = jnp.where(col_ok, tb, NEG_INF)
    tb = jnp.concatenate([tb, jnp.full((n_h, 1, GRID_W, GRID_W), NEG_INF, F32)], axis=1)
    return jnp.concatenate([tb[:, pairs[:, 0]], tb[:, pairs[:, 1]]], axis=-1)


def _na_kernel(case_ref, pair_ref, q_ref, k0_ref, k1_ref, k2_ref, k3_ref, v0_ref, v1_ref, v2_ref, v3_ref,
               kc_ref, vc_ref, b_ref, o_ref):
    scale = HEAD_DIM ** -0.5
    nt = (((1,), (1,)), ((), ()))
    n_pair = NA_K_ROWS // 2
    q = q_ref[...]
    k_loc = jnp.concatenate([k0_ref[...], k1_ref[...], k2_ref[...], k3_ref[...]], axis=0)
    v_loc = jnp.concatenate([v0_ref[...], v1_ref[...], v2_ref[...], v3_ref[...]], axis=0)
    s_loc = lax.dot_general(q, k_loc, nt, preferred_element_type=F32) * scale
    base = case_ref[pl.program_id(2)] * (NA_Q_ROWS * n_pair)
    biased = []
    for rl in range(NA_Q_ROWS):
        blocks = [b_ref[0, pair_ref[base + rl * n_pair + p]] for p in range(n_pair)]
        biased.append(s_loc[rl * GRID_W:(rl + 1) * GRID_W, :] + jnp.concatenate(blocks, axis=1))
    s_loc = jnp.concatenate(biased, axis=0)
    s_ctx = lax.dot_general(q, kc_ref[...], nt, preferred_element_type=F32) * scale
    m = jnp.maximum(jnp.max(s_loc, axis=-1, keepdims=True), jnp.max(s_ctx, axis=-1, keepdims=True))
    p_loc = jnp.exp(s_loc - m)
    p_ctx = jnp.exp(s_ctx - m)
    denom = jnp.sum(p_loc, axis=-1, keepdims=True) + jnp.sum(p_ctx, axis=-1, keepdims=True)
    o = (jnp.dot(p_loc.astype(MXU_DTYPE), v_loc, preferred_element_type=F32)
         + jnp.dot(p_ctx.astype(MXU_DTYPE), vc_ref[...], preferred_element_type=F32))
    o_ref[...] = (o / denom).astype(o_ref.dtype)


def _na_attention(proj, pair_table, case_of_blk, pair_of, *, nl, seq, lc, n_batch, col_q, col_k, col_v):
    n = proj.shape[0]
    rows = seq // GRID_W
    tq = NA_Q_ROWS * GRID_W
    tkc = NA_K_CHUNK_ROWS * GRID_W
    n_blk = rows // NA_Q_ROWS
    n_chunk = NA_K_ROWS // NA_K_CHUNK_ROWS
    max_c0 = rows // NA_K_CHUNK_ROWS - n_chunk

    def qmap(h, b, i, case, pair):
        return (b * (seq // tq) + i, col_q + h)

    def kvmap(col, j):
        def f(h, b, i, case, pair):
            c0 = jnp.clip(i * (NA_Q_ROWS // NA_K_CHUNK_ROWS) - 1, 0, max_c0)
            return (b * (seq // tkc) + c0 + j, col + h)
        return f

    def ctxmap(col):
        return lambda h, b, i, case, pair: (nl // lc + b, col + h)

    in_specs = [pl.BlockSpec((tq, HEAD_DIM), qmap)]
    in_specs += [pl.BlockSpec((tkc, HEAD_DIM), kvmap(col_k, j)) for j in range(n_chunk)]
    in_specs += [pl.BlockSpec((tkc, HEAD_DIM), kvmap(col_v, j)) for j in range(n_chunk)]
    in_specs += [pl.BlockSpec((lc, HEAD_DIM), ctxmap(col_k)), pl.BlockSpec((lc, HEAD_DIM), ctxmap(col_v))]
    in_specs += [pl.BlockSpec((1,) + pair_table.shape[1:], lambda h, b, i, case, pair: (h, 0, 0, 0))]
    vm = 2 * (_nbytes(pair_table.shape[1:], F32) + 12 * _nbytes((tq, HEAD_DIM), MXU_DTYPE)) \
        + 7 * _nbytes((tq, NA_K_ROWS * GRID_W + lc), F32)
    grid_spec = pltpu.PrefetchScalarGridSpec(
        num_scalar_prefetch=2, grid=(NA_HEADS, n_batch, n_blk),
        in_specs=in_specs,
        out_specs=pl.BlockSpec((tq, HEAD_DIM), lambda h, b, i, case, pair: (b * (seq // tq) + i, h)))
    return pl.pallas_call(
        _na_kernel, name="na_attention",
        out_shape=jax.ShapeDtypeStruct((n, NA_HEADS * HEAD_DIM), MXU_DTYPE),
        grid_spec=grid_spec,
        compiler_params=_cparams(3, vm),
    )(case_of_blk, pair_of, *([proj] * (1 + 2 * n_chunk + 2)), pair_table)


def _ctx_attn_kernel(q_ref, k_ref, v_ref, prev_ref, o_ref, *, scale):
    del prev_ref
    s = lax.dot_general(q_ref[...], k_ref[...], (((1,), (1,)), ((), ())),
                        preferred_element_type=F32) * scale
    m = jnp.max(s, axis=-1, keepdims=True)
    p = jnp.exp(s - m)
    o = jnp.dot(p.astype(MXU_DTYPE), v_ref[...], preferred_element_type=F32)
    o_ref[...] = (o / jnp.sum(p, axis=-1, keepdims=True)).astype(o_ref.dtype)


def _ctx_attention(qa, ka, va, y, *, nl, lc, n_batch, n_heads, dk, dv, col_q, col_k, col_v, scale):
    vm = 2 * (2 * _nbytes((lc, dk), MXU_DTYPE) + 2 * _nbytes((lc, dv), MXU_DTYPE)) + 4 * _nbytes((lc, lc), F32)
    return pl.pallas_call(
        functools.partial(_ctx_attn_kernel, scale=scale), name="ctx_attention",
        out_shape=jax.ShapeDtypeStruct(y.shape, y.dtype),
        grid=(n_batch, n_heads),
        in_specs=[pl.BlockSpec((lc, dk), lambda b, h: (nl // lc + b, col_q + h)),
                  pl.BlockSpec((lc, dk), lambda b, h: (nl // lc + b, col_k + h)),
                  pl.BlockSpec((lc, dv), lambda b, h: (nl // lc + b, col_v + h)),
                  pl.BlockSpec(memory_space=pl.ANY)],
        out_specs=pl.BlockSpec((lc, dv), lambda b, h: (nl // lc + b, h)),
        input_output_aliases={3: 0},
        compiler_params=_cparams(2, vm),
    )(qa, ka, va, y)


def _mla_up_kernel(cq_ref, ckv_ref, kr_ref, qn_ref, kvn_ref, wq_ref, wk_ref, wv_ref, cos_ref, sin_ref,
                   q_ref, k_ref, v_ref):
    def rms(x, g):
        return (x * lax.rsqrt(jnp.mean(x * x, axis=-1, keepdims=True) + EPS) * g).astype(MXU_DTYPE)

    cos = cos_ref[...]
    sin = sin_ref[...]

    def rope(x):
        return x * cos + pltpu.roll(x, 2 * (MLA_ROPE // 2), axis=1) * sin

    cqn = rms(cq_ref[...].astype(F32), qn_ref[...])
    kvn = rms(ckv_ref[...].astype(F32), kvn_ref[...])
    q = jnp.dot(cqn, wq_ref[...], preferred_element_type=F32)
    kn = jnp.dot(kvn, wk_ref[...], preferred_element_type=F32)
    v_ref[...] = jnp.dot(kvn, wv_ref[...], preferred_element_type=F32).astype(v_ref.dtype)
    kr = rope(kr_ref[...].astype(F32)).astype(k_ref.dtype)
    for h in range(MLA_HEADS):
        c0 = h * MLA_HEAD_PAD
        q_ref[:, c0:c0 + MLA_NOPE] = q[:, c0:c0 + MLA_NOPE].astype(q_ref.dtype)
        q_ref[:, c0 + MLA_NOPE:c0 + MLA_HEAD_PAD] = rope(q[:, c0 + MLA_NOPE:c0 + MLA_HEAD_PAD]).astype(q_ref.dtype)
        k_ref[:, c0:c0 + MLA_NOPE] = kn[:, h * MLA_NOPE:(h + 1) * MLA_NOPE].astype(k_ref.dtype)
        k_ref[:, c0 + MLA_NOPE:c0 + MLA_HEAD_PAD] = kr


def _mla_up(proj, q_norm, kv_norm, wq, wk, wv, cos_t, sin_t, *, col_cq, col_ckv, col_kr):
    n = proj.shape[0]
    tm = 256
    rq, rkv = wq.shape[0], wk.shape[0]
    wq_cols = wq.shape[1]
    vm = 2 * (_nbytes(wq.shape, MXU_DTYPE) + _nbytes(wk.shape, MXU_DTYPE) + _nbytes(wv.shape, MXU_DTYPE)
              + _nbytes((tm, rq + rkv + LANES), MXU_DTYPE) + 2 * _nbytes((tm, LANES), F32)
              + _nbytes((tm, 2 * wq_cols + wv.shape[1]), MXU_DTYPE)) + 3 * _nbytes((tm, wq_cols), F32)
    return pl.pallas_call(
        _mla_up_kernel, name="mla_up",
        out_shape=(jax.ShapeDtypeStruct((n, wq_cols), MXU_DTYPE),
                   jax.ShapeDtypeStruct((n, wq_cols), MXU_DTYPE),
                   jax.ShapeDtypeStruct((n, wv.shape[1]), MXU_DTYPE)),
        grid=(n // tm,),
        in_specs=[pl.BlockSpec((tm, rq), lambda i: (i, col_cq)),
                  pl.BlockSpec((tm, rkv), lambda i: (i, col_ckv)),
                  pl.BlockSpec((tm, LANES), lambda i: (i, col_kr)),
                  pl.BlockSpec((1, rq), lambda i: (0, 0)),
                  pl.BlockSpec((1, rkv), lambda i: (0, 0)),
                  pl.BlockSpec(wq.shape, lambda i: (0, 0)),
                  pl.BlockSpec(wk.shape, lambda i: (0, 0)),
                  pl.BlockSpec(wv.shape, lambda i: (0, 0)),
                  pl.BlockSpec((tm, LANES), lambda i: (i, 0)),
                  pl.BlockSpec((tm, LANES), lambda i: (i, 0))],
        out_specs=(pl.BlockSpec((tm, wq_cols), lambda i: (i, 0)),
                   pl.BlockSpec((tm, wq_cols), lambda i: (i, 0)),
                   pl.BlockSpec((tm, wv.shape[1]), lambda i: (i, 0))),
        compiler_params=_cparams(1, vm),
    )(proj, proj, proj, q_norm.reshape(1, rq), kv_norm.reshape(1, rkv), wq, wk, wv, cos_t, sin_t)


def _flash_kernel(q_ref, kc_ref, kl_ref, vc_ref, vl_ref, o_ref, m_ref, l_ref, acc_ref, *, scale, tk):
    nt = (((1,), (1,)), ((), ()))
    q = q_ref[...]
    c = scale * math.log2(math.e)

    def chunk(k, v):
        s = lax.dot_general(q, k, nt, preferred_element_type=F32)
        m_old = m_ref[...]
        m_new = jnp.maximum(m_old, jnp.max(s, axis=-1, keepdims=True) * c)
        alpha = jnp.exp2(m_old - m_new)
        p = jnp.exp2(s * c - jnp.tile(m_new, (1, s.shape[1] // LANES)))
        l_ref[...] = alpha * l_ref[...] + jnp.sum(p, axis=-1, keepdims=True)
        acc_ref[...] = alpha * acc_ref[...] + jnp.dot(p.astype(MXU_DTYPE), v, preferred_element_type=F32)
        m_ref[...] = m_new

    m_ref[...] = jnp.full_like(m_ref, -jnp.inf)
    l_ref[...] = jnp.zeros_like(l_ref)
    acc_ref[...] = jnp.zeros_like(acc_ref)
    chunk(kc_ref[...], vc_ref[...])

    def body(j, carry):
        r0 = pl.multiple_of(j * tk, tk)
        chunk(kl_ref[pl.ds(r0, tk), :], vl_ref[pl.ds(r0, tk), :])
        return carry

    lax.fori_loop(0, kl_ref.shape[0] // tk, body, 0)
    o_ref[...] = (acc_ref[...] / l_ref[...]).astype(o_ref.dtype)


def _mla_attention(qf, kf, vf, *, nl, seq, lc, n_batch):
    n = qf.shape[0]
    tq, tk = _tile(seq, (2048, 1024, 512)), 512
    dk, dv = MLA_HEAD_PAD, MLA_V
    assert dv == LANES
    scale = (MLA_NOPE + MLA_ROPE) ** -0.5
    vm = 2 * (_nbytes((tq, dk), MXU_DTYPE) + _nbytes((lc + seq, dk + dv), MXU_DTYPE) + _nbytes((tq, dv), MXU_DTYPE)) \
        + 6 * _nbytes((tq, tk), F32) + 2 * _nbytes((tq, dv), F32)
    return pl.pallas_call(
        functools.partial(_flash_kernel, scale=scale, tk=tk), name="mla_attention",
        out_shape=jax.ShapeDtypeStruct((n, MLA_HEADS * dv), MXU_DTYPE),
        grid=(n_batch, MLA_HEADS, seq // tq),
        in_specs=[pl.BlockSpec((tq, dk), lambda b, h, i: (b * (seq // tq) + i, h)),
                  pl.BlockSpec((lc, dk), lambda b, h, i: (nl // lc + b, h)),
                  pl.BlockSpec((seq, dk), lambda b, h, i: (b, h)),
                  pl.BlockSpec((lc, dv), lambda b, h, i: (nl // lc + b, h)),
                  pl.BlockSpec((seq, dv), lambda b, h, i: (b, h))],
        out_specs=pl.BlockSpec((tq, dv), lambda b, h, i: (b * (seq // tq) + i, h)),
        scratch_shapes=[pltpu.VMEM((tq, LANES), F32), pltpu.VMEM((tq, LANES), F32), pltpu.VMEM((tq, dv), F32)],
        compiler_params=_cparams(3, vm),
    )(qf, kf, kf, vf, vf)


def _merge_kernel(ya_ref, yb_ref, gb_ref, gc_ref, xin_ref, gcp_ref, xinp_ref, gcn_ref, xinn_ref, cw_ref,
                  wa_ref, wb_ref, wc_ref, g0_ref, g1_ref, g2_ref, o_ref, yc_ref, *, nl, seq, lc):
    i = pl.program_id(0)
    j = pl.program_id(1)
    tm = ya_ref.shape[0]

    @pl.when(j == 0)
    def _():
        u = gc_ref[...].astype(F32) * xin_ref[...].astype(F32)
        u_prev_halo = (gcp_ref[...].astype(F32) * xinp_ref[...].astype(F32))[SUBLANES - 1:SUBLANES, :]
        u_next_halo = (gcn_ref[...].astype(F32) * xinn_ref[...].astype(F32))[0:1, :]
        row = lax.broadcasted_iota(jnp.int32, (tm, 1), 0)
        g = i * tm + row
        pos = jnp.where(g < nl, g % seq, (g - nl) % lc)
        seq_len = jnp.where(g < nl, seq, lc)
        u_dn = jnp.where(row == 0, u_prev_halo, pltpu.roll(u, 1, axis=0))
        u_up = jnp.where(row == tm - 1, u_next_halo, pltpu.roll(u, tm - 1, axis=0))
        u_dn = jnp.where(pos == 0, 0.0, u_dn)
        u_up = jnp.where(pos == seq_len - 1, 0.0, u_up)
        cw = cw_ref[...]
        y = cw[0:1, :] * u_dn + cw[1:2, :] * u + cw[2:3, :] * u_up
        yc_ref[...] = (gb_ref[...].astype(F32) * y).astype(yc_ref.dtype)

    def branch(y, w_ref, g_ref):
        gate = jax.nn.sigmoid(g_ref[...].astype(F32))
        return gate * jnp.dot(y, w_ref[...], preferred_element_type=F32)

    m = (branch(ya_ref[...], wa_ref, g0_ref) + branch(yb_ref[...], wb_ref, g1_ref)
         + branch(yc_ref[...], wc_ref, g2_ref))
    o_ref[...] = m.astype(o_ref.dtype)


def _merge(ya, yb, proj, conv_w, wa, wb, wc, layer, *, d, nl, seq, lc, skip, col_gb, col_gc, col_xin, col_gate):
    n = ya.shape[0]
    tm, tn = ROW_BLOCK, 1024 if d % 1024 == 0 else d
    w = ya.shape[1]
    hb = tm // SUBLANES
    last_halo = n // SUBLANES - 1
    n_col = d // tn

    def row(i, j):
        return (i, 0)

    def cur(col):
        return lambda i, j: (i, col)

    def prev(col):
        return lambda i, j: (jnp.maximum(i * hb - 1, 0), col)

    def nxt(col):
        return lambda i, j: (jnp.minimum((i + 1) * hb, last_halo), col)

    def gate(br):
        return lambda i, j: (i, col_gate + br * n_col + j)

    in_specs = [pl.BlockSpec((tm, w), row), pl.BlockSpec((tm, w), row),
                pl.BlockSpec((tm, w), cur(col_gb)), pl.BlockSpec((tm, w), cur(col_gc)),
                pl.BlockSpec((tm, w), cur(col_xin)),
                pl.BlockSpec((SUBLANES, w), prev(col_gc)), pl.BlockSpec((SUBLANES, w), prev(col_xin)),
                pl.BlockSpec((SUBLANES, w), nxt(col_gc)), pl.BlockSpec((SUBLANES, w), nxt(col_xin)),
                pl.BlockSpec((None, SC_TAPS, w), lambda i, j: (layer, 0, 0)),
                pl.BlockSpec((None, w, tn), lambda i, j: (layer, 0, j)),
                pl.BlockSpec((None, w, tn), lambda i, j: (layer, 0, j)),
                pl.BlockSpec((None, w, tn), lambda i, j: (layer, 0, j)),
                pl.BlockSpec((tm, tn), gate(0)), pl.BlockSpec((tm, tn), gate(1)), pl.BlockSpec((tm, tn), gate(2))]
    vm = 2 * (5 * _nbytes((tm, w), MXU_DTYPE) + 3 * _nbytes((w, tn), MXU_DTYPE) + 4 * _nbytes((tm, tn), MXU_DTYPE)) \
        + _nbytes((tm, w), MXU_DTYPE) + 6 * _nbytes((tm, max(w, tn)), F32)
    return pl.pallas_call(
        functools.partial(_merge_kernel, nl=nl, seq=seq, lc=lc), name="conv_merge",
        out_shape=jax.ShapeDtypeStruct((n, d), MXU_DTYPE),
        grid=(n // tm - skip, n_col),
        in_specs=in_specs,
        out_specs=pl.BlockSpec((tm, tn), lambda i, j: (i, j)),
        scratch_shapes=[pltpu.VMEM((tm, w), MXU_DTYPE)],
        compiler_params=_cparams(2, vm),
    )(ya, yb, proj, proj, proj, proj, proj, proj, proj, conv_w, wa, wb, wc, proj, proj, proj)


def _sort_network(n):
    size = 1
    while size < n:
        size *= 2
    net = []

    def merge(lo, hi, r):
        step = r * 2
        if step < hi - lo:
            merge(lo, hi, step)
            merge(lo + r, hi, step)
            net.extend((i, i + r) for i in range(lo + r, hi - r, step))
        else:
            net.append((lo, lo + r))

    def sort(lo, hi):
        if hi - lo >= 1:
            mid = lo + (hi - lo) // 2
            sort(lo, mid)
            sort(mid + 1, hi)
            merge(lo, hi, 1)

    sort(0, size - 1)
    return [(i, j) for i, j in net if j < n]


def _topk_cols(s, k, ids=None):
    n_tile = s.shape[0] // SUBLANES
    v = [s[t * SUBLANES:(t + 1) * SUBLANES, :] for t in range(n_tile)]
    if ids is None:
        sub = lax.broadcasted_iota(jnp.int32, (SUBLANES, s.shape[1]), 0)
        r = [sub + t * SUBLANES for t in range(n_tile)]
    else:
        r = [ids[t * SUBLANES:(t + 1) * SUBLANES, :] for t in range(n_tile)]
    for a, b in _sort_network(n_tile):
        first = (v[a] > v[b]) | ((v[a] == v[b]) & (r[a] < r[b]))
        v[a], v[b] = jnp.where(first, v[a], v[b]), jnp.where(first, v[b], v[a])
        r[a], r[b] = jnp.where(first, r[a], r[b]), jnp.where(first, r[b], r[a])
    depth = min(n_tile, k)
    v, r = v[:depth], r[:depth]
    big = jnp.iinfo(jnp.int32).max
    vals, idxs = [], []
    for it in range(k):
        m = jnp.max(v[0], axis=0, keepdims=True)
        idx = jnp.min(jnp.where(v[0] == m, r[0], big), axis=0, keepdims=True)
        vals.append(m)
        idxs.append(idx)
        win = r[0] == idx
        live = min(depth, k - it)
        for t in range(live - 1):
            v[t] = jnp.where(win, v[t + 1], v[t])
            r[t] = jnp.where(win, r[t + 1], r[t])
        v[live - 1] = jnp.where(win, -jnp.inf, v[live - 1])
    return jnp.concatenate(vals, axis=0), jnp.concatenate(idxs, axis=0)


def _candidate_rows():
    k = PEER_TOPK
    groups = []
    for b0 in range(0, k, SUBLANES):
        groups.append((0, 1, b0, b0 + SUBLANES))
    a = 1
    while a < k and k // (a + 1) > 1:
        assert k // (a + 1) <= SUBLANES
        groups.append((a, a + 1, 0, SUBLANES))
        a += 1
    while a < k:
        groups.append((a, a + SUBLANES, 0, 1))
        a += SUBLANES
    covered = {(aa, bb) for a0, a1, b0, b1 in groups for aa in range(a0, a1) for bb in range(b0, b1)}
    assert all((aa, bb) in covered for aa in range(k) for bb in range(k) if (aa + 1) * (bb + 1) <= k)
    return groups


def _take_rows(table, idx):
    iota = lax.broadcasted_iota(jnp.int32, table.shape, 0)
    rows = [jnp.sum(jnp.where(iota == idx[j:j + 1, :], table, 0), axis=0, keepdims=True)
            for j in range(idx.shape[0])]
    return jnp.concatenate(rows, axis=0)


def _peer_topk_kernel(q_ref, keys_ref, g_ref, i1_ref, i2_ref):
    nt = (((1,), (1,)), ((), ()))
    half = PEER_KEY_DIM // 2
    q = q_ref[...]
    s1 = lax.dot_general(keys_ref[0, :PEER_NKEYS, :], q[:, :half], nt, preferred_element_type=F32)
    s2 = lax.dot_general(keys_ref[0, PEER_NKEYS:, :], q[:, half:], nt, preferred_element_type=F32)
    v1, j1 = _topk_cols(s1, PEER_TOPK)
    v2, j2 = _topk_cols(s2, PEER_TOPK)
    sub = lax.broadcasted_iota(jnp.int32, (SUBLANES, q.shape[0]), 0)
    cand, flat = [], []
    for a0, a1, b0, b1 in _candidate_rows():
        cand.append(v1[a0:a1, :] + v2[b0:b1, :])
        flat.append(sub * (1 if a1 - a0 == 1 else PEER_TOPK) + (a0 * PEER_TOPK + b0))
    top_s, pos = _topk_cols(jnp.concatenate(cand, axis=0), PEER_TOPK, jnp.concatenate(flat, axis=0))
    i1_ref[...] = _take_rows(j1, jnp.right_shift(pos, PEER_TOPK.bit_length() - 1))
    i2_ref[...] = _take_rows(j2, jnp.bitwise_and(pos, PEER_TOPK - 1))
    e = jnp.exp(top_s - top_s[0:1, :])
    g_ref[...] = e / jnp.sum(e, axis=0, keepdims=True)


def _peer_topk(q, keys, *, skip):
    n = q.shape[0]
    t = 256
    off = skip * (ROW_BLOCK // t)
    rows = PEER_HEADS * PEER_TOPK
    out_spec = pl.BlockSpec((PEER_TOPK, t), lambda i, h: (h, i))
    vm = 2 * (_nbytes((t, PEER_KEY_DIM), MXU_DTYPE) + _nbytes(keys.shape[1:], MXU_DTYPE)) \
        + 8 * _nbytes((PEER_TOPK * PEER_TOPK, t), F32)
    return pl.pallas_call(
        _peer_topk_kernel, name="peer_topk",
        out_shape=(jax.ShapeDtypeStruct((rows, n), F32), jax.ShapeDtypeStruct((rows, n), jnp.int32),
                   jax.ShapeDtypeStruct((rows, n), jnp.int32)),
        grid=(n // t - off, PEER_HEADS),
        in_specs=[pl.BlockSpec((t, PEER_KEY_DIM), lambda i, h: (i, h)),
                  pl.BlockSpec((1,) + keys.shape[1:], lambda i, h: (h, 0, 0))],
        out_specs=(out_spec, out_spec, out_spec),
        compiler_params=_cparams(2, vm),
    )(q, keys)


def _peer_gates_kernel(i1_ref, i2_ref, g_ref, o_ref):
    n_sel = i1_ref.shape[1]
    iota = lax.broadcasted_iota(jnp.int32, (PEER_NKEYS, n_sel), 0)
    nt = (((1,), (1,)), ((), ()))

    def body(t, carry):
        i1b = i1_ref[pl.ds(t, 1), :]
        i2b = i2_ref[pl.ds(t, 1), :]
        g = g_ref[pl.ds(t, 1), :]
        g_hi = g.astype(MXU_DTYPE).astype(F32)
        g_lo = g - g_hi
        m1 = iota == i1b
        lhs = jnp.concatenate([jnp.where(m1, g_hi, 0.0), jnp.where(m1, g_lo, 0.0)], axis=1).astype(MXU_DTYPE)
        hot2 = jnp.where(iota == i2b, 1.0, 0.0).astype(MXU_DTYPE)
        rhs = jnp.concatenate([hot2, hot2], axis=1)
        o_ref[t] = lax.dot_general(lhs, rhs, nt, preferred_element_type=F32)
        return carry

    lax.fori_loop(0, i1_ref.shape[0], body, 0, unroll=32)


def _peer_gates(i1, i2, g, *, skip):
    n, n_sel = i1.shape
    t = 128
    off = skip * (ROW_BLOCK // t)
    spec = pl.BlockSpec((t, n_sel), lambda i: (i, 0))
    vm = 2 * (3 * _nbytes((t, n_sel), F32) + _nbytes((t, PEER_NKEYS, PEER_NKEYS), F32))
    return pl.pallas_call(
        _peer_gates_kernel, name="peer_gates",
        out_shape=jax.ShapeDtypeStruct((n, PEER_NKEYS, PEER_NKEYS), F32),
        grid=(n // t - off,),
        in_specs=[spec, spec, spec],
        out_specs=pl.BlockSpec((t, PEER_NKEYS, PEER_NKEYS), lambda i: (i, 0, 0)),
        compiler_params=_cparams(1, vm),
    )(i1, i2, g)


def _peer_up_kernel(h_ref, u_ref, g_ref, o_ref):
    acc = lax.dot_general(h_ref[...], u_ref[...], (((1,), (1,)), ((), ())), preferred_element_type=F32)
    for jj in range(g_ref.shape[1]):
        a = acc[:, jj * PEER_NKEYS:(jj + 1) * PEER_NKEYS]
        act = 0.5 * a * (1.0 + lax.erf(a * (2.0 ** -0.5)))
        o_ref[:, jj * PEER_NKEYS:(jj + 1) * PEER_NKEYS] = (act * g_ref[:, jj, :]).astype(o_ref.dtype)


def _peer_up(h, u, layer, gmat, *, skip):
    n, d = h.shape
    e = u.shape[1]
    tm = ROW_BLOCK
    ti = SUBLANES
    tn = ti * PEER_NKEYS
    vm = 2 * (_nbytes((tm, d), MXU_DTYPE) + _nbytes((d, tn), MXU_DTYPE) + _nbytes((tm, ti, PEER_NKEYS), F32)
              + _nbytes((tm, tn), MXU_DTYPE)) + 3 * _nbytes((tm, tn), F32)
    return pl.pallas_call(
        _peer_up_kernel, name="peer_up",
        out_shape=jax.ShapeDtypeStruct((n, e), MXU_DTYPE),
        grid=(n // tm - skip, e // tn),
        in_specs=[pl.BlockSpec((tm, d), lambda i, j: (i, 0)),
                  pl.BlockSpec((None, tn, d), lambda i, j: (layer, j, 0)),
                  pl.BlockSpec((tm, ti, PEER_NKEYS), lambda i, j: (i, j, 0))],
        out_specs=pl.BlockSpec((tm, tn), lambda i, j: (i, j)),
        compiler_params=_cparams(2, vm),
    )(h, u, gmat)


def _final_norm_kernel(x_ref, g_ref, o_ref):
    x = x_ref[...]
    o_ref[...] = x * lax.rsqrt(jnp.mean(x * x, axis=-1, keepdims=True) + EPS) * g_ref[...]


def _final_norm(x, gain, *, nl):
    d = x.shape[1]
    tb = 256
    vm = 4 * _nbytes((tb, d), F32) + 2 * _nbytes((tb, d), F32)
    return pl.pallas_call(
        _final_norm_kernel, name="final_norm",
        out_shape=jax.ShapeDtypeStruct((nl, d), F32),
        grid=(nl // tb,),
        in_specs=[pl.BlockSpec((tb, d), lambda i: (i, 0)), pl.BlockSpec((1, d), lambda i: (0, 0))],
        out_specs=pl.BlockSpec((tb, d), lambda i: (i, 0)),
        compiler_params=_cparams(1, vm),
    )(x, gain.reshape(1, d))


def _regroup_kernel(w_ref, o_ref, *, pad_from, pad_to):
    half = MLA_ROPE // 2
    r0 = pl.program_id(1) * o_ref.shape[0]
    in_pad = (r0 >= pad_from) & (r0 < pad_to)

    @pl.when(jnp.logical_not(in_pad))
    def _():
        o_ref[...] = w_ref[0].astype(o_ref.dtype)

    @pl.when(in_pad)
    def _():
        o_ref[...] = jnp.zeros_like(o_ref)

    @pl.when(r0 == pad_from)
    def _():
        o_ref[0:half, :] = w_ref[0, 0:half, :].astype(o_ref.dtype)
        o_ref[LANES // 2:LANES // 2 + half, :] = w_ref[0, half:2 * half, :].astype(o_ref.dtype)


def _regroup_in_proj(w_in_t, *, wide, kv_rank, d):
    n_layer, cols, _ = w_in_t.shape
    src_ckv = 4 * wide
    src_rope = src_ckv + kv_rank
    src_conv = src_rope + MLA_ROPE
    src_gate = src_conv + 3 * wide
    assert src_gate + N_BRANCH * d == cols
    dst_ckv = 7 * wide
    dst_rope = dst_ckv + kv_rank
    dst_gate = 8 * wide
    out_rows = dst_gate + N_BRANCH * d
    tr = 256
    assert dst_rope % tr == 0 and kv_rank % tr == 0 and (N_BRANCH * d) % tr == 0 and tr >= LANES

    def src_row(l, i):
        r = i * tr
        src = jnp.where(r < 4 * wide, r,
                        jnp.where(r < dst_ckv, r - 4 * wide + src_conv,
                                  jnp.where(r < dst_rope, r - dst_ckv + src_ckv,
                                            jnp.where(r < dst_gate, src_rope, r - dst_gate + src_gate))))
        return (l, pl.multiple_of(src, MLA_ROPE), 0)

    vm = 2 * (_nbytes((tr, d), F32) + _nbytes((tr, d), MXU_DTYPE)) + _nbytes((tr, d), F32)
    return pl.pallas_call(
        functools.partial(_regroup_kernel, pad_from=dst_rope, pad_to=dst_gate),
        name="regroup_in_proj",
        out_shape=jax.ShapeDtypeStruct((n_layer, out_rows, d), MXU_DTYPE),
        grid=(n_layer, out_rows // tr),
        in_specs=[pl.BlockSpec((pl.Element(1), pl.Element(tr), pl.Element(d)), src_row)],
        out_specs=pl.BlockSpec((None, tr, d), lambda l, i: (l, i, 0)),
        compiler_params=_cparams(2, vm),
    )(w_in_t)


def _rope_lane_layout(w):
    half = MLA_ROPE // 2
    z = jnp.zeros(w.shape[:-1] + (LANES // 2 - half,), w.dtype)
    return jnp.concatenate([w[..., :half], z, w[..., half:], z], axis=-1)


def _rope_tables(seq, nc):
    pos = jnp.arange(seq, dtype=jnp.int32)
    n_freq = MLA_ROPE // 4
    inv_freq = jnp.power(ROPE_BASE, -jnp.arange(n_freq, dtype=F32) / n_freq)
    row = (pos // GRID_W).astype(F32)
    col = (pos % GRID_W).astype(F32)
    ang = jnp.concatenate([row[:, None] * inv_freq, col[:, None] * inv_freq], axis=-1)
    cos, sin = jnp.cos(ang), jnp.sin(ang)
    cos_l = _rope_lane_layout(jnp.concatenate([cos, cos], axis=-1))
    sin_l = _rope_lane_layout(jnp.concatenate([-sin, sin], axis=-1))
    one = _rope_lane_layout(jnp.ones((nc, MLA_ROPE), F32))
    return one, jnp.zeros((nc, LANES), F32), cos_l, sin_l


def kernel(x, c, ctx, c_ctx, w_ada, b_ada, norm_mix, norm_ffn, w_in, na_rpb, mla_q_norm, mla_kv_norm, mla_w_uq, mla_w_ukv, conv_w, w_branch_a, w_branch_b, w_branch_c, w_out, peer_w_q, peer_sub_keys, peer_u, peer_v, final_norm):
    n_batch, seq, d = x.shape
    lc = ctx.shape[1]
    depth = w_ada.shape[0]
    nc = n_batch * lc
    nl = n_batch * seq
    na_w = NA_HEADS * HEAD_DIM
    q_rank, kv_rank = mla_w_uq.shape[1], mla_w_ukv.shape[1]
    sc_w = conv_w.shape[2]
    rows = seq // GRID_W
    assert nc == ROW_BLOCK and seq % ROW_BLOCK == 0 and d % LANES == 0
    assert rows % NA_Q_ROWS == 0 and rows >= NA_K_ROWS
    assert na_w == q_rank == sc_w == MLA_HEADS * MLA_V and kv_rank * 2 == na_w
    kw = dict(nl=nl, seq=seq, n_batch=n_batch)

    wide = na_w
    col_gate = 8 * wide
    proj_cols = col_gate + N_BRANCH * d

    cmat = jnp.zeros((SUBLANES, d), F32).at[:n_batch].set(c).at[n_batch].set(c_ctx)
    xs = jnp.concatenate([x.reshape(nl, d), ctx.reshape(nc, d)], axis=0)

    na_pairs, na_pair_of, case_of_blk, dc_hot, col_ok = _na_bias_tables(rows)
    one_c, zero_c, cos_l, sin_l = _rope_tables(seq, nc)
    cos_t = jnp.concatenate([cos_l] * n_batch + [one_c], axis=0)
    sin_t = jnp.concatenate([sin_l] * n_batch + [zero_c], axis=0)

    w_proj = _regroup_in_proj(jnp.swapaxes(w_in, 1, 2), wide=wide, kv_rank=kv_rank, d=d)
    b_ada3 = b_ada.reshape(depth, 1, N_MOD * d)
    wa_c, wb_c, wc_c = (w.astype(MXU_DTYPE) for w in (w_branch_a, w_branch_b, w_branch_c))
    w_out_c = w_out.astype(MXU_DTYPE)
    w_q_c = peer_w_q.astype(MXU_DTYPE)
    u_c = peer_u.astype(MXU_DTYPE)
    v_c = peer_v.astype(MXU_DTYPE)

    for l in range(depth):
        last = l == depth - 1
        skip = 1 if last else 0
        wq = mla_w_uq[l].reshape(q_rank, MLA_HEADS, MLA_NOPE + MLA_ROPE)
        wq = jnp.concatenate([wq[..., :MLA_NOPE], _rope_lane_layout(wq[..., MLA_NOPE:])], axis=-1)
        wq = wq.reshape(q_rank, MLA_HEADS * MLA_HEAD_PAD).astype(MXU_DTYPE)
        wkv = mla_w_ukv[l].reshape(kv_rank, MLA_HEADS, MLA_NOPE + MLA_V)
        wk = wkv[..., :MLA_NOPE].reshape(kv_rank, MLA_HEADS * MLA_NOPE).astype(MXU_DTYPE)
        wv = wkv[..., MLA_NOPE:].reshape(kv_rank, MLA_HEADS * MLA_V).astype(MXU_DTYPE)
        pair_table = _na_pair_table(na_rpb[l], na_pairs, dc_hot, col_ok)

        mod = _ada_table(cmat, w_ada, b_ada3, l)
        modtab = mod[:n_batch + 1].reshape((n_batch + 1) * N_MOD, 1, d)

        h = _normmod(xs, norm_mix[l], modtab, 0, skip=0, **kw)
        proj = _matmul(h, w_proj, l, tn=_tile(proj_cols, (2048, 1024, 512, 256)), out_dtype=MXU_DTYPE, w_rows=True,
                       name="in_proj")
        hd = wide // HEAD_DIM
        ya = _na_attention(proj, pair_table, jnp.asarray(case_of_blk), jnp.asarray(na_pair_of), lc=lc,
                           col_q=0, col_k=hd, col_v=2 * hd, **kw)
        qf, kf, vf = _mla_up(proj, mla_q_norm[l], mla_kv_norm[l], wq, wk, wv, cos_t, sin_t,
                             col_cq=3, col_ckv=7 * wide // kv_rank, col_kr=(7 * wide + kv_rank) // LANES)
        yb = _mla_attention(qf, kf, vf, lc=lc, **kw)
        if not last:
            ya = _ctx_attention(proj, proj, proj, ya, nl=nl, lc=lc, n_batch=n_batch, n_heads=NA_HEADS, dk=HEAD_DIM,
                                dv=HEAD_DIM, col_q=0, col_k=hd, col_v=2 * hd, scale=HEAD_DIM ** -0.5)
            yb = _ctx_attention(qf, kf, vf, yb, nl=nl, lc=lc, n_batch=n_batch, n_heads=MLA_HEADS, dk=MLA_HEAD_PAD,
                                dv=MLA_V, col_q=0, col_k=0, col_v=0, scale=(MLA_NOPE + MLA_ROPE) ** -0.5)
        tn_m = 1024 if d % 1024 == 0 else d
        m = _merge(ya, yb, proj, conv_w, wa_c, wb_c, wc_c, l, d=d, nl=nl, seq=seq, lc=lc, skip=skip,
                   col_gb=4, col_gc=5, col_xin=6, col_gate=col_gate // tn_m)
        xs = _matmul_resid(m, w_out_c, l, xs, modtab, 2, tn=tn_m, tk=d, skip=skip, name="out_proj", **kw)

        h2 = _normmod(xs, norm_ffn[l], modtab, 3, skip=skip, **kw)
        pq = _matmul(h2, w_q_c, l, tn=2048, out_dtype=MXU_DTYPE, skip=skip, name="peer_query")
        keys = peer_sub_keys[l].reshape(PEER_HEADS, 2 * PEER_NKEYS, PEER_KEY_DIM // 2).astype(MXU_DTYPE)
        gates_t, i1_t, i2_t = _peer_topk(pq, keys, skip=skip)
        gmat = _peer_gates(i1_t.T, i2_t.T, gates_t.T, skip=skip)
        hexp = _peer_up(h2, u_c, l, gmat, skip=skip)
        xs = _matmul_resid(hexp, v_c, l, xs, modtab, 5, tn=min(d, 2048), tk=2048, skip=skip, name="peer_down",
                           **kw)

    out = _final_norm(xs, final_norm, nl=nl)
    return out.reshape(n_batch, seq, d)
```

```python
import functools
import math

import numpy as np
import jax
import jax.numpy as jnp
from jax import lax
from jax.experimental import pallas as pl
from jax.experimental.pallas import tpu as pltpu

F32 = jnp.float32
MXU_DTYPE = jnp.bfloat16

GRID_W = 64
EPS = 1e-6
N_MOD = 6
NEG_INF = -1e30
HEAD_DIM = 128
NA_HEADS = 8
NA_WIN_H = 8
NA_WIN_W = 16
MLA_HEADS = 8
MLA_NOPE = 128
MLA_ROPE = 64
MLA_V = 128
ROPE_BASE = 10000.0
SC_TAPS = 3
N_BRANCH = 3
PEER_HEADS = 8
PEER_NKEYS = 128
PEER_KEY_DIM = 256
PEER_TOPK = 16

V7X_VMEM_BYTES = 64 * 1024 * 1024
LANES = 128
SUBLANES = 8

ROW_BLOCK = 512
MLA_HEAD_PAD = 256
NA_Q_ROWS = 8
NA_K_ROWS = 16
NA_K_CHUNK_ROWS = 4
NA_HEADS_PER_STEP = 4


def _cparams(n_axes, vmem_bytes):
    limit = min(int(vmem_bytes) + (4 << 20), V7X_VMEM_BYTES - (6 << 20))
    return pltpu.CompilerParams(dimension_semantics=("arbitrary",) * n_axes,
                                vmem_limit_bytes=limit)


def _nbytes(shape, dtype):
    return int(np.prod(shape)) * jnp.dtype(dtype).itemsize


def _tile(n, prefs):
    return next(t for t in prefs if n % t == 0)


def _mod_row(row0, nl, seq, n_batch):
    return jnp.where(row0 >= nl, n_batch, row0 // seq)


def _ada_kernel(c_ref, w_ref, b_ref, o_ref):
    c = c_ref[...]
    s = (c * jax.nn.sigmoid(c)).astype(MXU_DTYPE)
    acc = jnp.dot(s, w_ref[...].astype(MXU_DTYPE), preferred_element_type=F32)
    o_ref[...] = acc + b_ref[...]


def _ada_table(cmat, w_ada, b_ada, layer):
    rows, d = cmat.shape
    n = w_ada.shape[2]
    tn = 512
    vm = 2 * (_nbytes((d, tn), F32) + _nbytes((rows, tn), F32) * 2) + _nbytes((rows, d), F32) * 2 \
        + _nbytes((d, tn), MXU_DTYPE)
    return pl.pallas_call(
        _ada_kernel, name="ada_table",
        out_shape=jax.ShapeDtypeStruct((rows, n), F32),
        grid=(n // tn,),
        in_specs=[pl.BlockSpec((rows, d), lambda j: (0, 0)),
                  pl.BlockSpec((None, d, tn), lambda j: (layer, 0, j)),
                  pl.BlockSpec((None, 1, tn), lambda j: (layer, 0, j))],
        out_specs=pl.BlockSpec((rows, tn), lambda j: (0, j)),
        compiler_params=_cparams(1, vm),
    )(cmat, w_ada, b_ada)


def _normmod_kernel(x_ref, g_ref, sh_ref, sc_ref, o_ref):
    x = x_ref[...]
    y = x * lax.rsqrt(jnp.mean(x * x, axis=-1, keepdims=True) + EPS)
    o_ref[...] = ((y * g_ref[...]) * (1.0 + sc_ref[0]) + sh_ref[0]).astype(o_ref.dtype)


def _normmod(x, gain, modtab, m_shift, *, nl, seq, n_batch, skip):
    n, d = x.shape
    tb = 256
    off = skip * (ROW_BLOCK // tb)

    def mrow(i, m):
        return (_mod_row(i * tb, nl, seq, n_batch) * N_MOD + m, 0, 0)

    vm = 2 * (_nbytes((tb, d), F32) + _nbytes((tb, d), MXU_DTYPE)) + 3 * _nbytes((tb, d), F32)
    return pl.pallas_call(
        _normmod_kernel, name="normmod",
        out_shape=jax.ShapeDtypeStruct((n, d), MXU_DTYPE),
        grid=(n // tb - off,),
        in_specs=[pl.BlockSpec((tb, d), lambda i: (i, 0)),
                  pl.BlockSpec((1, d), lambda i: (0, 0)),
                  pl.BlockSpec((1, 1, d), lambda i: mrow(i, m_shift)),
                  pl.BlockSpec((1, 1, d), lambda i: mrow(i, m_shift + 1))],
        out_specs=pl.BlockSpec((tb, d), lambda i: (i, 0)),
        compiler_params=_cparams(1, vm),
    )(x, gain.reshape(1, d), modtab, modtab)


def _mm_kernel(x_ref, w_ref, o_ref):
    o_ref[...] = jnp.dot(x_ref[...], w_ref[...], preferred_element_type=F32).astype(o_ref.dtype)


def _mm_nt_kernel(x_ref, w_ref, o_ref):
    o_ref[...] = lax.dot_general(x_ref[...], w_ref[...], (((1,), (1,)), ((), ())),
                                 preferred_element_type=F32).astype(o_ref.dtype)


def _matmul(x, w, layer, *, tn, out_dtype, skip=0, w_rows=False, name="matmul"):
    n, k = x.shape
    nn = w.shape[1] if w_rows else w.shape[2]
    tm = ROW_BLOCK
    vm = 2 * (_nbytes((tm, k), x.dtype) + _nbytes((k, tn), w.dtype) + _nbytes((tm, tn), out_dtype)) \
        + _nbytes((tm, tn), F32)
    w_spec = (pl.BlockSpec((None, tn, k), lambda j, i: (layer, j, 0)) if w_rows
              else pl.BlockSpec((None, k, tn), lambda j, i: (layer, 0, j)))
    return pl.pallas_call(
        _mm_nt_kernel if w_rows else _mm_kernel, name=name,
        out_shape=jax.ShapeDtypeStruct((n, nn), out_dtype),
        grid=(nn // tn, n // tm - skip),
        in_specs=[pl.BlockSpec((tm, k), lambda j, i: (i, 0)), w_spec],
        out_specs=pl.BlockSpec((tm, tn), lambda j, i: (i, j)),
        compiler_params=_cparams(2, vm),
    )(x, w)


def _mm_resid_kernel(x_ref, w_ref, r_ref, g_ref, o_ref):
    @pl.when(pl.program_id(2) == 0)
    def _():
        o_ref[...] = r_ref[...]

    o_ref[...] += g_ref[0] * jnp.dot(x_ref[...], w_ref[...], preferred_element_type=F32)


def _matmul_resid(x, w, layer, resid, modtab, m_gate, *, tn, tk, nl, seq, n_batch, skip, name):
    n, kk = x.shape
    nn = w.shape[2]
    tm = ROW_BLOCK

    def mrow(i, j, k):
        return (_mod_row(i * tm, nl, seq, n_batch) * N_MOD + m_gate, 0, j)

    vm = 2 * (_nbytes((tm, tk), x.dtype) + _nbytes((tk, tn), w.dtype) + 2 * _nbytes((tm, tn), F32)) \
        + 2 * _nbytes((tm, tn), F32)
    return pl.pallas_call(
        _mm_resid_kernel, name=name,
        out_shape=jax.ShapeDtypeStruct((n, nn), F32),
        grid=(n // tm - skip, nn // tn, kk // tk),
        in_specs=[pl.BlockSpec((tm, tk), lambda i, j, k: (i, k)),
                  pl.BlockSpec((None, tk, tn), lambda i, j, k: (layer, k, j)),
                  pl.BlockSpec((tm, tn), lambda i, j, k: (i, j)),
                  pl.BlockSpec((1, 1, tn), mrow)],
        out_specs=pl.BlockSpec((tm, tn), lambda i, j, k: (i, j)),
        compiler_params=_cparams(3, vm),
    )(x, w, resid, modtab)


def _na_bias_tables(rows):
    assert 2 * GRID_W == LANES and NA_K_ROWS % 2 == 0
    n_dr = 2 * NA_WIN_H - 1
    n_blk = rows // NA_Q_ROWS
    rl = np.arange(NA_Q_ROWS)[:, None]
    krl = np.arange(NA_K_ROWS)[None, :]
    cases, case_of_blk = [], []
    for i in range(n_blk):
        r0 = i * NA_Q_ROWS
        k0 = int(np.clip(r0 - NA_K_CHUNK_ROWS, 0, rows - NA_K_ROWS))
        r = r0 + rl
        kr = k0 + krl
        start = np.clip(r - NA_WIN_H // 2, 0, rows - NA_WIN_H)
        ok = (kr >= start) & (kr < start + NA_WIN_H)
        dr = np.where(ok, kr - r + NA_WIN_H - 1, n_dr).astype(np.int32)
        assert dr.min() >= 0
        for ci, known in enumerate(cases):
            if np.array_equal(known, dr):
                case_of_blk.append(ci)
                break
        else:
            case_of_blk.append(len(cases))
            cases.append(dr)
    c = np.arange(GRID_W)[:, None]
    kc = np.arange(GRID_W)[None, :]
    ws = np.clip(c - NA_WIN_W // 2, 0, GRID_W - NA_WIN_W)
    col_ok = (kc >= ws) & (kc < ws + NA_WIN_W)
    dc = np.clip(kc - c + NA_WIN_W - 1, 0, 2 * NA_WIN_W - 2)
    dc_hot = (dc[None] == np.arange(2 * NA_WIN_W - 1)[:, None, None]).astype(np.float32)
    dr_of = np.stack(cases).reshape(len(cases), NA_Q_ROWS, NA_K_ROWS // 2, 2)
    pairs, pair_of = np.unique(dr_of.reshape(-1, 2), axis=0, return_inverse=True)
    return (pairs.astype(np.int32), pair_of.reshape(-1).astype(np.int32), np.asarray(case_of_blk, np.int32),
            dc_hot, col_ok)


def _na_pair_table(rpb, pairs, dc_hot, col_ok):
    n_h = rpb.shape[0]
    tb = jnp.einsum('hdk,kcq->hdcq', rpb.astype(F32), dc_hot, precision=lax.Precision.HIGHEST)
    tb = jnp.where(col_ok, tb, NEG_INF)
    tb = jnp.concatenate([tb, jnp.full((n_h, 1, GRID_W, GRID_W), NEG_INF, F32)], axis=1)
    return jnp.concatenate([tb[:, pairs[:, 0]], tb[:, pairs[:, 1]]], axis=-1)


def _na_kernel(case_ref, pair_ref, q_ref, k0_ref, k1_ref, k2_ref, k3_ref, v0_ref, v1_ref, v2_ref, v3_ref,
               kc_ref, vc_ref, b_ref, o_ref):
    scale = HEAD_DIM ** -0.5
    nt = (((1,), (1,)), ((), ()))
    n_pair = NA_K_ROWS // 2
    base = case_ref[pl.program_id(2)] * (NA_Q_ROWS * n_pair)
    for hh in range(NA_HEADS_PER_STEP):
        cs = slice(hh * HEAD_DIM, (hh + 1) * HEAD_DIM)
        q = q_ref[:, cs]
        k_loc = jnp.concatenate([k0_ref[:, cs], k1_ref[:, cs], k2_ref[:, cs], k3_ref[:, cs]], axis=0)
        v_loc = jnp.concatenate([v0_ref[:, cs], v1_ref[:, cs], v2_ref[:, cs], v3_ref[:, cs]], axis=0)
        s_loc = lax.dot_general(q, k_loc, nt, preferred_element_type=F32) * scale
        biased = []
        for rl in range(NA_Q_ROWS):
            blocks = [b_ref[hh, pair_ref[base + rl * n_pair + p]] for p in range(n_pair)]
            biased.append(s_loc[rl * GRID_W:(rl + 1) * GRID_W, :] + jnp.concatenate(blocks, axis=1))
        s_loc = jnp.concatenate(biased, axis=0)
        s_ctx = lax.dot_general(q, kc_ref[:, cs], nt, preferred_element_type=F32) * scale
        m = jnp.maximum(jnp.max(s_loc, axis=-1, keepdims=True), jnp.max(s_ctx, axis=-1, keepdims=True))
        p_loc = jnp.exp(s_loc - m)
        p_ctx = jnp.exp(s_ctx - m)
        denom = jnp.sum(p_loc, axis=-1, keepdims=True) + jnp.sum(p_ctx, axis=-1, keepdims=True)
        o = (jnp.dot(p_loc.astype(MXU_DTYPE), v_loc, preferred_element_type=F32)
             + jnp.dot(p_ctx.astype(MXU_DTYPE), vc_ref[:, cs], preferred_element_type=F32))
        o_ref[:, cs] = (o / denom).astype(o_ref.dtype)


def _na_attention(proj, pair_table, case_of_blk, pair_of, *, nl, seq, lc, n_batch, col_q, col_k, col_v):
    n = proj.shape[0]
    rows = seq // GRID_W
    tq = NA_Q_ROWS * GRID_W
    tkc = NA_K_CHUNK_ROWS * GRID_W
    n_blk = rows // NA_Q_ROWS
    n_chunk = NA_K_ROWS // NA_K_CHUNK_ROWS
    max_c0 = rows // NA_K_CHUNK_ROWS - n_chunk
    hw = NA_HEADS_PER_STEP * HEAD_DIM

    def qmap(h, b, i, case, pair):
        return (b * (seq // tq) + i, col_q + h)

    def kvmap(col, j):
        def f(h, b, i, case, pair):
            c0 = jnp.clip(i * (NA_Q_ROWS // NA_K_CHUNK_ROWS) - 1, 0, max_c0)
            return (b * (seq // tkc) + c0 + j, col + h)
        return f

    def ctxmap(col):
        return lambda h, b, i, case, pair: (nl // lc + b, col + h)

    in_specs = [pl.BlockSpec((tq, hw), qmap)]
    in_specs += [pl.BlockSpec((tkc, hw), kvmap(col_k, j)) for j in range(n_chunk)]
    in_specs += [pl.BlockSpec((tkc, hw), kvmap(col_v, j)) for j in range(n_chunk)]
    in_specs += [pl.BlockSpec((lc, hw), ctxmap(col_k)), pl.BlockSpec((lc, hw), ctxmap(col_v))]
    in_specs += [pl.BlockSpec((NA_HEADS_PER_STEP,) + pair_table.shape[1:],
                              lambda h, b, i, case, pair: (h, 0, 0, 0))]
    vm = 2 * (NA_HEADS_PER_STEP * _nbytes(pair_table.shape[1:], F32) + 12 * _nbytes((tq, hw), MXU_DTYPE)) \
        + 7 * NA_HEADS_PER_STEP * _nbytes((tq, NA_K_ROWS * GRID_W + lc), F32)
    grid_spec = pltpu.PrefetchScalarGridSpec(
        num_scalar_prefetch=2, grid=(NA_HEADS // NA_HEADS_PER_STEP, n_batch, n_blk),
        in_specs=in_specs,
        out_specs=pl.BlockSpec((tq, hw), lambda h, b, i, case, pair: (b * (seq // tq) + i, h)))
    return pl.pallas_call(
        _na_kernel, name="na_attention",
        out_shape=jax.ShapeDtypeStruct((n, NA_HEADS * HEAD_DIM), MXU_DTYPE),
        grid_spec=grid_spec,
        compiler_params=_cparams(3, vm),
    )(case_of_blk, pair_of, *([proj] * (1 + 2 * n_chunk + 2)), pair_table)


def _ctx_attn_kernel(q_ref, k_ref, v_ref, prev_ref, o_ref, *, scale):
    del prev_ref
    s = lax.dot_general(q_ref[...], k_ref[...], (((1,), (1,)), ((), ())),
                        preferred_element_type=F32) * scale
    m = jnp.max(s, axis=-1, keepdims=True)
    p = jnp.exp(s - m)
    o = jnp.dot(p.astype(MXU_DTYPE), v_ref[...], preferred_element_type=F32)
    o_ref[...] = (o / jnp.sum(p, axis=-1, keepdims=True)).astype(o_ref.dtype)


def _ctx_attention(qa, ka, va, y, *, nl, lc, n_batch, n_heads, dk, dv, col_q, col_k, col_v, scale):
    vm = 2 * (2 * _nbytes((lc, dk), MXU_DTYPE) + 2 * _nbytes((lc, dv), MXU_DTYPE)) + 4 * _nbytes((lc, lc), F32)
    return pl.pallas_call(
        functools.partial(_ctx_attn_kernel, scale=scale), name="ctx_attention",
        out_shape=jax.ShapeDtypeStruct(y.shape, y.dtype),
        grid=(n_batch, n_heads),
        in_specs=[pl.BlockSpec((lc, dk), lambda b, h: (nl // lc + b, col_q + h)),
                  pl.BlockSpec((lc, dk), lambda b, h: (nl // lc + b, col_k + h)),
                  pl.BlockSpec((lc, dv), lambda b, h: (nl // lc + b, col_v + h)),
                  pl.BlockSpec(memory_space=pl.ANY)],
        out_specs=pl.BlockSpec((lc, dv), lambda b, h: (nl // lc + b, h)),
        input_output_aliases={3: 0},
        compiler_params=_cparams(2, vm),
    )(qa, ka, va, y)


def _mla_up_kernel(cq_ref, ckv_ref, kr_ref, qn_ref, kvn_ref, wq_ref, wk_ref, wv_ref, cos_ref, sin_ref,
                   q_ref, k_ref, v_ref):
    def rms(x, g):
        return (x * lax.rsqrt(jnp.mean(x * x, axis=-1, keepdims=True) + EPS) * g).astype(MXU_DTYPE)

    cos = cos_ref[...]
    sin = sin_ref[...]

    def rope(x):
        return x * cos + pltpu.roll(x, 2 * (MLA_ROPE // 2), axis=1) * sin

    cqn = rms(cq_ref[...].astype(F32), qn_ref[...])
    kvn = rms(ckv_ref[...].astype(F32), kvn_ref[...])
    q = jnp.dot(cqn, wq_ref[...], preferred_element_type=F32)
    kn = jnp.dot(kvn, wk_ref[...], preferred_element_type=F32)
    v_ref[...] = jnp.dot(kvn, wv_ref[...], preferred_element_type=F32).astype(v_ref.dtype)
    kr = rope(kr_ref[...].astype(F32)).astype(k_ref.dtype)
    for h in range(MLA_HEADS):
        c0 = h * MLA_HEAD_PAD
        q_ref[:, c0:c0 + MLA_NOPE] = q[:, c0:c0 + MLA_NOPE].astype(q_ref.dtype)
        q_ref[:, c0 + MLA_NOPE:c0 + MLA_HEAD_PAD] = rope(q[:, c0 + MLA_NOPE:c0 + MLA_HEAD_PAD]).astype(q_ref.dtype)
        k_ref[:, c0:c0 + MLA_NOPE] = kn[:, h * MLA_NOPE:(h + 1) * MLA_NOPE].astype(k_ref.dtype)
        k_ref[:, c0 + MLA_NOPE:c0 + MLA_HEAD_PAD] = kr


def _mla_up(proj, q_norm, kv_norm, wq, wk, wv, cos_t, sin_t, *, col_cq, col_ckv, col_kr):
    n = proj.shape[0]
    tm = 256
    rq, rkv = wq.shape[0], wk.shape[0]
    wq_cols = wq.shape[1]
    vm = 2 * (_nbytes(wq.shape, MXU_DTYPE) + _nbytes(wk.shape, MXU_DTYPE) + _nbytes(wv.shape, MXU_DTYPE)
              + _nbytes((tm, rq + rkv + LANES), MXU_DTYPE) + 2 * _nbytes((tm, LANES), F32)
              + _nbytes((tm, 2 * wq_cols + wv.shape[1]), MXU_DTYPE)) + 3 * _nbytes((tm, wq_cols), F32)
    return pl.pallas_call(
        _mla_up_kernel, name="mla_up",
        out_shape=(jax.ShapeDtypeStruct((n, wq_cols), MXU_DTYPE),
                   jax.ShapeDtypeStruct((n, wq_cols), MXU_DTYPE),
                   jax.ShapeDtypeStruct((n, wv.shape[1]), MXU_DTYPE)),
        grid=(n // tm,),
        in_specs=[pl.BlockSpec((tm, rq), lambda i: (i, col_cq)),
                  pl.BlockSpec((tm, rkv), lambda i: (i, col_ckv)),
                  pl.BlockSpec((tm, LANES), lambda i: (i, col_kr)),
                  pl.BlockSpec((1, rq), lambda i: (0, 0)),
                  pl.BlockSpec((1, rkv), lambda i: (0, 0)),
                  pl.BlockSpec(wq.shape, lambda i: (0, 0)),
                  pl.BlockSpec(wk.shape, lambda i: (0, 0)),
                  pl.BlockSpec(wv.shape, lambda i: (0, 0)),
                  pl.BlockSpec((tm, LANES), lambda i: (i, 0)),
                  pl.BlockSpec((tm, LANES), lambda i: (i, 0))],
        out_specs=(pl.BlockSpec((tm, wq_cols), lambda i: (i, 0)),
                   pl.BlockSpec((tm, wq_cols), lambda i: (i, 0)),
                   pl.BlockSpec((tm, wv.shape[1]), lambda i: (i, 0))),
        compiler_params=_cparams(1, vm),
    )(proj, proj, proj, q_norm.reshape(1, rq), kv_norm.reshape(1, rkv), wq, wk, wv, cos_t, sin_t)


def _flash_kernel(q_ref, kc_ref, kl_ref, vc_ref, vl_ref, o_ref, m_ref, l_ref, acc_ref, *, scale, tk):
    nt = (((1,), (1,)), ((), ()))
    q = q_ref[...]
    c = scale * math.log2(math.e)

    def chunk(k, v):
        s = lax.dot_general(q, k, nt, preferred_element_type=F32)
        m_old = m_ref[...]
        m_new = jnp.maximum(m_old, jnp.max(s, axis=-1, keepdims=True) * c)
        alpha = jnp.exp2(m_old - m_new)
        p = jnp.exp2(s * c - jnp.tile(m_new, (1, s.shape[1] // LANES)))
        l_ref[...] = alpha * l_ref[...] + jnp.sum(p, axis=-1, keepdims=True)
        acc_ref[...] = alpha * acc_ref[...] + jnp.dot(p.astype(MXU_DTYPE), v, preferred_element_type=F32)
        m_ref[...] = m_new

    m_ref[...] = jnp.full_like(m_ref, -jnp.inf)
    l_ref[...] = jnp.zeros_like(l_ref)
    acc_ref[...] = jnp.zeros_like(acc_ref)
    chunk(kc_ref[...], vc_ref[...])

    def body(j, carry):
        r0 = pl.multiple_of(j * tk, tk)
        chunk(kl_ref[pl.ds(r0, tk), :], vl_ref[pl.ds(r0, tk), :])
        return carry

    lax.fori_loop(0, kl_ref.shape[0] // tk, body, 0)
    o_ref[...] = (acc_ref[...] / l_ref[...]).astype(o_ref.dtype)


def _mla_attention(qf, kf, vf, *, nl, seq, lc, n_batch):
    n = qf.shape[0]
    tq, tk = _tile(seq, (2048, 1024, 512)), 512
    dk, dv = MLA_HEAD_PAD, MLA_V
    assert dv == LANES
    scale = (MLA_NOPE + MLA_ROPE) ** -0.5
    vm = 2 * (_nbytes((tq, dk), MXU_DTYPE) + _nbytes((lc + seq, dk + dv), MXU_DTYPE) + _nbytes((tq, dv), MXU_DTYPE)) \
        + 6 * _nbytes((tq, tk), F32) + 2 * _nbytes((tq, dv), F32)
    return pl.pallas_call(
        functools.partial(_flash_kernel, scale=scale, tk=tk), name="mla_attention",
        out_shape=jax.ShapeDtypeStruct((n, MLA_HEADS * dv), MXU_DTYPE),
        grid=(n_batch, MLA_HEADS, seq // tq),
        in_specs=[pl.BlockSpec((tq, dk), lambda b, h, i: (b * (seq // tq) + i, h)),
                  pl.BlockSpec((lc, dk), lambda b, h, i: (nl // lc + b, h)),
                  pl.BlockSpec((seq, dk), lambda b, h, i: (b, h)),
                  pl.BlockSpec((lc, dv), lambda b, h, i: (nl // lc + b, h)),
                  pl.BlockSpec((seq, dv), lambda b, h, i: (b, h))],
        out_specs=pl.BlockSpec((tq, dv), lambda b, h, i: (b * (seq // tq) + i, h)),
        scratch_shapes=[pltpu.VMEM((tq, LANES), F32), pltpu.VMEM((tq, LANES), F32), pltpu.VMEM((tq, dv), F32)],
        compiler_params=_cparams(3, vm),
    )(qf, kf, kf, vf, vf)


def _merge_kernel(ya_ref, yb_ref, gb_ref, gc_ref, xin_ref, gcp_ref, xinp_ref, gcn_ref, xinn_ref, cw_ref,
                  wa_ref, wb_ref, wc_ref, g0_ref, g1_ref, g2_ref, o_ref, yc_ref, *, nl, seq, lc):
    i = pl.program_id(0)
    j = pl.program_id(1)
    tm = ya_ref.shape[0]

    @pl.when(j == 0)
    def _():
        u = gc_ref[...].astype(F32) * xin_ref[...].astype(F32)
        u_prev_halo = (gcp_ref[...].astype(F32) * xinp_ref[...].astype(F32))[SUBLANES - 1:SUBLANES, :]
        u_next_halo = (gcn_ref[...].astype(F32) * xinn_ref[...].astype(F32))[0:1, :]
        row = lax.broadcasted_iota(jnp.int32, (tm, 1), 0)
        g = i * tm + row
        pos = jnp.where(g < nl, g % seq, (g - nl) % lc)
        seq_len = jnp.where(g < nl, seq, lc)
        u_dn = jnp.where(row == 0, u_prev_halo, pltpu.roll(u, 1, axis=0))
        u_up = jnp.where(row == tm - 1, u_next_halo, pltpu.roll(u, tm - 1, axis=0))
        u_dn = jnp.where(pos == 0, 0.0, u_dn)
        u_up = jnp.where(pos == seq_len - 1, 0.0, u_up)
        cw = cw_ref[...]
        y = cw[0:1, :] * u_dn + cw[1:2, :] * u + cw[2:3, :] * u_up
        yc_ref[...] = (gb_ref[...].astype(F32) * y).astype(yc_ref.dtype)

    def branch(y, w_ref, g_ref):
        gate = jax.nn.sigmoid(g_ref[...].astype(F32))
        return gate * jnp.dot(y, w_ref[...], preferred_element_type=F32)

    m = (branch(ya_ref[...], wa_ref, g0_ref) + branch(yb_ref[...], wb_ref, g1_ref)
         + branch(yc_ref[...], wc_ref, g2_ref))
    o_ref[...] = m.astype(o_ref.dtype)


def _merge(ya, yb, proj, conv_w, wa, wb, wc, layer, *, d, nl, seq, lc, skip, col_gb, col_gc, col_xin, col_gate):
    n = ya.shape[0]
    tm, tn = ROW_BLOCK, 1024 if d % 1024 == 0 else d
    w = ya.shape[1]
    hb = tm // SUBLANES
    last_halo = n // SUBLANES - 1
    n_col = d // tn

    def row(i, j):
        return (i, 0)

    def cur(col):
        return lambda i, j: (i, col)

    def prev(col):
        return lambda i, j: (jnp.maximum(i * hb - 1, 0), col)

    def nxt(col):
        return lambda i, j: (jnp.minimum((i + 1) * hb, last_halo), col)

    def gate(br):
        return lambda i, j: (i, col_gate + br * n_col + j)

    in_specs = [pl.BlockSpec((tm, w), row), pl.BlockSpec((tm, w), row),
                pl.BlockSpec((tm, w), cur(col_gb)), pl.BlockSpec((tm, w), cur(col_gc)),
                pl.BlockSpec((tm, w), cur(col_xin)),
                pl.BlockSpec((SUBLANES, w), prev(col_gc)), pl.BlockSpec((SUBLANES, w), prev(col_xin)),
                pl.BlockSpec((SUBLANES, w), nxt(col_gc)), pl.BlockSpec((SUBLANES, w), nxt(col_xin)),
                pl.BlockSpec((None, SC_TAPS, w), lambda i, j: (layer, 0, 0)),
                pl.BlockSpec((None, w, tn), lambda i, j: (layer, 0, j)),
                pl.BlockSpec((None, w, tn), lambda i, j: (layer, 0, j)),
                pl.BlockSpec((None, w, tn), lambda i, j: (layer, 0, j)),
                pl.BlockSpec((tm, tn), gate(0)), pl.BlockSpec((tm, tn), gate(1)), pl.BlockSpec((tm, tn), gate(2))]
    vm = 2 * (5 * _nbytes((tm, w), MXU_DTYPE) + 3 * _nbytes((w, tn), MXU_DTYPE) + 4 * _nbytes((tm, tn), MXU_DTYPE)) \
        + _nbytes((tm, w), MXU_DTYPE) + 6 * _nbytes((tm, max(w, tn)), F32)
    return pl.pallas_call(
        functools.partial(_merge_kernel, nl=nl, seq=seq, lc=lc), name="conv_merge",
        out_shape=jax.ShapeDtypeStruct((n, d), MXU_DTYPE),
        grid=(n // tm - skip, n_col),
        in_specs=in_specs,
        out_specs=pl.BlockSpec((tm, tn), lambda i, j: (i, j)),
        scratch_shapes=[pltpu.VMEM((tm, w), MXU_DTYPE)],
        compiler_params=_cparams(2, vm),
    )(ya, yb, proj, proj, proj, proj, proj, proj, proj, conv_w, wa, wb, wc, proj, proj, proj)


def _sort_network(n):
    size = 1
    while size < n:
        size *= 2
    net = []

    def merge(lo, hi, r):
        step = r * 2
        if step < hi - lo:
            merge(lo, hi, step)
            merge(lo + r, hi, step)
            net.extend((i, i + r) for i in range(lo + r, hi - r, step))
        else:
            net.append((lo, lo + r))

    def sort(lo, hi):
        if hi - lo >= 1:
            mid = lo + (hi - lo) // 2
            sort(lo, mid)
            sort(mid + 1, hi)
            merge(lo, hi, 1)

    sort(0, size - 1)
    return [(i, j) for i, j in net if j < n]


def _topk_cols(s, k, ids=None):
    n_tile = s.shape[0] // SUBLANES
    v = [s[t * SUBLANES:(t + 1) * SUBLANES, :] for t in range(n_tile)]
    if ids is None:
        sub = lax.broadcasted_iota(jnp.int32, (SUBLANES, s.shape[1]), 0)
        r = [sub + t * SUBLANES for t in range(n_tile)]
    else:
        r = [ids[t * SUBLANES:(t + 1) * SUBLANES, :] for t in range(n_tile)]
    for a, b in _sort_network(n_tile):
        first = (v[a] > v[b]) | ((v[a] == v[b]) & (r[a] < r[b]))
        v[a], v[b] = jnp.where(first, v[a], v[b]), jnp.where(first, v[b], v[a])
        r[a], r[b] = jnp.where(first, r[a], r[b]), jnp.where(first, r[b], r[a])
    depth = min(n_tile, k)
    v, r = v[:depth], r[:depth]
    big = jnp.iinfo(jnp.int32).max
    vals, idxs = [], []
    for it in range(k):
        m = jnp.max(v[0], axis=0, keepdims=True)
        idx = jnp.min(jnp.where(v[0] == m, r[0], big), axis=0, keepdims=True)
        vals.append(m)
        idxs.append(idx)
        win = r[0] == idx
        live = min(depth, k - it)
        for t in range(live - 1):
            v[t] = jnp.where(win, v[t + 1], v[t])
            r[t] = jnp.where(win, r[t + 1], r[t])
        v[live - 1] = jnp.where(win, -jnp.inf, v[live - 1])
    return jnp.concatenate(vals, axis=0), jnp.concatenate(idxs, axis=0)


def _candidate_rows():
    k = PEER_TOPK
    groups = []
    for b0 in range(0, k, SUBLANES):
        groups.append((0, 1, b0, b0 + SUBLANES))
    a = 1
    while a < k and k // (a + 1) > 1:
        assert k // (a + 1) <= SUBLANES
        groups.append((a, a + 1, 0, SUBLANES))
        a += 1
    while a < k:
        groups.append((a, a + SUBLANES, 0, 1))
        a += SUBLANES
    covered = {(aa, bb) for a0, a1, b0, b1 in groups for aa in range(a0, a1) for bb in range(b0, b1)}
    assert all((aa, bb) in covered for aa in range(k) for bb in range(k) if (aa + 1) * (bb + 1) <= k)
    return groups


def _take_rows(table, idx):
    iota = lax.broadcasted_iota(jnp.int32, table.shape, 0)
    rows = [jnp.sum(jnp.where(iota == idx[j:j + 1, :], table, 0), axis=0, keepdims=True)
            for j in range(idx.shape[0])]
    return jnp.concatenate(rows, axis=0)


def _peer_topk_kernel(q_ref, keys_ref, g_ref, i1_ref, i2_ref):
    nt = (((1,), (1,)), ((), ()))
    half = PEER_KEY_DIM // 2
    q = q_ref[...]
    s1 = lax.dot_general(keys_ref[0, :PEER_NKEYS, :], q[:, :half], nt, preferred_element_type=F32)
    s2 = lax.dot_general(keys_ref[0, PEER_NKEYS:, :], q[:, half:], nt, preferred_element_type=F32)
    v1, j1 = _topk_cols(s1, PEER_TOPK)
    v2, j2 = _topk_cols(s2, PEER_TOPK)
    sub = lax.broadcasted_iota(jnp.int32, (SUBLANES, q.shape[0]), 0)
    cand, flat = [], []
    for a0, a1, b0, b1 in _candidate_rows():
        cand.append(v1[a0:a1, :] + v2[b0:b1, :])
        flat.append(sub * (1 if a1 - a0 == 1 else PEER_TOPK) + (a0 * PEER_TOPK + b0))
    top_s, pos = _topk_cols(jnp.concatenate(cand, axis=0), PEER_TOPK, jnp.concatenate(flat, axis=0))
    i1_ref[...] = _take_rows(j1, jnp.right_shift(pos, PEER_TOPK.bit_length() - 1))
    i2_ref[...] = _take_rows(j2, jnp.bitwise_and(pos, PEER_TOPK - 1))
    e = jnp.exp(top_s - top_s[0:1, :])
    g_ref[...] = e / jnp.sum(e, axis=0, keepdims=True)


def _peer_topk(q, keys, *, skip):
    n = q.shape[0]
    t = 256
    off = skip * (ROW_BLOCK // t)
    rows = PEER_HEADS * PEER_TOPK
    out_spec = pl.BlockSpec((PEER_TOPK, t), lambda i, h: (h, i))
    vm = 2 * (_nbytes((t, PEER_KEY_DIM), MXU_DTYPE) + _nbytes(keys.shape[1:], MXU_DTYPE)) \
        + 8 * _nbytes((PEER_TOPK * PEER_TOPK, t), F32)
    return pl.pallas_call(
        _peer_topk_kernel, name="peer_topk",
        out_shape=(jax.ShapeDtypeStruct((rows, n), F32), jax.ShapeDtypeStruct((rows, n), jnp.int32),
                   jax.ShapeDtypeStruct((rows, n), jnp.int32)),
        grid=(n // t - off, PEER_HEADS),
        in_specs=[pl.BlockSpec((t, PEER_KEY_DIM), lambda i, h: (i, h)),
                  pl.BlockSpec((1,) + keys.shape[1:], lambda i, h: (h, 0, 0))],
        out_specs=(out_spec, out_spec, out_spec),
        compiler_params=_cparams(2, vm),
    )(q, keys)


def _peer_gates_kernel(i1_ref, i2_ref, g_ref, o_ref):
    n_sel = i1_ref.shape[1]
    iota = lax.broadcasted_iota(jnp.int32, (PEER_NKEYS, n_sel), 0)
    nt = (((1,), (1,)), ((), ()))

    def body(t, carry):
        i1b = i1_ref[pl.ds(t, 1), :]
        i2b = i2_ref[pl.ds(t, 1), :]
        g = g_ref[pl.ds(t, 1), :]
        g_hi = g.astype(MXU_DTYPE).astype(F32)
        g_lo = g - g_hi
        m1 = iota == i1b
        lhs = jnp.concatenate([jnp.where(m1, g_hi, 0.0), jnp.where(m1, g_lo, 0.0)], axis=1).astype(MXU_DTYPE)
        hot2 = jnp.where(iota == i2b, 1.0, 0.0).astype(MXU_DTYPE)
        rhs = jnp.concatenate([hot2, hot2], axis=1)
        o_ref[t] = lax.dot_general(lhs, rhs, nt, preferred_element_type=F32)
        return carry

    lax.fori_loop(0, i1_ref.shape[0], body, 0, unroll=32)


def _peer_gates(i1, i2, g, *, skip):
    n, n_sel = i1.shape
    t = 128
    off = skip * (ROW_BLOCK // t)
    spec = pl.BlockSpec((t, n_sel), lambda i: (i, 0))
    vm = 2 * (3 * _nbytes((t, n_sel), F32) + _nbytes((t, PEER_NKEYS, PEER_NKEYS), F32))
    return pl.pallas_call(
        _peer_gates_kernel, name="peer_gates",
        out_shape=jax.ShapeDtypeStruct((n, PEER_NKEYS, PEER_NKEYS), F32),
        grid=(n // t - off,),
        in_specs=[spec, spec, spec],
        out_specs=pl.BlockSpec((t, PEER_NKEYS, PEER_NKEYS), lambda i: (i, 0, 0)),
        compiler_params=_cparams(1, vm),
    )(i1, i2, g)


def _peer_up_kernel(h_ref, u_ref, g_ref, o_ref):
    acc = lax.dot_general(h_ref[...], u_ref[...], (((1,), (1,)), ((), ())), preferred_element_type=F32)
    for jj in range(g_ref.shape[1]):
        a = acc[:, jj * PEER_NKEYS:(jj + 1) * PEER_NKEYS]
        act = 0.5 * a * (1.0 + lax.erf(a * (2.0 ** -0.5)))
        o_ref[:, jj * PEER_NKEYS:(jj + 1) * PEER_NKEYS] = (act * g_ref[:, jj, :]).astype(o_ref.dtype)


def _peer_up(h, u, layer, gmat, *, skip):
    n, d = h.shape
    e = u.shape[1]
    tm = ROW_BLOCK
    ti = SUBLANES
    tn = ti * PEER_NKEYS
    vm = 2 * (_nbytes((tm, d), MXU_DTYPE) + _nbytes((d, tn), MXU_DTYPE) + _nbytes((tm, ti, PEER_NKEYS), F32)
              + _nbytes((tm, tn), MXU_DTYPE)) + 3 * _nbytes((tm, tn), F32)
    return pl.pallas_call(
        _peer_up_kernel, name="peer_up",
        out_shape=jax.ShapeDtypeStruct((n, e), MXU_DTYPE),
        grid=(n // tm - skip, e // tn),
        in_specs=[pl.BlockSpec((tm, d), lambda i, j: (i, 0)),
                  pl.BlockSpec((None, tn, d), lambda i, j: (layer, j, 0)),
                  pl.BlockSpec((tm, ti, PEER_NKEYS), lambda i, j: (i, j, 0))],
        out_specs=pl.BlockSpec((tm, tn), lambda i, j: (i, j)),
        compiler_params=_cparams(2, vm),
    )(h, u, gmat)


def _final_norm_kernel(x_ref, g_ref, o_ref):
    x = x_ref[...]
    o_ref[...] = x * lax.rsqrt(jnp.mean(x * x, axis=-1, keepdims=True) + EPS) * g_ref[...]


def _final_norm(x, gain, *, nl):
    d = x.shape[1]
    tb = 256
    vm = 4 * _nbytes((tb, d), F32) + 2 * _nbytes((tb, d), F32)
    return pl.pallas_call(
        _final_norm_kernel, name="final_norm",
        out_shape=jax.ShapeDtypeStruct((nl, d), F32),
        grid=(nl // tb,),
        in_specs=[pl.BlockSpec((tb, d), lambda i: (i, 0)), pl.BlockSpec((1, d), lambda i: (0, 0))],
        out_specs=pl.BlockSpec((tb, d), lambda i: (i, 0)),
        compiler_params=_cparams(1, vm),
    )(x, gain.reshape(1, d))


def _regroup_kernel(w_ref, o_ref, *, pad_from, pad_to):
    half = MLA_ROPE // 2
    r0 = pl.program_id(1) * o_ref.shape[0]
    in_pad = (r0 >= pad_from) & (r0 < pad_to)

    @pl.when(jnp.logical_not(in_pad))
    def _():
        o_ref[...] = w_ref[0].astype(o_ref.dtype)

    @pl.when(in_pad)
    def _():
        o_ref[...] = jnp.zeros_like(o_ref)

    @pl.when(r0 == pad_from)
    def _():
        o_ref[0:half, :] = w_ref[0, 0:half, :].astype(o_ref.dtype)
        o_ref[LANES // 2:LANES // 2 + half, :] = w_ref[0, half:2 * half, :].astype(o_ref.dtype)


def _regroup_in_proj(w_in_t, *, wide, kv_rank, d):
    n_layer, cols, _ = w_in_t.shape
    src_ckv = 4 * wide
    src_rope = src_ckv + kv_rank
    src_conv = src_rope + MLA_ROPE
    src_gate = src_conv + 3 * wide
    assert src_gate + N_BRANCH * d == cols
    dst_ckv = 7 * wide
    dst_rope = dst_ckv + kv_rank
    dst_gate = 8 * wide
    out_rows = dst_gate + N_BRANCH * d
    tr = 256
    assert dst_rope % tr == 0 and kv_rank % tr == 0 and (N_BRANCH * d) % tr == 0 and tr >= LANES

    def src_row(l, i):
        r = i * tr
        src = jnp.where(r < 4 * wide, r,
                        jnp.where(r < dst_ckv, r - 4 * wide + src_conv,
                                  jnp.where(r < dst_rope, r - dst_ckv + src_ckv,
                                            jnp.where(r < dst_gate, src_rope, r - dst_gate + src_gate))))
        return (l, pl.multiple_of(src, MLA_ROPE), 0)

    vm = 2 * (_nbytes((tr, d), F32) + _nbytes((tr, d), MXU_DTYPE)) + _nbytes((tr, d), F32)
    return pl.pallas_call(
        functools.partial(_regroup_kernel, pad_from=dst_rope, pad_to=dst_gate),
        name="regroup_in_proj",
        out_shape=jax.ShapeDtypeStruct((n_layer, out_rows, d), MXU_DTYPE),
        grid=(n_layer, out_rows // tr),
        in_specs=[pl.BlockSpec((pl.Element(1), pl.Element(tr), pl.Element(d)), src_row)],
        out_specs=pl.BlockSpec((None, tr, d), lambda l, i: (l, i, 0)),
        compiler_params=_cparams(2, vm),
    )(w_in_t)


def _rope_lane_layout(w):
    half = MLA_ROPE // 2
    z = jnp.zeros(w.shape[:-1] + (LANES // 2 - half,), w.dtype)
    return jnp.concatenate([w[..., :half], z, w[..., half:], z], axis=-1)


def _rope_tables(seq, nc):
    pos = jnp.arange(seq, dtype=jnp.int32)
    n_freq = MLA_ROPE // 4
    inv_freq = jnp.power(ROPE_BASE, -jnp.arange(n_freq, dtype=F32) / n_freq)
    row = (pos // GRID_W).astype(F32)
    col = (pos % GRID_W).astype(F32)
    ang = jnp.concatenate([row[:, None] * inv_freq, col[:, None] * inv_freq], axis=-1)
    cos, sin = jnp.cos(ang), jnp.sin(ang)
    cos_l = _rope_lane_layout(jnp.concatenate([cos, cos], axis=-1))
    sin_l = _rope_lane_layout(jnp.concatenate([-sin, sin], axis=-1))
    one = _rope_lane_layout(jnp.ones((nc, MLA_ROPE), F32))
    return one, jnp.zeros((nc, LANES), F32), cos_l, sin_l


def kernel(x, c, ctx, c_ctx, w_ada, b_ada, norm_mix, norm_ffn, w_in, na_rpb, mla_q_norm, mla_kv_norm, mla_w_uq, mla_w_ukv, conv_w, w_branch_a, w_branch_b, w_branch_c, w_out, peer_w_q, peer_sub_keys, peer_u, peer_v, final_norm):
    n_batch, seq, d = x.shape
    lc = ctx.shape[1]
    depth = w_ada.shape[0]
    nc = n_batch * lc
    nl = n_batch * seq
    na_w = NA_HEADS * HEAD_DIM
    q_rank, kv_rank = mla_w_uq.shape[1], mla_w_ukv.shape[1]
    sc_w = conv_w.shape[2]
    rows = seq // GRID_W
    assert nc == ROW_BLOCK and seq % ROW_BLOCK == 0 and d % LANES == 0
    assert rows % NA_Q_ROWS == 0 and rows >= NA_K_ROWS
    assert na_w == q_rank == sc_w == MLA_HEADS * MLA_V and kv_rank * 2 == na_w
    kw = dict(nl=nl, seq=seq, n_batch=n_batch)

    wide = na_w
    col_gate = 8 * wide
    proj_cols = col_gate + N_BRANCH * d

    cmat = jnp.zeros((SUBLANES, d), F32).at[:n_batch].set(c).at[n_batch].set(c_ctx)
    xs = jnp.concatenate([x.reshape(nl, d), ctx.reshape(nc, d)], axis=0)

    na_pairs, na_pair_of, case_of_blk, dc_hot, col_ok = _na_bias_tables(rows)
    one_c, zero_c, cos_l, sin_l = _rope_tables(seq, nc)
    cos_t = jnp.concatenate([cos_l] * n_batch + [one_c], axis=0)
    sin_t = jnp.concatenate([sin_l] * n_batch + [zero_c], axis=0)

    w_proj = _regroup_in_proj(jnp.swapaxes(w_in, 1, 2), wide=wide, kv_rank=kv_rank, d=d)
    b_ada3 = b_ada.reshape(depth, 1, N_MOD * d)
    wa_c, wb_c, wc_c = (w.astype(MXU_DTYPE) for w in (w_branch_a, w_branch_b, w_branch_c))
    w_out_c = w_out.astype(MXU_DTYPE)
    w_q_c = peer_w_q.astype(MXU_DTYPE)
    u_c = peer_u.astype(MXU_DTYPE)
    v_c = peer_v.astype(MXU_DTYPE)

    for l in range(depth):
        last = l == depth - 1
        skip = 1 if last else 0
        wq = mla_w_uq[l].reshape(q_rank, MLA_HEADS, MLA_NOPE + MLA_ROPE)
        wq = jnp.concatenate([wq[..., :MLA_NOPE], _rope_lane_layout(wq[..., MLA_NOPE:])], axis=-1)
        wq = wq.reshape(q_rank, MLA_HEADS * MLA_HEAD_PAD).astype(MXU_DTYPE)
        wkv = mla_w_ukv[l].reshape(kv_rank, MLA_HEADS, MLA_NOPE + MLA_V)
        wk = wkv[..., :MLA_NOPE].reshape(kv_rank, MLA_HEADS * MLA_NOPE).astype(MXU_DTYPE)
        wv = wkv[..., MLA_NOPE:].reshape(kv_rank, MLA_HEADS * MLA_V).astype(MXU_DTYPE)
        pair_table = _na_pair_table(na_rpb[l], na_pairs, dc_hot, col_ok)

        mod = _ada_table(cmat, w_ada, b_ada3, l)
        modtab = mod[:n_batch + 1].reshape((n_batch + 1) * N_MOD, 1, d)

        h = _normmod(xs, norm_mix[l], modtab, 0, skip=0, **kw)
        proj = _matmul(h, w_proj, l, tn=_tile(proj_cols, (2048, 1024, 512, 256)), out_dtype=MXU_DTYPE, w_rows=True,
                       name="in_proj")
        hd = wide // HEAD_DIM
        ya = _na_attention(proj, pair_table, jnp.asarray(case_of_blk), jnp.asarray(na_pair_of), lc=lc,
                           col_q=0, col_k=hd // NA_HEADS_PER_STEP, col_v=2 * hd // NA_HEADS_PER_STEP, **kw)
        qf, kf, vf = _mla_up(proj, mla_q_norm[l], mla_kv_norm[l], wq, wk, wv, cos_t, sin_t,
                             col_cq=3, col_ckv=7 * wide // kv_rank, col_kr=(7 * wide + kv_rank) // LANES)
        yb = _mla_attention(qf, kf, vf, lc=lc, **kw)
        if not last:
            ya = _ctx_attention(proj, proj, proj, ya, nl=nl, lc=lc, n_batch=n_batch, n_heads=NA_HEADS, dk=HEAD_DIM,
                                dv=HEAD_DIM, col_q=0, col_k=hd, col_v=2 * hd, scale=HEAD_DIM ** -0.5)
            yb = _ctx_attention(qf, kf, vf, yb, nl=nl, lc=lc, n_batch=n_batch, n_heads=MLA_HEADS, dk=MLA_HEAD_PAD,
                                dv=MLA_V, col_q=0, col_k=0, col_v=0, scale=(MLA_NOPE + MLA_ROPE) ** -0.5)
        tn_m = 1024 if d % 1024 == 0 else d
        m = _merge(ya, yb, proj, conv_w, wa_c, wb_c, wc_c, l, d=d, nl=nl, seq=seq, lc=lc, skip=skip,
                   col_gb=4, col_gc=5, col_xin=6, col_gate=col_gate // tn_m)
        xs = _matmul_resid(m, w_out_c, l, xs, modtab, 2, tn=tn_m, tk=d, skip=skip, name="out_proj", **kw)

        h2 = _normmod(xs, norm_ffn[l], modtab, 3, skip=skip, **kw)
        pq = _matmul(h2, w_q_c, l, tn=2048, out_dtype=MXU_DTYPE, skip=skip, name="peer_query")
        keys = peer_sub_keys[l].reshape(PEER_HEADS, 2 * PEER_NKEYS, PEER_KEY_DIM // 2).astype(MXU_DTYPE)
        gates_t, i1_t, i2_t = _peer_topk(pq, keys, skip=skip)
        gmat = _peer_gates(i1_t.T, i2_t.T, gates_t.T, skip=skip)
        hexp = _peer_up(h2, u_c, l, gmat, skip=skip)
        xs = _matmul_resid(hexp, v_c, l, xs, modtab, 5, tn=min(d, 2048), tk=2048, skip=skip, name="peer_down",
                           **kw)

    out = _final_norm(xs, final_norm, nl=nl)
    return out.reshape(n_batch, seq, d)
```

```python
import functools
import math

import numpy as np
import jax
import jax.numpy as jnp
from jax import lax
from jax.experimental import pallas as pl
from jax.experimental.pallas import tpu as pltpu

F32 = jnp.float32
MXU_DTYPE = jnp.bfloat16

GRID_W = 64
EPS = 1e-6
N_MOD = 6
NEG_INF = -1e30
HEAD_DIM = 128
NA_HEADS = 8
NA_WIN_H = 8
NA_WIN_W = 16
MLA_HEADS = 8
MLA_NOPE = 128
MLA_ROPE = 64
MLA_V = 128
ROPE_BASE = 10000.0
SC_TAPS = 3
N_BRANCH = 3
PEER_HEADS = 8
PEER_NKEYS = 128
PEER_KEY_DIM = 256
PEER_TOPK = 16

V7X_VMEM_BYTES = 64 * 1024 * 1024
LANES = 128
SUBLANES = 8

ROW_BLOCK = 512
MLA_HEAD_PAD = 256
NA_Q_ROWS = 8
NA_K_ROWS = 16
NA_K_CHUNK_ROWS = 4
NA_HEADS_PER_STEP = 4
PEER_HEADS_PER_STEP = 4


def _cparams(n_axes, vmem_bytes):
    limit = min(int(vmem_bytes) + (4 << 20), V7X_VMEM_BYTES - (6 << 20))
    return pltpu.CompilerParams(dimension_semantics=("arbitrary",) * n_axes,
                                vmem_limit_bytes=limit)


def _nbytes(shape, dtype):
    return int(np.prod(shape)) * jnp.dtype(dtype).itemsize


def _tile(n, prefs):
    return next(t for t in prefs if n % t == 0)


def _mod_row(row0, nl, seq, n_batch):
    return jnp.where(row0 >= nl, n_batch, row0 // seq)


def _ada_kernel(c_ref, w_ref, b_ref, o_ref):
    c = c_ref[...]
    s = (c * jax.nn.sigmoid(c)).astype(MXU_DTYPE)
    acc = jnp.dot(s, w_ref[...].astype(MXU_DTYPE), preferred_element_type=F32)
    o_ref[...] = acc + b_ref[...]


def _ada_table(cmat, w_ada, b_ada, layer):
    rows, d = cmat.shape
    n = w_ada.shape[2]
    tn = 512
    vm = 2 * (_nbytes((d, tn), F32) + _nbytes((rows, tn), F32) * 2) + _nbytes((rows, d), F32) * 2 \
        + _nbytes((d, tn), MXU_DTYPE)
    return pl.pallas_call(
        _ada_kernel, name="ada_table",
        out_shape=jax.ShapeDtypeStruct((rows, n), F32),
        grid=(n // tn,),
        in_specs=[pl.BlockSpec((rows, d), lambda j: (0, 0)),
                  pl.BlockSpec((None, d, tn), lambda j: (layer, 0, j)),
                  pl.BlockSpec((None, 1, tn), lambda j: (layer, 0, j))],
        out_specs=pl.BlockSpec((rows, tn), lambda j: (0, j)),
        compiler_params=_cparams(1, vm),
    )(cmat, w_ada, b_ada)


def _normmod_kernel(x_ref, g_ref, sh_ref, sc_ref, o_ref):
    x = x_ref[...]
    y = x * lax.rsqrt(jnp.mean(x * x, axis=-1, keepdims=True) + EPS)
    o_ref[...] = ((y * g_ref[...]) * (1.0 + sc_ref[0]) + sh_ref[0]).astype(o_ref.dtype)


def _normmod(x, gain, modtab, m_shift, *, nl, seq, n_batch, skip):
    n, d = x.shape
    tb = 256
    off = skip * (ROW_BLOCK // tb)

    def mrow(i, m):
        return (_mod_row(i * tb, nl, seq, n_batch) * N_MOD + m, 0, 0)

    vm = 2 * (_nbytes((tb, d), F32) + _nbytes((tb, d), MXU_DTYPE)) + 3 * _nbytes((tb, d), F32)
    return pl.pallas_call(
        _normmod_kernel, name="normmod",
        out_shape=jax.ShapeDtypeStruct((n, d), MXU_DTYPE),
        grid=(n // tb - off,),
        in_specs=[pl.BlockSpec((tb, d), lambda i: (i, 0)),
                  pl.BlockSpec((1, d), lambda i: (0, 0)),
                  pl.BlockSpec((1, 1, d), lambda i: mrow(i, m_shift)),
                  pl.BlockSpec((1, 1, d), lambda i: mrow(i, m_shift + 1))],
        out_specs=pl.BlockSpec((tb, d), lambda i: (i, 0)),
        compiler_params=_cparams(1, vm),
    )(x, gain.reshape(1, d), modtab, modtab)


def _mm_kernel(x_ref, w_ref, o_ref):
    o_ref[...] = jnp.dot(x_ref[...], w_ref[...], preferred_element_type=F32).astype(o_ref.dtype)


def _mm_nt_kernel(x_ref, w_ref, o_ref):
    o_ref[...] = lax.dot_general(x_ref[...], w_ref[...], (((1,), (1,)), ((), ())),
                                 preferred_element_type=F32).astype(o_ref.dtype)


def _matmul(x, w, layer, *, tn, out_dtype, skip=0, w_rows=False, name="matmul"):
    n, k = x.shape
    nn = w.shape[1] if w_rows else w.shape[2]
    tm = ROW_BLOCK
    vm = 2 * (_nbytes((tm, k), x.dtype) + _nbytes((k, tn), w.dtype) + _nbytes((tm, tn), out_dtype)) \
        + _nbytes((tm, tn), F32)
    w_spec = (pl.BlockSpec((None, tn, k), lambda j, i: (layer, j, 0)) if w_rows
              else pl.BlockSpec((None, k, tn), lambda j, i: (layer, 0, j)))
    return pl.pallas_call(
        _mm_nt_kernel if w_rows else _mm_kernel, name=name,
        out_shape=jax.ShapeDtypeStruct((n, nn), out_dtype),
        grid=(nn // tn, n // tm - skip),
        in_specs=[pl.BlockSpec((tm, k), lambda j, i: (i, 0)), w_spec],
        out_specs=pl.BlockSpec((tm, tn), lambda j, i: (i, j)),
        compiler_params=_cparams(2, vm),
    )(x, w)


def _mm_resid_kernel(x_ref, w_ref, r_ref, g_ref, o_ref):
    @pl.when(pl.program_id(2) == 0)
    def _():
        o_ref[...] = r_ref[...]

    o_ref[...] += g_ref[0] * jnp.dot(x_ref[...], w_ref[...], preferred_element_type=F32)


def _matmul_resid(x, w, layer, resid, modtab, m_gate, *, tn, tk, nl, seq, n_batch, skip, name):
    n, kk = x.shape
    nn = w.shape[2]
    tm = ROW_BLOCK

    def mrow(i, j, k):
        return (_mod_row(i * tm, nl, seq, n_batch) * N_MOD + m_gate, 0, j)

    vm = 2 * (_nbytes((tm, tk), x.dtype) + _nbytes((tk, tn), w.dtype) + 2 * _nbytes((tm, tn), F32)) \
        + 2 * _nbytes((tm, tn), F32)
    return pl.pallas_call(
        _mm_resid_kernel, name=name,
        out_shape=jax.ShapeDtypeStruct((n, nn), F32),
        grid=(n // tm - skip, nn // tn, kk // tk),
        in_specs=[pl.BlockSpec((tm, tk), lambda i, j, k: (i, k)),
                  pl.BlockSpec((None, tk, tn), lambda i, j, k: (layer, k, j)),
                  pl.BlockSpec((tm, tn), lambda i, j, k: (i, j)),
                  pl.BlockSpec((1, 1, tn), mrow)],
        out_specs=pl.BlockSpec((tm, tn), lambda i, j, k: (i, j)),
        compiler_params=_cparams(3, vm),
    )(x, w, resid, modtab)


def _na_bias_tables(rows):
    assert 2 * GRID_W == LANES and NA_K_ROWS % 2 == 0
    n_dr = 2 * NA_WIN_H - 1
    n_blk = rows // NA_Q_ROWS
    rl = np.arange(NA_Q_ROWS)[:, None]
    krl = np.arange(NA_K_ROWS)[None, :]
    cases, case_of_blk = [], []
    for i in range(n_blk):
        r0 = i * NA_Q_ROWS
        k0 = int(np.clip(r0 - NA_K_CHUNK_ROWS, 0, rows - NA_K_ROWS))
        r = r0 + rl
        kr = k0 + krl
        start = np.clip(r - NA_WIN_H // 2, 0, rows - NA_WIN_H)
        ok = (kr >= start) & (kr < start + NA_WIN_H)
        dr = np.where(ok, kr - r + NA_WIN_H - 1, n_dr).astype(np.int32)
        assert dr.min() >= 0
        for ci, known in enumerate(cases):
            if np.array_equal(known, dr):
                case_of_blk.append(ci)
                break
        else:
            case_of_blk.append(len(cases))
            cases.append(dr)
    c = np.arange(GRID_W)[:, None]
    kc = np.arange(GRID_W)[None, :]
    ws = np.clip(c - NA_WIN_W // 2, 0, GRID_W - NA_WIN_W)
    col_ok = (kc >= ws) & (kc < ws + NA_WIN_W)
    dc = np.clip(kc - c + NA_WIN_W - 1, 0, 2 * NA_WIN_W - 2)
    dc_hot = (dc[None] == np.arange(2 * NA_WIN_W - 1)[:, None, None]).astype(np.float32)
    dr_of = np.stack(cases).reshape(len(cases), NA_Q_ROWS, NA_K_ROWS // 2, 2)
    pairs, pair_of = np.unique(dr_of.reshape(-1, 2), axis=0, return_inverse=True)
    return (pairs.astype(np.int32), pair_of.reshape(-1).astype(np.int32), np.asarray(case_of_blk, np.int32),
            dc_hot, col_ok)


def _na_pair_table(rpb, pairs, dc_hot, col_ok):
    n_h = rpb.shape[0]
    tb = jnp.einsum('hdk,kcq->hdcq', rpb.astype(F32), dc_hot, precision=lax.Precision.HIGHEST)
    tb = jnp.where(col_ok, tb, NEG_INF)
    tb = jnp.concatenate([tb, jnp.full((n_h, 1, GRID_W, GRID_W), NEG_INF, F32)], axis=1)
    return jnp.concatenate([tb[:, pairs[:, 0]], tb[:, pairs[:, 1]]], axis=-1)


def _na_kernel(case_ref, pair_ref, q_ref, k0_ref, k1_ref, k2_ref, k3_ref, v0_ref, v1_ref, v2_ref, v3_ref,
               kc_ref, vc_ref, b_ref, o_ref):
    scale = HEAD_DIM ** -0.5
    nt = (((1,), (1,)), ((), ()))
    n_pair = NA_K_ROWS // 2
    base = case_ref[pl.program_id(2)] * (NA_Q_ROWS * n_pair)
    for hh in range(NA_HEADS_PER_STEP):
        cs = slice(hh * HEAD_DIM, (hh + 1) * HEAD_DIM)
        q = q_ref[:, cs]
        k_loc = jnp.concatenate([k0_ref[:, cs], k1_ref[:, cs], k2_ref[:, cs], k3_ref[:, cs]], axis=0)
        v_loc = jnp.concatenate([v0_ref[:, cs], v1_ref[:, cs], v2_ref[:, cs], v3_ref[:, cs]], axis=0)
        s_loc = lax.dot_general(q, k_loc, nt, preferred_element_type=F32) * scale
        biased = []
        for rl in range(NA_Q_ROWS):
            blocks = [b_ref[hh, pair_ref[base + rl * n_pair + p]] for p in range(n_pair)]
            biased.append(s_loc[rl * GRID_W:(rl + 1) * GRID_W, :] + jnp.concatenate(blocks, axis=1))
        s_loc = jnp.concatenate(biased, axis=0)
        s_ctx = lax.dot_general(q, kc_ref[:, cs], nt, preferred_element_type=F32) * scale
        m = jnp.maximum(jnp.max(s_loc, axis=-1, keepdims=True), jnp.max(s_ctx, axis=-1, keepdims=True))
        p_loc = jnp.exp(s_loc - m)
        p_ctx = jnp.exp(s_ctx - m)
        denom = jnp.sum(p_loc, axis=-1, keepdims=True) + jnp.sum(p_ctx, axis=-1, keepdims=True)
        o = (jnp.dot(p_loc.astype(MXU_DTYPE), v_loc, preferred_element_type=F32)
             + jnp.dot(p_ctx.astype(MXU_DTYPE), vc_ref[:, cs], preferred_element_type=F32))
        o_ref[:, cs] = (o / denom).astype(o_ref.dtype)


def _na_attention(proj, pair_table, case_of_blk, pair_of, *, nl, seq, lc, n_batch, col_q, col_k, col_v):
    n = proj.shape[0]
    rows = seq // GRID_W
    tq = NA_Q_ROWS * GRID_W
    tkc = NA_K_CHUNK_ROWS * GRID_W
    n_blk = rows // NA_Q_ROWS
    n_chunk = NA_K_ROWS // NA_K_CHUNK_ROWS
    max_c0 = rows // NA_K_CHUNK_ROWS - n_chunk
    hw = NA_HEADS_PER_STEP * HEAD_DIM

    def qmap(h, b, i, case, pair):
        return (b * (seq // tq) + i, col_q + h)

    def kvmap(col, j):
        def f(h, b, i, case, pair):
            c0 = jnp.clip(i * (NA_Q_ROWS // NA_K_CHUNK_ROWS) - 1, 0, max_c0)
            return (b * (seq // tkc) + c0 + j, col + h)
        return f

    def ctxmap(col):
        return lambda h, b, i, case, pair: (nl // lc + b, col + h)

    in_specs = [pl.BlockSpec((tq, hw), qmap)]
    in_specs += [pl.BlockSpec((tkc, hw), kvmap(col_k, j)) for j in range(n_chunk)]
    in_specs += [pl.BlockSpec((tkc, hw), kvmap(col_v, j)) for j in range(n_chunk)]
    in_specs += [pl.BlockSpec((lc, hw), ctxmap(col_k)), pl.BlockSpec((lc, hw), ctxmap(col_v))]
    in_specs += [pl.BlockSpec((NA_HEADS_PER_STEP,) + pair_table.shape[1:],
                              lambda h, b, i, case, pair: (h, 0, 0, 0))]
    vm = 2 * (NA_HEADS_PER_STEP * _nbytes(pair_table.shape[1:], F32) + 12 * _nbytes((tq, hw), MXU_DTYPE)) \
        + 7 * NA_HEADS_PER_STEP * _nbytes((tq, NA_K_ROWS * GRID_W + lc), F32)
    grid_spec = pltpu.PrefetchScalarGridSpec(
        num_scalar_prefetch=2, grid=(NA_HEADS // NA_HEADS_PER_STEP, n_batch, n_blk),
        in_specs=in_specs,
        out_specs=pl.BlockSpec((tq, hw), lambda h, b, i, case, pair: (b * (seq // tq) + i, h)))
    return pl.pallas_call(
        _na_kernel, name="na_attention",
        out_shape=jax.ShapeDtypeStruct((n, NA_HEADS * HEAD_DIM), MXU_DTYPE),
        grid_spec=grid_spec,
        compiler_params=_cparams(3, vm),
    )(case_of_blk, pair_of, *([proj] * (1 + 2 * n_chunk + 2)), pair_table)


def _ctx_attn_kernel(q_ref, k_ref, v_ref, prev_ref, o_ref, *, scale):
    del prev_ref
    s = lax.dot_general(q_ref[...], k_ref[...], (((1,), (1,)), ((), ())),
                        preferred_element_type=F32) * scale
    m = jnp.max(s, axis=-1, keepdims=True)
    p = jnp.exp(s - m)
    o = jnp.dot(p.astype(MXU_DTYPE), v_ref[...], preferred_element_type=F32)
    o_ref[...] = (o / jnp.sum(p, axis=-1, keepdims=True)).astype(o_ref.dtype)


def _ctx_attention(qa, ka, va, y, *, nl, lc, n_batch, n_heads, dk, dv, col_q, col_k, col_v, scale):
    vm = 2 * (2 * _nbytes((lc, dk), MXU_DTYPE) + 2 * _nbytes((lc, dv), MXU_DTYPE)) + 4 * _nbytes((lc, lc), F32)
    return pl.pallas_call(
        functools.partial(_ctx_attn_kernel, scale=scale), name="ctx_attention",
        out_shape=jax.ShapeDtypeStruct(y.shape, y.dtype),
        grid=(n_batch, n_heads),
        in_specs=[pl.BlockSpec((lc, dk), lambda b, h: (nl // lc + b, col_q + h)),
                  pl.BlockSpec((lc, dk), lambda b, h: (nl // lc + b, col_k + h)),
                  pl.BlockSpec((lc, dv), lambda b, h: (nl // lc + b, col_v + h)),
                  pl.BlockSpec(memory_space=pl.ANY)],
        out_specs=pl.BlockSpec((lc, dv), lambda b, h: (nl // lc + b, h)),
        input_output_aliases={3: 0},
        compiler_params=_cparams(2, vm),
    )(qa, ka, va, y)


def _mla_up_kernel(cq_ref, ckv_ref, kr_ref, qn_ref, kvn_ref, wq_ref, wk_ref, wv_ref, cos_ref, sin_ref,
                   q_ref, k_ref, v_ref):
    def rms(x, g):
        return (x * lax.rsqrt(jnp.mean(x * x, axis=-1, keepdims=True) + EPS) * g).astype(MXU_DTYPE)

    cos = cos_ref[...]
    sin = sin_ref[...]

    def rope(x):
        return x * cos + pltpu.roll(x, 2 * (MLA_ROPE // 2), axis=1) * sin

    cqn = rms(cq_ref[...].astype(F32), qn_ref[...])
    kvn = rms(ckv_ref[...].astype(F32), kvn_ref[...])
    q = jnp.dot(cqn, wq_ref[...], preferred_element_type=F32)
    kn = jnp.dot(kvn, wk_ref[...], preferred_element_type=F32)
    v_ref[...] = jnp.dot(kvn, wv_ref[...], preferred_element_type=F32).astype(v_ref.dtype)
    kr = rope(kr_ref[...].astype(F32)).astype(k_ref.dtype)
    for h in range(MLA_HEADS):
        c0 = h * MLA_HEAD_PAD
        q_ref[:, c0:c0 + MLA_NOPE] = q[:, c0:c0 + MLA_NOPE].astype(q_ref.dtype)
        q_ref[:, c0 + MLA_NOPE:c0 + MLA_HEAD_PAD] = rope(q[:, c0 + MLA_NOPE:c0 + MLA_HEAD_PAD]).astype(q_ref.dtype)
        k_ref[:, c0:c0 + MLA_NOPE] = kn[:, h * MLA_NOPE:(h + 1) * MLA_NOPE].astype(k_ref.dtype)
        k_ref[:, c0 + MLA_NOPE:c0 + MLA_HEAD_PAD] = kr


def _mla_up(proj, q_norm, kv_norm, wq, wk, wv, cos_t, sin_t, *, col_cq, col_ckv, col_kr):
    n = proj.shape[0]
    tm = 256
    rq, rkv = wq.shape[0], wk.shape[0]
    wq_cols = wq.shape[1]
    vm = 2 * (_nbytes(wq.shape, MXU_DTYPE) + _nbytes(wk.shape, MXU_DTYPE) + _nbytes(wv.shape, MXU_DTYPE)
              + _nbytes((tm, rq + rkv + LANES), MXU_DTYPE) + 2 * _nbytes((tm, LANES), F32)
              + _nbytes((tm, 2 * wq_cols + wv.shape[1]), MXU_DTYPE)) + 3 * _nbytes((tm, wq_cols), F32)
    return pl.pallas_call(
        _mla_up_kernel, name="mla_up",
        out_shape=(jax.ShapeDtypeStruct((n, wq_cols), MXU_DTYPE),
                   jax.ShapeDtypeStruct((n, wq_cols), MXU_DTYPE),
                   jax.ShapeDtypeStruct((n, wv.shape[1]), MXU_DTYPE)),
        grid=(n // tm,),
        in_specs=[pl.BlockSpec((tm, rq), lambda i: (i, col_cq)),
                  pl.BlockSpec((tm, rkv), lambda i: (i, col_ckv)),
                  pl.BlockSpec((tm, LANES), lambda i: (i, col_kr)),
                  pl.BlockSpec((1, rq), lambda i: (0, 0)),
                  pl.BlockSpec((1, rkv), lambda i: (0, 0)),
                  pl.BlockSpec(wq.shape, lambda i: (0, 0)),
                  pl.BlockSpec(wk.shape, lambda i: (0, 0)),
                  pl.BlockSpec(wv.shape, lambda i: (0, 0)),
                  pl.BlockSpec((tm, LANES), lambda i: (i, 0)),
                  pl.BlockSpec((tm, LANES), lambda i: (i, 0))],
        out_specs=(pl.BlockSpec((tm, wq_cols), lambda i: (i, 0)),
                   pl.BlockSpec((tm, wq_cols), lambda i: (i, 0)),
                   pl.BlockSpec((tm, wv.shape[1]), lambda i: (i, 0))),
        compiler_params=_cparams(1, vm),
    )(proj, proj, proj, q_norm.reshape(1, rq), kv_norm.reshape(1, rkv), wq, wk, wv, cos_t, sin_t)


def _flash_kernel(q_ref, kc_ref, kl_ref, vc_ref, vl_ref, o_ref, m_ref, l_ref, acc_ref, *, scale, tk):
    nt = (((1,), (1,)), ((), ()))
    q = q_ref[...]
    c = scale * math.log2(math.e)

    def chunk(k, v):
        s = lax.dot_general(q, k, nt, preferred_element_type=F32)
        m_old = m_ref[...]
        m_new = jnp.maximum(m_old, jnp.max(s, axis=-1, keepdims=True) * c)
        alpha = jnp.exp2(m_old - m_new)
        p = jnp.exp2(s * c - jnp.tile(m_new, (1, s.shape[1] // LANES)))
        l_ref[...] = alpha * l_ref[...] + jnp.sum(p, axis=-1, keepdims=True)
        acc_ref[...] = alpha * acc_ref[...] + jnp.dot(p.astype(MXU_DTYPE), v, preferred_element_type=F32)
        m_ref[...] = m_new

    m_ref[...] = jnp.full_like(m_ref, -jnp.inf)
    l_ref[...] = jnp.zeros_like(l_ref)
    acc_ref[...] = jnp.zeros_like(acc_ref)
    chunk(kc_ref[...], vc_ref[...])

    def body(j, carry):
        r0 = pl.multiple_of(j * tk, tk)
        chunk(kl_ref[pl.ds(r0, tk), :], vl_ref[pl.ds(r0, tk), :])
        return carry

    lax.fori_loop(0, kl_ref.shape[0] // tk, body, 0)
    o_ref[...] = (acc_ref[...] / l_ref[...]).astype(o_ref.dtype)


def _mla_attention(qf, kf, vf, *, nl, seq, lc, n_batch):
    n = qf.shape[0]
    tq, tk = _tile(seq, (2048, 1024, 512)), 512
    dk, dv = MLA_HEAD_PAD, MLA_V
    assert dv == LANES
    scale = (MLA_NOPE + MLA_ROPE) ** -0.5
    vm = 2 * (_nbytes((tq, dk), MXU_DTYPE) + _nbytes((lc + seq, dk + dv), MXU_DTYPE) + _nbytes((tq, dv), MXU_DTYPE)) \
        + 6 * _nbytes((tq, tk), F32) + 2 * _nbytes((tq, dv), F32)
    return pl.pallas_call(
        functools.partial(_flash_kernel, scale=scale, tk=tk), name="mla_attention",
        out_shape=jax.ShapeDtypeStruct((n, MLA_HEADS * dv), MXU_DTYPE),
        grid=(n_batch, MLA_HEADS, seq // tq),
        in_specs=[pl.BlockSpec((tq, dk), lambda b, h, i: (b * (seq // tq) + i, h)),
                  pl.BlockSpec((lc, dk), lambda b, h, i: (nl // lc + b, h)),
                  pl.BlockSpec((seq, dk), lambda b, h, i: (b, h)),
                  pl.BlockSpec((lc, dv), lambda b, h, i: (nl // lc + b, h)),
                  pl.BlockSpec((seq, dv), lambda b, h, i: (b, h))],
        out_specs=pl.BlockSpec((tq, dv), lambda b, h, i: (b * (seq // tq) + i, h)),
        scratch_shapes=[pltpu.VMEM((tq, LANES), F32), pltpu.VMEM((tq, LANES), F32), pltpu.VMEM((tq, dv), F32)],
        compiler_params=_cparams(3, vm),
    )(qf, kf, kf, vf, vf)


def _merge_kernel(ya_ref, yb_ref, gb_ref, gc_ref, xin_ref, gcp_ref, xinp_ref, gcn_ref, xinn_ref, cw_ref,
                  wa_ref, wb_ref, wc_ref, g0_ref, g1_ref, g2_ref, o_ref, yc_ref, *, nl, seq, lc):
    i = pl.program_id(0)
    j = pl.program_id(1)
    tm = ya_ref.shape[0]

    @pl.when(j == 0)
    def _():
        u = gc_ref[...].astype(F32) * xin_ref[...].astype(F32)
        u_prev_halo = (gcp_ref[...].astype(F32) * xinp_ref[...].astype(F32))[SUBLANES - 1:SUBLANES, :]
        u_next_halo = (gcn_ref[...].astype(F32) * xinn_ref[...].astype(F32))[0:1, :]
        row = lax.broadcasted_iota(jnp.int32, (tm, 1), 0)
        g = i * tm + row
        pos = jnp.where(g < nl, g % seq, (g - nl) % lc)
        seq_len = jnp.where(g < nl, seq, lc)
        u_dn = jnp.where(row == 0, u_prev_halo, pltpu.roll(u, 1, axis=0))
        u_up = jnp.where(row == tm - 1, u_next_halo, pltpu.roll(u, tm - 1, axis=0))
        u_dn = jnp.where(pos == 0, 0.0, u_dn)
        u_up = jnp.where(pos == seq_len - 1, 0.0, u_up)
        cw = cw_ref[...]
        y = cw[0:1, :] * u_dn + cw[1:2, :] * u + cw[2:3, :] * u_up
        yc_ref[...] = (gb_ref[...].astype(F32) * y).astype(yc_ref.dtype)

    def branch(y, w_ref, g_ref):
        gate = jax.nn.sigmoid(g_ref[...].astype(F32))
        return gate * jnp.dot(y, w_ref[...], preferred_element_type=F32)

    m = (branch(ya_ref[...], wa_ref, g0_ref) + branch(yb_ref[...], wb_ref, g1_ref)
         + branch(yc_ref[...], wc_ref, g2_ref))
    o_ref[...] = m.astype(o_ref.dtype)


def _merge(ya, yb, proj, conv_w, wa, wb, wc, layer, *, d, nl, seq, lc, skip, col_gb, col_gc, col_xin, col_gate):
    n = ya.shape[0]
    tm, tn = ROW_BLOCK, 1024 if d % 1024 == 0 else d
    w = ya.shape[1]
    hb = tm // SUBLANES
    last_halo = n // SUBLANES - 1
    n_col = d // tn

    def row(i, j):
        return (i, 0)

    def cur(col):
        return lambda i, j: (i, col)

    def prev(col):
        return lambda i, j: (jnp.maximum(i * hb - 1, 0), col)

    def nxt(col):
        return lambda i, j: (jnp.minimum((i + 1) * hb, last_halo), col)

    def gate(br):
        return lambda i, j: (i, col_gate + br * n_col + j)

    in_specs = [pl.BlockSpec((tm, w), row), pl.BlockSpec((tm, w), row),
                pl.BlockSpec((tm, w), cur(col_gb)), pl.BlockSpec((tm, w), cur(col_gc)),
                pl.BlockSpec((tm, w), cur(col_xin)),
                pl.BlockSpec((SUBLANES, w), prev(col_gc)), pl.BlockSpec((SUBLANES, w), prev(col_xin)),
                pl.BlockSpec((SUBLANES, w), nxt(col_gc)), pl.BlockSpec((SUBLANES, w), nxt(col_xin)),
                pl.BlockSpec((None, SC_TAPS, w), lambda i, j: (layer, 0, 0)),
                pl.BlockSpec((None, w, tn), lambda i, j: (layer, 0, j)),
                pl.BlockSpec((None, w, tn), lambda i, j: (layer, 0, j)),
                pl.BlockSpec((None, w, tn), lambda i, j: (layer, 0, j)),
                pl.BlockSpec((tm, tn), gate(0)), pl.BlockSpec((tm, tn), gate(1)), pl.BlockSpec((tm, tn), gate(2))]
    vm = 2 * (5 * _nbytes((tm, w), MXU_DTYPE) + 3 * _nbytes((w, tn), MXU_DTYPE) + 4 * _nbytes((tm, tn), MXU_DTYPE)) \
        + _nbytes((tm, w), MXU_DTYPE) + 6 * _nbytes((tm, max(w, tn)), F32)
    return pl.pallas_call(
        functools.partial(_merge_kernel, nl=nl, seq=seq, lc=lc), name="conv_merge",
        out_shape=jax.ShapeDtypeStruct((n, d), MXU_DTYPE),
        grid=(n // tm - skip, n_col),
        in_specs=in_specs,
        out_specs=pl.BlockSpec((tm, tn), lambda i, j: (i, j)),
        scratch_shapes=[pltpu.VMEM((tm, w), MXU_DTYPE)],
        compiler_params=_cparams(2, vm),
    )(ya, yb, proj, proj, proj, proj, proj, proj, proj, conv_w, wa, wb, wc, proj, proj, proj)


def _sort_network(n):
    size = 1
    while size < n:
        size *= 2
    net = []

    def merge(lo, hi, r):
        step = r * 2
        if step < hi - lo:
            merge(lo, hi, step)
            merge(lo + r, hi, step)
            net.extend((i, i + r) for i in range(lo + r, hi - r, step))
        else:
            net.append((lo, lo + r))

    def sort(lo, hi):
        if hi - lo >= 1:
            mid = lo + (hi - lo) // 2
            sort(lo, mid)
            sort(mid + 1, hi)
            merge(lo, hi, 1)

    sort(0, size - 1)
    return [(i, j) for i, j in net if j < n]


def _topk_cols(s, k, ids=None):
    n_tile = s.shape[0] // SUBLANES
    v = [s[t * SUBLANES:(t + 1) * SUBLANES, :] for t in range(n_tile)]
    if ids is None:
        sub = lax.broadcasted_iota(jnp.int32, (SUBLANES, s.shape[1]), 0)
        r = [sub + t * SUBLANES for t in range(n_tile)]
    else:
        r = [ids[t * SUBLANES:(t + 1) * SUBLANES, :] for t in range(n_tile)]
    for a, b in _sort_network(n_tile):
        first = (v[a] > v[b]) | ((v[a] == v[b]) & (r[a] < r[b]))
        v[a], v[b] = jnp.where(first, v[a], v[b]), jnp.where(first, v[b], v[a])
        r[a], r[b] = jnp.where(first, r[a], r[b]), jnp.where(first, r[b], r[a])
    depth = min(n_tile, k)
    v, r = v[:depth], r[:depth]
    big = jnp.iinfo(jnp.int32).max
    vals, idxs = [], []
    for it in range(k):
        m = jnp.max(v[0], axis=0, keepdims=True)
        idx = jnp.min(jnp.where(v[0] == m, r[0], big), axis=0, keepdims=True)
        vals.append(m)
        idxs.append(idx)
        win = r[0] == idx
        live = min(depth, k - it)
        for t in range(live - 1):
            v[t] = jnp.where(win, v[t + 1], v[t])
            r[t] = jnp.where(win, r[t + 1], r[t])
        v[live - 1] = jnp.where(win, -jnp.inf, v[live - 1])
    return jnp.concatenate(vals, axis=0), jnp.concatenate(idxs, axis=0)


def _candidate_rows():
    k = PEER_TOPK
    groups = []
    for b0 in range(0, k, SUBLANES):
        groups.append((0, 1, b0, b0 + SUBLANES))
    a = 1
    while a < k and k // (a + 1) > 1:
        assert k // (a + 1) <= SUBLANES
        groups.append((a, a + 1, 0, SUBLANES))
        a += 1
    while a < k:
        groups.append((a, a + SUBLANES, 0, 1))
        a += SUBLANES
    covered = {(aa, bb) for a0, a1, b0, b1 in groups for aa in range(a0, a1) for bb in range(b0, b1)}
    assert all((aa, bb) in covered for aa in range(k) for bb in range(k) if (aa + 1) * (bb + 1) <= k)
    return groups


def _take_rows(table, idx):
    iota = lax.broadcasted_iota(jnp.int32, table.shape, 0)
    rows = [jnp.sum(jnp.where(iota == idx[j:j + 1, :], table, 0), axis=0, keepdims=True)
            for j in range(idx.shape[0])]
    return jnp.concatenate(rows, axis=0)


def _peer_topk_kernel(q_ref, keys_ref, g_ref, i1_ref, i2_ref):
    nt = (((1,), (1,)), ((), ()))
    half = PEER_KEY_DIM // 2
    sub = lax.broadcasted_iota(jnp.int32, (SUBLANES, q_ref.shape[0]), 0)
    for hh in range(PEER_HEADS_PER_STEP):
        q = q_ref[:, hh * PEER_KEY_DIM:(hh + 1) * PEER_KEY_DIM]
        s1 = lax.dot_general(keys_ref[hh, :PEER_NKEYS, :], q[:, :half], nt, preferred_element_type=F32)
        s2 = lax.dot_general(keys_ref[hh, PEER_NKEYS:, :], q[:, half:], nt, preferred_element_type=F32)
        v1, j1 = _topk_cols(s1, PEER_TOPK)
        v2, j2 = _topk_cols(s2, PEER_TOPK)
        cand, flat = [], []
        for a0, a1, b0, b1 in _candidate_rows():
            cand.append(v1[a0:a1, :] + v2[b0:b1, :])
            flat.append(sub * (1 if a1 - a0 == 1 else PEER_TOPK) + (a0 * PEER_TOPK + b0))
        top_s, pos = _topk_cols(jnp.concatenate(cand, axis=0), PEER_TOPK, jnp.concatenate(flat, axis=0))
        rows = slice(hh * PEER_TOPK, (hh + 1) * PEER_TOPK)
        i1_ref[rows, :] = _take_rows(j1, jnp.right_shift(pos, PEER_TOPK.bit_length() - 1))
        i2_ref[rows, :] = _take_rows(j2, jnp.bitwise_and(pos, PEER_TOPK - 1))
        e = jnp.exp(top_s - top_s[0:1, :])
        g_ref[rows, :] = e / jnp.sum(e, axis=0, keepdims=True)


def _peer_topk(q, keys, *, skip):
    n = q.shape[0]
    t = 256
    off = skip * (ROW_BLOCK // t)
    rows = PEER_HEADS * PEER_TOPK
    hs = PEER_HEADS_PER_STEP
    out_spec = pl.BlockSpec((hs * PEER_TOPK, t), lambda i, h: (h, i))
    vm = 2 * hs * (_nbytes((t, PEER_KEY_DIM), MXU_DTYPE) + _nbytes(keys.shape[1:], MXU_DTYPE)) \
        + 8 * hs * _nbytes((PEER_TOPK * PEER_TOPK, t), F32)
    return pl.pallas_call(
        _peer_topk_kernel, name="peer_topk",
        out_shape=(jax.ShapeDtypeStruct((rows, n), F32), jax.ShapeDtypeStruct((rows, n), jnp.int32),
                   jax.ShapeDtypeStruct((rows, n), jnp.int32)),
        grid=(n // t - off, PEER_HEADS // hs),
        in_specs=[pl.BlockSpec((t, hs * PEER_KEY_DIM), lambda i, h: (i, h)),
                  pl.BlockSpec((hs,) + keys.shape[1:], lambda i, h: (h, 0, 0))],
        out_specs=(out_spec, out_spec, out_spec),
        compiler_params=_cparams(2, vm),
    )(q, keys)


def _peer_gates_kernel(i1_ref, i2_ref, g_ref, o_ref):
    n_sel = i1_ref.shape[1]
    iota = lax.broadcasted_iota(jnp.int32, (PEER_NKEYS, n_sel), 0)
    nt = (((1,), (1,)), ((), ()))

    def body(t, carry):
        i1b = i1_ref[pl.ds(t, 1), :]
        i2b = i2_ref[pl.ds(t, 1), :]
        g = g_ref[pl.ds(t, 1), :]
        g_hi = g.astype(MXU_DTYPE).astype(F32)
        g_lo = g - g_hi
        m1 = iota == i1b
        lhs = jnp.concatenate([jnp.where(m1, g_hi, 0.0), jnp.where(m1, g_lo, 0.0)], axis=1).astype(MXU_DTYPE)
        hot2 = jnp.where(iota == i2b, 1.0, 0.0).astype(MXU_DTYPE)
        rhs = jnp.concatenate([hot2, hot2], axis=1)
        o_ref[t] = lax.dot_general(lhs, rhs, nt, preferred_element_type=F32)
        return carry

    lax.fori_loop(0, i1_ref.shape[0], body, 0, unroll=32)


def _peer_gates(i1, i2, g, *, skip):
    n, n_sel = i1.shape
    t = 128
    off = skip * (ROW_BLOCK // t)
    spec = pl.BlockSpec((t, n_sel), lambda i: (i, 0))
    vm = 2 * (3 * _nbytes((t, n_sel), F32) + _nbytes((t, PEER_NKEYS, PEER_NKEYS), F32))
    return pl.pallas_call(
        _peer_gates_kernel, name="peer_gates",
        out_shape=jax.ShapeDtypeStruct((n, PEER_NKEYS, PEER_NKEYS), F32),
        grid=(n // t - off,),
        in_specs=[spec, spec, spec],
        out_specs=pl.BlockSpec((t, PEER_NKEYS, PEER_NKEYS), lambda i: (i, 0, 0)),
        compiler_params=_cparams(1, vm),
    )(i1, i2, g)


def _peer_up_kernel(h_ref, u_ref, g_ref, o_ref):
    acc = lax.dot_general(h_ref[...], u_ref[...], (((1,), (1,)), ((), ())), preferred_element_type=F32)
    for jj in range(g_ref.shape[1]):
        a = acc[:, jj * PEER_NKEYS:(jj + 1) * PEER_NKEYS]
        act = 0.5 * a * (1.0 + lax.erf(a * (2.0 ** -0.5)))
        o_ref[:, jj * PEER_NKEYS:(jj + 1) * PEER_NKEYS] = (act * g_ref[:, jj, :]).astype(o_ref.dtype)


def _peer_up(h, u, layer, gmat, *, skip):
    n, d = h.shape
    e = u.shape[1]
    tm = ROW_BLOCK
    ti = SUBLANES
    tn = ti * PEER_NKEYS
    vm = 2 * (_nbytes((tm, d), MXU_DTYPE) + _nbytes((d, tn), MXU_DTYPE) + _nbytes((tm, ti, PEER_NKEYS), F32)
              + _nbytes((tm, tn), MXU_DTYPE)) + 3 * _nbytes((tm, tn), F32)
    return pl.pallas_call(
        _peer_up_kernel, name="peer_up",
        out_shape=jax.ShapeDtypeStruct((n, e), MXU_DTYPE),
        grid=(n // tm - skip, e // tn),
        in_specs=[pl.BlockSpec((tm, d), lambda i, j: (i, 0)),
                  pl.BlockSpec((None, tn, d), lambda i, j: (layer, j, 0)),
                  pl.BlockSpec((tm, ti, PEER_NKEYS), lambda i, j: (i, j, 0))],
        out_specs=pl.BlockSpec((tm, tn), lambda i, j: (i, j)),
        compiler_params=_cparams(2, vm),
    )(h, u, gmat)


def _final_norm_kernel(x_ref, g_ref, o_ref):
    x = x_ref[...]
    o_ref[...] = x * lax.rsqrt(jnp.mean(x * x, axis=-1, keepdims=True) + EPS) * g_ref[...]


def _final_norm(x, gain, *, nl):
    d = x.shape[1]
    tb = 256
    vm = 4 * _nbytes((tb, d), F32) + 2 * _nbytes((tb, d), F32)
    return pl.pallas_call(
        _final_norm_kernel, name="final_norm",
        out_shape=jax.ShapeDtypeStruct((nl, d), F32),
        grid=(nl // tb,),
        in_specs=[pl.BlockSpec((tb, d), lambda i: (i, 0)), pl.BlockSpec((1, d), lambda i: (0, 0))],
        out_specs=pl.BlockSpec((tb, d), lambda i: (i, 0)),
        compiler_params=_cparams(1, vm),
    )(x, gain.reshape(1, d))


def _regroup_kernel(w_ref, o_ref, *, pad_from, pad_to):
    half = MLA_ROPE // 2
    r0 = pl.program_id(1) * o_ref.shape[0]
    in_pad = (r0 >= pad_from) & (r0 < pad_to)

    @pl.when(jnp.logical_not(in_pad))
    def _():
        o_ref[...] = w_ref[0].astype(o_ref.dtype)

    @pl.when(in_pad)
    def _():
        o_ref[...] = jnp.zeros_like(o_ref)

    @pl.when(r0 == pad_from)
    def _():
        o_ref[0:half, :] = w_ref[0, 0:half, :].astype(o_ref.dtype)
        o_ref[LANES // 2:LANES // 2 + half, :] = w_ref[0, half:2 * half, :].astype(o_ref.dtype)


def _regroup_in_proj(w_in_t, *, wide, kv_rank, d):
    n_layer, cols, _ = w_in_t.shape
    src_ckv = 4 * wide
    src_rope = src_ckv + kv_rank
    src_conv = src_rope + MLA_ROPE
    src_gate = src_conv + 3 * wide
    assert src_gate + N_BRANCH * d == cols
    dst_ckv = 7 * wide
    dst_rope = dst_ckv + kv_rank
    dst_gate = 8 * wide
    out_rows = dst_gate + N_BRANCH * d
    tr = 256
    assert dst_rope % tr == 0 and kv_rank % tr == 0 and (N_BRANCH * d) % tr == 0 and tr >= LANES

    def src_row(l, i):
        r = i * tr
        src = jnp.where(r < 4 * wide, r,
                        jnp.where(r < dst_ckv, r - 4 * wide + src_conv,
                                  jnp.where(r < dst_rope, r - dst_ckv + src_ckv,
                                            jnp.where(r < dst_gate, src_rope, r - dst_gate + src_gate))))
        return (l, pl.multiple_of(src, MLA_ROPE), 0)

    vm = 2 * (_nbytes((tr, d), F32) + _nbytes((tr, d), MXU_DTYPE)) + _nbytes((tr, d), F32)
    return pl.pallas_call(
        functools.partial(_regroup_kernel, pad_from=dst_rope, pad_to=dst_gate),
        name="regroup_in_proj",
        out_shape=jax.ShapeDtypeStruct((n_layer, out_rows, d), MXU_DTYPE),
        grid=(n_layer, out_rows // tr),
        in_specs=[pl.BlockSpec((pl.Element(1), pl.Element(tr), pl.Element(d)), src_row)],
        out_specs=pl.BlockSpec((None, tr, d), lambda l, i: (l, i, 0)),
        compiler_params=_cparams(2, vm),
    )(w_in_t)


def _rope_lane_layout(w):
    half = MLA_ROPE // 2
    z = jnp.zeros(w.shape[:-1] + (LANES // 2 - half,), w.dtype)
    return jnp.concatenate([w[..., :half], z, w[..., half:], z], axis=-1)


def _rope_tables(seq, nc):
    pos = jnp.arange(seq, dtype=jnp.int32)
    n_freq = MLA_ROPE // 4
    inv_freq = jnp.power(ROPE_BASE, -jnp.arange(n_freq, dtype=F32) / n_freq)
    row = (pos // GRID_W).astype(F32)
    col = (pos % GRID_W).astype(F32)
    ang = jnp.concatenate([row[:, None] * inv_freq, col[:, None] * inv_freq], axis=-1)
    cos, sin = jnp.cos(ang), jnp.sin(ang)
    cos_l = _rope_lane_layout(jnp.concatenate([cos, cos], axis=-1))
    sin_l = _rope_lane_layout(jnp.concatenate([-sin, sin], axis=-1))
    one = _rope_lane_layout(jnp.ones((nc, MLA_ROPE), F32))
    return one, jnp.zeros((nc, LANES), F32), cos_l, sin_l


def kernel(x, c, ctx, c_ctx, w_ada, b_ada, norm_mix, norm_ffn, w_in, na_rpb, mla_q_norm, mla_kv_norm, mla_w_uq, mla_w_ukv, conv_w, w_branch_a, w_branch_b, w_branch_c, w_out, peer_w_q, peer_sub_keys, peer_u, peer_v, final_norm):
    n_batch, seq, d = x.shape
    lc = ctx.shape[1]
    depth = w_ada.shape[0]
    nc = n_batch * lc
    nl = n_batch * seq
    na_w = NA_HEADS * HEAD_DIM
    q_rank, kv_rank = mla_w_uq.shape[1], mla_w_ukv.shape[1]
    sc_w = conv_w.shape[2]
    rows = seq // GRID_W
    assert nc == ROW_BLOCK and seq % ROW_BLOCK == 0 and d % LANES == 0
    assert rows % NA_Q_ROWS == 0 and rows >= NA_K_ROWS
    assert na_w == q_rank == sc_w == MLA_HEADS * MLA_V and kv_rank * 2 == na_w
    kw = dict(nl=nl, seq=seq, n_batch=n_batch)

    wide = na_w
    col_gate = 8 * wide
    proj_cols = col_gate + N_BRANCH * d

    cmat = jnp.zeros((SUBLANES, d), F32).at[:n_batch].set(c).at[n_batch].set(c_ctx)
    xs = jnp.concatenate([x.reshape(nl, d), ctx.reshape(nc, d)], axis=0)

    na_pairs, na_pair_of, case_of_blk, dc_hot, col_ok = _na_bias_tables(rows)
    one_c, zero_c, cos_l, sin_l = _rope_tables(seq, nc)
    cos_t = jnp.concatenate([cos_l] * n_batch + [one_c], axis=0)
    sin_t = jnp.concatenate([sin_l] * n_batch + [zero_c], axis=0)

    w_proj = _regroup_in_proj(jnp.swapaxes(w_in, 1, 2), wide=wide, kv_rank=kv_rank, d=d)
    b_ada3 = b_ada.reshape(depth, 1, N_MOD * d)
    wa_c, wb_c, wc_c = (w.astype(MXU_DTYPE) for w in (w_branch_a, w_branch_b, w_branch_c))
    w_out_c = w_out.astype(MXU_DTYPE)
    w_q_c = peer_w_q.astype(MXU_DTYPE)
    u_c = peer_u.astype(MXU_DTYPE)
    v_c = peer_v.astype(MXU_DTYPE)

    for l in range(depth):
        last = l == depth - 1
        skip = 1 if last else 0
        wq = mla_w_uq[l].reshape(q_rank, MLA_HEADS, MLA_NOPE + MLA_ROPE)
        wq = jnp.concatenate([wq[..., :MLA_NOPE], _rope_lane_layout(wq[..., MLA_NOPE:])], axis=-1)
        wq = wq.reshape(q_rank, MLA_HEADS * MLA_HEAD_PAD).astype(MXU_DTYPE)
        wkv = mla_w_ukv[l].reshape(kv_rank, MLA_HEADS, MLA_NOPE + MLA_V)
        wk = wkv[..., :MLA_NOPE].reshape(kv_rank, MLA_HEADS * MLA_NOPE).astype(MXU_DTYPE)
        wv = wkv[..., MLA_NOPE:].reshape(kv_rank, MLA_HEADS * MLA_V).astype(MXU_DTYPE)
        pair_table = _na_pair_table(na_rpb[l], na_pairs, dc_hot, col_ok)

        mod = _ada_table(cmat, w_ada, b_ada3, l)
        modtab = mod[:n_batch + 1].reshape((n_batch + 1) * N_MOD, 1, d)

        h = _normmod(xs, norm_mix[l], modtab, 0, skip=0, **kw)
        proj = _matmul(h, w_proj, l, tn=_tile(proj_cols, (2048, 1024, 512, 256)), out_dtype=MXU_DTYPE, w_rows=True,
                       name="in_proj")
        hd = wide // HEAD_DIM
        ya = _na_attention(proj, pair_table, jnp.asarray(case_of_blk), jnp.asarray(na_pair_of), lc=lc,
                           col_q=0, col_k=hd // NA_HEADS_PER_STEP, col_v=2 * hd // NA_HEADS_PER_STEP, **kw)
        qf, kf, vf = _mla_up(proj, mla_q_norm[l], mla_kv_norm[l], wq, wk, wv, cos_t, sin_t,
                             col_cq=3, col_ckv=7 * wide // kv_rank, col_kr=(7 * wide + kv_rank) // LANES)
        yb = _mla_attention(qf, kf, vf, lc=lc, **kw)
        if not last:
            ya = _ctx_attention(proj, proj, proj, ya, nl=nl, lc=lc, n_batch=n_batch, n_heads=NA_HEADS, dk=HEAD_DIM,
                                dv=HEAD_DIM, col_q=0, col_k=hd, col_v=2 * hd, scale=HEAD_DIM ** -0.5)
            yb = _ctx_attention(qf, kf, vf, yb, nl=nl, lc=lc, n_batch=n_batch, n_heads=MLA_HEADS, dk=MLA_HEAD_PAD,
                                dv=MLA_V, col_q=0, col_k=0, col_v=0, scale=(MLA_NOPE + MLA_ROPE) ** -0.5)
        tn_m = 1024 if d % 1024 == 0 else d
        m = _merge(ya, yb, proj, conv_w, wa_c, wb_c, wc_c, l, d=d, nl=nl, seq=seq, lc=lc, skip=skip,
                   col_gb=4, col_gc=5, col_xin=6, col_gate=col_gate // tn_m)
        xs = _matmul_resid(m, w_out_c, l, xs, modtab, 2, tn=tn_m, tk=d, skip=skip, name="out_proj", **kw)

        h2 = _normmod(xs, norm_ffn[l], modtab, 3, skip=skip, **kw)
        pq = _matmul(h2, w_q_c, l, tn=2048, out_dtype=MXU_DTYPE, skip=skip, name="peer_query")
        keys = peer_sub_keys[l].reshape(PEER_HEADS, 2 * PEER_NKEYS, PEER_KEY_DIM // 2).astype(MXU_DTYPE)
        gates_t, i1_t, i2_t = _peer_topk(pq, keys, skip=skip)
        gmat = _peer_gates(i1_t.T, i2_t.T, gates_t.T, skip=skip)
        hexp = _peer_up(h2, u_c, l, gmat, skip=skip)
        xs = _matmul_resid(hexp, v_c, l, xs, modtab, 5, tn=min(d, 2048), tk=2048, skip=skip, name="peer_down",
                           **kw)

    out = _final_norm(xs, final_norm, nl=nl)
    return out.reshape(n_batch, seq, d)
```

```python
import functools
import math

import numpy as np
import jax
import jax.numpy as jnp
from jax import lax
from jax.experimental import pallas as pl
from jax.experimental.pallas import tpu as pltpu

F32 = jnp.float32
MXU_DTYPE = jnp.bfloat16

GRID_W = 64
EPS = 1e-6
N_MOD = 6
NEG_INF = -1e30
HEAD_DIM = 128
NA_HEADS = 8
NA_WIN_H = 8
NA_WIN_W = 16
MLA_HEADS = 8
MLA_NOPE = 128
MLA_ROPE = 64
MLA_V = 128
ROPE_BASE = 10000.0
SC_TAPS = 3
N_BRANCH = 3
PEER_HEADS = 8
PEER_NKEYS = 128
PEER_KEY_DIM = 256
PEER_TOPK = 16

V7X_VMEM_BYTES = 64 * 1024 * 1024
LANES = 128
SUBLANES = 8

ROW_BLOCK = 512
MLA_HEAD_PAD = 256
NA_Q_ROWS = 8
NA_K_ROWS = 16
NA_K_CHUNK_ROWS = 4
NA_HEADS_PER_STEP = 8
PEER_HEADS_PER_STEP = 4


def _cparams(n_axes, vmem_bytes):
    limit = min(int(vmem_bytes) + (4 << 20), V7X_VMEM_BYTES - (6 << 20))
    return pltpu.CompilerParams(dimension_semantics=("arbitrary",) * n_axes,
                                vmem_limit_bytes=limit)


def _nbytes(shape, dtype):
    return int(np.prod(shape)) * jnp.dtype(dtype).itemsize


def _tile(n, prefs):
    return next(t for t in prefs if n % t == 0)


def _mod_row(row0, nl, seq, n_batch):
    return jnp.where(row0 >= nl, n_batch, row0 // seq)


def _ada_kernel(c_ref, w_ref, b_ref, o_ref):
    c = c_ref[...]
    s = (c * jax.nn.sigmoid(c)).astype(MXU_DTYPE)
    acc = jnp.dot(s, w_ref[...].astype(MXU_DTYPE), preferred_element_type=F32)
    o_ref[...] = acc + b_ref[...]


def _ada_table(cmat, w_ada, b_ada, layer):
    rows, d = cmat.shape
    n = w_ada.shape[2]
    tn = 512
    vm = 2 * (_nbytes((d, tn), F32) + _nbytes((rows, tn), F32) * 2) + _nbytes((rows, d), F32) * 2 \
        + _nbytes((d, tn), MXU_DTYPE)
    return pl.pallas_call(
        _ada_kernel, name="ada_table",
        out_shape=jax.ShapeDtypeStruct((rows, n), F32),
        grid=(n // tn,),
        in_specs=[pl.BlockSpec((rows, d), lambda j: (0, 0)),
                  pl.BlockSpec((None, d, tn), lambda j: (layer, 0, j)),
                  pl.BlockSpec((None, 1, tn), lambda j: (layer, 0, j))],
        out_specs=pl.BlockSpec((rows, tn), lambda j: (0, j)),
        compiler_params=_cparams(1, vm),
    )(cmat, w_ada, b_ada)


def _normmod_kernel(*refs, n_head):
    g_ref, sh_ref, sc_ref, o_ref = refs[-4:]
    if n_head is None:
        x = refs[0][...]
    else:
        x = jnp.where(pl.program_id(0) < n_head, refs[0][...], refs[1][...])
    y = x * lax.rsqrt(jnp.mean(x * x, axis=-1, keepdims=True) + EPS)
    o_ref[...] = ((y * g_ref[...]) * (1.0 + sc_ref[0]) + sh_ref[0]).astype(o_ref.dtype)


def _split_rows(x, tail, tb):
    n_head = x.shape[0] // tb
    return n_head, (lambda i: jnp.minimum(i, n_head - 1)), (lambda i: jnp.maximum(i - n_head, 0))


def _normmod(x, gain, modtab, m_shift, *, nl, seq, n_batch, skip, tail=None):
    d = x.shape[1]
    tb = 256
    off = skip * (ROW_BLOCK // tb)
    n = x.shape[0] + (0 if tail is None else tail.shape[0])

    def mrow(i, m):
        return (_mod_row(i * tb, nl, seq, n_batch) * N_MOD + m, 0, 0)

    if tail is None:
        n_head, xs_in = None, (x,)
        x_specs = [pl.BlockSpec((tb, d), lambda i: (i, 0))]
    else:
        n_head, xi, ti = _split_rows(x, tail, tb)
        xs_in = (x, tail)
        x_specs = [pl.BlockSpec((tb, d), lambda i: (xi(i), 0)), pl.BlockSpec((tb, d), lambda i: (ti(i), 0))]
    vm = 2 * (len(xs_in) * _nbytes((tb, d), F32) + _nbytes((tb, d), MXU_DTYPE)) + 3 * _nbytes((tb, d), F32)
    return pl.pallas_call(
        functools.partial(_normmod_kernel, n_head=n_head), name="normmod",
        out_shape=jax.ShapeDtypeStruct((n, d), MXU_DTYPE),
        grid=(n // tb - off,),
        in_specs=x_specs + [pl.BlockSpec((1, d), lambda i: (0, 0)),
                            pl.BlockSpec((1, 1, d), lambda i: mrow(i, m_shift)),
                            pl.BlockSpec((1, 1, d), lambda i: mrow(i, m_shift + 1))],
        out_specs=pl.BlockSpec((tb, d), lambda i: (i, 0)),
        compiler_params=_cparams(1, vm),
    )(*xs_in, gain.reshape(1, d), modtab, modtab)


def _mm_kernel(x_ref, w_ref, o_ref):
    o_ref[...] = jnp.dot(x_ref[...], w_ref[...], preferred_element_type=F32).astype(o_ref.dtype)


def _mm_nt_kernel(x_ref, w_ref, o_ref):
    o_ref[...] = lax.dot_general(x_ref[...], w_ref[...], (((1,), (1,)), ((), ())),
                                 preferred_element_type=F32).astype(o_ref.dtype)


def _matmul(x, w, layer, *, tn, out_dtype, skip=0, w_rows=False, name="matmul"):
    n, k = x.shape
    nn = w.shape[1] if w_rows else w.shape[2]
    tm = ROW_BLOCK
    vm = 2 * (_nbytes((tm, k), x.dtype) + _nbytes((k, tn), w.dtype) + _nbytes((tm, tn), out_dtype)) \
        + _nbytes((tm, tn), F32)
    w_spec = (pl.BlockSpec((None, tn, k), lambda j, i: (layer, j, 0)) if w_rows
              else pl.BlockSpec((None, k, tn), lambda j, i: (layer, 0, j)))
    return pl.pallas_call(
        _mm_nt_kernel if w_rows else _mm_kernel, name=name,
        out_shape=jax.ShapeDtypeStruct((n, nn), out_dtype),
        grid=(nn // tn, n // tm - skip),
        in_specs=[pl.BlockSpec((tm, k), lambda j, i: (i, 0)), w_spec],
        out_specs=pl.BlockSpec((tm, tn), lambda j, i: (i, j)),
        compiler_params=_cparams(2, vm),
    )(x, w)


def _mm_resid_kernel(x_ref, w_ref, *refs, n_head):
    g_ref, o_ref = refs[-2:]

    @pl.when(pl.program_id(2) == 0)
    def _():
        if n_head is None:
            o_ref[...] = refs[0][...]
        else:
            o_ref[...] = jnp.where(pl.program_id(0) < n_head, refs[0][...], refs[1][...])

    o_ref[...] += g_ref[0] * jnp.dot(x_ref[...], w_ref[...], preferred_element_type=F32)


def _matmul_resid(x, w, layer, resid, modtab, m_gate, *, tn, tk, nl, seq, n_batch, skip, name, resid_tail=None):
    n, kk = x.shape
    nn = w.shape[2]
    tm = ROW_BLOCK

    def mrow(i, j, k):
        return (_mod_row(i * tm, nl, seq, n_batch) * N_MOD + m_gate, 0, j)

    if resid_tail is None:
        n_head, resids = None, (resid,)
        resid_specs = [pl.BlockSpec((tm, tn), lambda i, j, k: (i, j))]
    else:
        n_head, ri, ti = _split_rows(resid, resid_tail, tm)
        resids = (resid, resid_tail)
        resid_specs = [pl.BlockSpec((tm, tn), lambda i, j, k: (ri(i), j)),
                       pl.BlockSpec((tm, tn), lambda i, j, k: (ti(i), j))]
    vm = 2 * (_nbytes((tm, tk), x.dtype) + _nbytes((tk, tn), w.dtype) + (1 + len(resids)) * _nbytes((tm, tn), F32)) \
        + 2 * _nbytes((tm, tn), F32)
    return pl.pallas_call(
        functools.partial(_mm_resid_kernel, n_head=n_head), name=name,
        out_shape=jax.ShapeDtypeStruct((n, nn), F32),
        grid=(n // tm - skip, nn // tn, kk // tk),
        in_specs=[pl.BlockSpec((tm, tk), lambda i, j, k: (i, k)),
                  pl.BlockSpec((None, tk, tn), lambda i, j, k: (layer, k, j))] + resid_specs
        + [pl.BlockSpec((1, 1, tn), mrow)],
        out_specs=pl.BlockSpec((tm, tn), lambda i, j, k: (i, j)),
        compiler_params=_cparams(3, vm),
    )(x, w, *resids, modtab)


def _na_bias_tables(rows):
    assert 2 * GRID_W == LANES and NA_K_ROWS % 2 == 0
    n_dr = 2 * NA_WIN_H - 1
    n_blk = rows // NA_Q_ROWS
    rl = np.arange(NA_Q_ROWS)[:, None]
    krl = np.arange(NA_K_ROWS)[None, :]
    cases, case_of_blk = [], []
    for i in range(n_blk):
        r0 = i * NA_Q_ROWS
        k0 = int(np.clip(r0 - NA_K_CHUNK_ROWS, 0, rows - NA_K_ROWS))
        r = r0 + rl
        kr = k0 + krl
        start = np.clip(r - NA_WIN_H // 2, 0, rows - NA_WIN_H)
        ok = (kr >= start) & (kr < start + NA_WIN_H)
        dr = np.where(ok, kr - r + NA_WIN_H - 1, n_dr).astype(np.int32)
        assert dr.min() >= 0
        for ci, known in enumerate(cases):
            if np.array_equal(known, dr):
                case_of_blk.append(ci)
                break
        else:
            case_of_blk.append(len(cases))
            cases.append(dr)
    c = np.arange(GRID_W)[:, None]
    kc = np.arange(GRID_W)[None, :]
    ws = np.clip(c - NA_WIN_W // 2, 0, GRID_W - NA_WIN_W)
    col_ok = (kc >= ws) & (kc < ws + NA_WIN_W)
    dc = np.clip(kc - c + NA_WIN_W - 1, 0, 2 * NA_WIN_W - 2)
    dc_hot = (dc[None] == np.arange(2 * NA_WIN_W - 1)[:, None, None]).astype(np.float32)
    dr_of = np.stack(cases).reshape(len(cases), NA_Q_ROWS, NA_K_ROWS // 2, 2)
    pairs, pair_of = np.unique(dr_of.reshape(-1, 2), axis=0, return_inverse=True)
    return (pairs.astype(np.int32), pair_of.reshape(-1).astype(np.int32), np.asarray(case_of_blk, np.int32),
            dc_hot, col_ok)


def _na_pair_table(rpb, pairs, dc_hot, col_ok):
    n_h = rpb.shape[0]
    tb = jnp.einsum('hdk,kcq->hdcq', rpb.astype(F32), dc_hot, precision=lax.Precision.HIGHEST)
    tb = jnp.where(col_ok, tb, NEG_INF)
    tb = jnp.concatenate([tb, jnp.full((n_h, 1, GRID_W, GRID_W), NEG_INF, F32)], axis=1)
    return jnp.concatenate([tb[:, pairs[:, 0]], tb[:, pairs[:, 1]]], axis=-1)


def _na_kernel(case_ref, pair_ref, q_ref, k0_ref, k1_ref, k2_ref, k3_ref, v0_ref, v1_ref, v2_ref, v3_ref,
               kc_ref, vc_ref, b_ref, o_ref):
    scale = HEAD_DIM ** -0.5
    nt = (((1,), (1,)), ((), ()))
    n_pair = NA_K_ROWS // 2
    base = case_ref[pl.program_id(2)] * (NA_Q_ROWS * n_pair)
    for hh in range(NA_HEADS_PER_STEP):
        cs = slice(hh * HEAD_DIM, (hh + 1) * HEAD_DIM)
        q = q_ref[:, cs]
        k_loc = jnp.concatenate([k0_ref[:, cs], k1_ref[:, cs], k2_ref[:, cs], k3_ref[:, cs]], axis=0)
        v_loc = jnp.concatenate([v0_ref[:, cs], v1_ref[:, cs], v2_ref[:, cs], v3_ref[:, cs]], axis=0)
        s_loc = lax.dot_general(q, k_loc, nt, preferred_element_type=F32) * scale
        biased = []
        for rl in range(NA_Q_ROWS):
            blocks = [b_ref[hh, pair_ref[base + rl * n_pair + p]] for p in range(n_pair)]
            biased.append(s_loc[rl * GRID_W:(rl + 1) * GRID_W, :] + jnp.concatenate(blocks, axis=1))
        s_loc = jnp.concatenate(biased, axis=0)
        s_ctx = lax.dot_general(q, kc_ref[:, cs], nt, preferred_element_type=F32) * scale
        m = jnp.maximum(jnp.max(s_loc, axis=-1, keepdims=True), jnp.max(s_ctx, axis=-1, keepdims=True))
        p_loc = jnp.exp(s_loc - m)
        p_ctx = jnp.exp(s_ctx - m)
        denom = jnp.sum(p_loc, axis=-1, keepdims=True) + jnp.sum(p_ctx, axis=-1, keepdims=True)
        o = (jnp.dot(p_loc.astype(MXU_DTYPE), v_loc, preferred_element_type=F32)
             + jnp.dot(p_ctx.astype(MXU_DTYPE), vc_ref[:, cs], preferred_element_type=F32))
        o_ref[:, cs] = (o / denom).astype(o_ref.dtype)


def _na_attention(proj, pair_table, case_of_blk, pair_of, *, nl, seq, lc, n_batch, col_q, col_k, col_v):
    n = proj.shape[0]
    rows = seq // GRID_W
    tq = NA_Q_ROWS * GRID_W
    tkc = NA_K_CHUNK_ROWS * GRID_W
    n_blk = rows // NA_Q_ROWS
    n_chunk = NA_K_ROWS // NA_K_CHUNK_ROWS
    max_c0 = rows // NA_K_CHUNK_ROWS - n_chunk
    hw = NA_HEADS_PER_STEP * HEAD_DIM

    def qmap(h, b, i, case, pair):
        return (b * (seq // tq) + i, col_q + h)

    def kvmap(col, j):
        def f(h, b, i, case, pair):
            c0 = jnp.clip(i * (NA_Q_ROWS // NA_K_CHUNK_ROWS) - 1, 0, max_c0)
            return (b * (seq // tkc) + c0 + j, col + h)
        return f

    def ctxmap(col):
        return lambda h, b, i, case, pair: (nl // lc + b, col + h)

    in_specs = [pl.BlockSpec((tq, hw), qmap)]
    in_specs += [pl.BlockSpec((tkc, hw), kvmap(col_k, j)) for j in range(n_chunk)]
    in_specs += [pl.BlockSpec((tkc, hw), kvmap(col_v, j)) for j in range(n_chunk)]
    in_specs += [pl.BlockSpec((lc, hw), ctxmap(col_k)), pl.BlockSpec((lc, hw), ctxmap(col_v))]
    in_specs += [pl.BlockSpec((NA_HEADS_PER_STEP,) + pair_table.shape[1:],
                              lambda h, b, i, case, pair: (h, 0, 0, 0))]
    vm = 2 * (NA_HEADS_PER_STEP * _nbytes(pair_table.shape[1:], F32) + 12 * _nbytes((tq, hw), MXU_DTYPE)) \
        + 7 * NA_HEADS_PER_STEP * _nbytes((tq, NA_K_ROWS * GRID_W + lc), F32)
    grid_spec = pltpu.PrefetchScalarGridSpec(
        num_scalar_prefetch=2, grid=(NA_HEADS // NA_HEADS_PER_STEP, n_batch, n_blk),
        in_specs=in_specs,
        out_specs=pl.BlockSpec((tq, hw), lambda h, b, i, case, pair: (b * (seq // tq) + i, h)))
    return pl.pallas_call(
        _na_kernel, name="na_attention",
        out_shape=jax.ShapeDtypeStruct((n, NA_HEADS * HEAD_DIM), MXU_DTYPE),
        grid_spec=grid_spec,
        compiler_params=_cparams(3, vm),
    )(case_of_blk, pair_of, *([proj] * (1 + 2 * n_chunk + 2)), pair_table)


def _ctx_attn_kernel(q_ref, k_ref, v_ref, prev_ref, o_ref, *, scale):
    del prev_ref
    s = lax.dot_general(q_ref[...], k_ref[...], (((1,), (1,)), ((), ())),
                        preferred_element_type=F32) * scale
    m = jnp.max(s, axis=-1, keepdims=True)
    p = jnp.exp(s - m)
    o = jnp.dot(p.astype(MXU_DTYPE), v_ref[...], preferred_element_type=F32)
    o_ref[...] = (o / jnp.sum(p, axis=-1, keepdims=True)).astype(o_ref.dtype)


def _ctx_attention(qa, ka, va, y, *, nl, lc, n_batch, n_heads, dk, dv, col_q, col_k, col_v, scale):
    vm = 2 * (2 * _nbytes((lc, dk), MXU_DTYPE) + 2 * _nbytes((lc, dv), MXU_DTYPE)) + 4 * _nbytes((lc, lc), F32)
    return pl.pallas_call(
        functools.partial(_ctx_attn_kernel, scale=scale), name="ctx_attention",
        out_shape=jax.ShapeDtypeStruct(y.shape, y.dtype),
        grid=(n_batch, n_heads),
        in_specs=[pl.BlockSpec((lc, dk), lambda b, h: (nl // lc + b, col_q + h)),
                  pl.BlockSpec((lc, dk), lambda b, h: (nl // lc + b, col_k + h)),
                  pl.BlockSpec((lc, dv), lambda b, h: (nl // lc + b, col_v + h)),
                  pl.BlockSpec(memory_space=pl.ANY)],
        out_specs=pl.BlockSpec((lc, dv), lambda b, h: (nl // lc + b, h)),
        input_output_aliases={3: 0},
        compiler_params=_cparams(2, vm),
    )(qa, ka, va, y)


def _mla_up_kernel(cq_ref, ckv_ref, kr_ref, qn_ref, kvn_ref, wq_ref, wk_ref, wv_ref, cos_ref, sin_ref,
                   q_ref, k_ref, v_ref):
    def rms(x, g):
        return (x * lax.rsqrt(jnp.mean(x * x, axis=-1, keepdims=True) + EPS) * g).astype(MXU_DTYPE)

    cos = cos_ref[...]
    sin = sin_ref[...]

    def rope(x):
        return x * cos + pltpu.roll(x, 2 * (MLA_ROPE // 2), axis=1) * sin

    cqn = rms(cq_ref[...].astype(F32), qn_ref[...])
    kvn = rms(ckv_ref[...].astype(F32), kvn_ref[...])
    q = jnp.dot(cqn, wq_ref[...], preferred_element_type=F32)
    kn = jnp.dot(kvn, wk_ref[...], preferred_element_type=F32)
    v_ref[...] = jnp.dot(kvn, wv_ref[...], preferred_element_type=F32).astype(v_ref.dtype)
    kr = rope(kr_ref[...].astype(F32)).astype(k_ref.dtype)
    for h in range(MLA_HEADS):
        c0 = h * MLA_HEAD_PAD
        q_ref[:, c0:c0 + MLA_NOPE] = q[:, c0:c0 + MLA_NOPE].astype(q_ref.dtype)
        q_ref[:, c0 + MLA_NOPE:c0 + MLA_HEAD_PAD] = rope(q[:, c0 + MLA_NOPE:c0 + MLA_HEAD_PAD]).astype(q_ref.dtype)
        k_ref[:, c0:c0 + MLA_NOPE] = kn[:, h * MLA_NOPE:(h + 1) * MLA_NOPE].astype(k_ref.dtype)
        k_ref[:, c0 + MLA_NOPE:c0 + MLA_HEAD_PAD] = kr


def _mla_up(proj, q_norm, kv_norm, wq, wk, wv, cos_t, sin_t, *, col_cq, col_ckv, col_kr):
    n = proj.shape[0]
    tm = 256
    rq, rkv = wq.shape[0], wk.shape[0]
    wq_cols = wq.shape[1]
    vm = 2 * (_nbytes(wq.shape, MXU_DTYPE) + _nbytes(wk.shape, MXU_DTYPE) + _nbytes(wv.shape, MXU_DTYPE)
              + _nbytes((tm, rq + rkv + LANES), MXU_DTYPE) + 2 * _nbytes((tm, LANES), F32)
              + _nbytes((tm, 2 * wq_cols + wv.shape[1]), MXU_DTYPE)) + 3 * _nbytes((tm, wq_cols), F32)
    return pl.pallas_call(
        _mla_up_kernel, name="mla_up",
        out_shape=(jax.ShapeDtypeStruct((n, wq_cols), MXU_DTYPE),
                   jax.ShapeDtypeStruct((n, wq_cols), MXU_DTYPE),
                   jax.ShapeDtypeStruct((n, wv.shape[1]), MXU_DTYPE)),
        grid=(n // tm,),
        in_specs=[pl.BlockSpec((tm, rq), lambda i: (i, col_cq)),
                  pl.BlockSpec((tm, rkv), lambda i: (i, col_ckv)),
                  pl.BlockSpec((tm, LANES), lambda i: (i, col_kr)),
                  pl.BlockSpec((1, rq), lambda i: (0, 0)),
                  pl.BlockSpec((1, rkv), lambda i: (0, 0)),
                  pl.BlockSpec(wq.shape, lambda i: (0, 0)),
                  pl.BlockSpec(wk.shape, lambda i: (0, 0)),
                  pl.BlockSpec(wv.shape, lambda i: (0, 0)),
                  pl.BlockSpec((tm, LANES), lambda i: (i, 0)),
                  pl.BlockSpec((tm, LANES), lambda i: (i, 0))],
        out_specs=(pl.BlockSpec((tm, wq_cols), lambda i: (i, 0)),
                   pl.BlockSpec((tm, wq_cols), lambda i: (i, 0)),
                   pl.BlockSpec((tm, wv.shape[1]), lambda i: (i, 0))),
        compiler_params=_cparams(1, vm),
    )(proj, proj, proj, q_norm.reshape(1, rq), kv_norm.reshape(1, rkv), wq, wk, wv, cos_t, sin_t)


def _flash_kernel(q_ref, kc_ref, kl_ref, vc_ref, vl_ref, o_ref, m_ref, l_ref, acc_ref, *, scale, tk):
    nt = (((1,), (1,)), ((), ()))
    q = q_ref[...]
    c = scale * math.log2(math.e)

    def chunk(k, v):
        s = lax.dot_general(q, k, nt, preferred_element_type=F32)
        m_old = m_ref[...]
        m_new = jnp.maximum(m_old, jnp.max(s, axis=-1, keepdims=True) * c)
        alpha = jnp.exp2(m_old - m_new)
        p = jnp.exp2(s * c - jnp.tile(m_new, (1, s.shape[1] // LANES)))
        l_ref[...] = alpha * l_ref[...] + jnp.sum(p, axis=-1, keepdims=True)
        acc_ref[...] = alpha * acc_ref[...] + jnp.dot(p.astype(MXU_DTYPE), v, preferred_element_type=F32)
        m_ref[...] = m_new

    m_ref[...] = jnp.full_like(m_ref, -jnp.inf)
    l_ref[...] = jnp.zeros_like(l_ref)
    acc_ref[...] = jnp.zeros_like(acc_ref)
    chunk(kc_ref[...], vc_ref[...])

    def body(j, carry):
        r0 = pl.multiple_of(j * tk, tk)
        chunk(kl_ref[pl.ds(r0, tk), :], vl_ref[pl.ds(r0, tk), :])
        return carry

    lax.fori_loop(0, kl_ref.shape[0] // tk, body, 0)
    o_ref[...] = (acc_ref[...] / l_ref[...]).astype(o_ref.dtype)


def _mla_attention(qf, kf, vf, *, nl, seq, lc, n_batch):
    n = qf.shape[0]
    tq, tk = _tile(seq, (2048, 1024, 512)), _tile(seq, (1024, 512))
    dk, dv = MLA_HEAD_PAD, MLA_V
    assert dv == LANES
    scale = (MLA_NOPE + MLA_ROPE) ** -0.5
    vm = 2 * (_nbytes((tq, dk), MXU_DTYPE) + _nbytes((lc + seq, dk + dv), MXU_DTYPE) + _nbytes((tq, dv), MXU_DTYPE)) \
        + 6 * _nbytes((tq, tk), F32) + 2 * _nbytes((tq, dv), F32)
    return pl.pallas_call(
        functools.partial(_flash_kernel, scale=scale, tk=tk), name="mla_attention",
        out_shape=jax.ShapeDtypeStruct((n, MLA_HEADS * dv), MXU_DTYPE),
        grid=(n_batch, MLA_HEADS, seq // tq),
        in_specs=[pl.BlockSpec((tq, dk), lambda b, h, i: (b * (seq // tq) + i, h)),
                  pl.BlockSpec((lc, dk), lambda b, h, i: (nl // lc + b, h)),
                  pl.BlockSpec((seq, dk), lambda b, h, i: (b, h)),
                  pl.BlockSpec((lc, dv), lambda b, h, i: (nl // lc + b, h)),
                  pl.BlockSpec((seq, dv), lambda b, h, i: (b, h))],
        out_specs=pl.BlockSpec((tq, dv), lambda b, h, i: (b * (seq // tq) + i, h)),
        scratch_shapes=[pltpu.VMEM((tq, LANES), F32), pltpu.VMEM((tq, LANES), F32), pltpu.VMEM((tq, dv), F32)],
        compiler_params=_cparams(3, vm),
    )(qf, kf, kf, vf, vf)


def _merge_kernel(ya_ref, yb_ref, gb_ref, gc_ref, xin_ref, gcp_ref, xinp_ref, gcn_ref, xinn_ref, cw_ref,
                  wa_ref, wb_ref, wc_ref, g0_ref, g1_ref, g2_ref, o_ref, yc_ref, *, nl, seq, lc):
    i = pl.program_id(0)
    j = pl.program_id(1)
    tm = ya_ref.shape[0]

    @pl.when(j == 0)
    def _():
        u = gc_ref[...].astype(F32) * xin_ref[...].astype(F32)
        u_prev_halo = (gcp_ref[...].astype(F32) * xinp_ref[...].astype(F32))[SUBLANES - 1:SUBLANES, :]
        u_next_halo = (gcn_ref[...].astype(F32) * xinn_ref[...].astype(F32))[0:1, :]
        row = lax.broadcasted_iota(jnp.int32, (tm, 1), 0)
        g = i * tm + row
        pos = jnp.where(g < nl, g % seq, (g - nl) % lc)
        seq_len = jnp.where(g < nl, seq, lc)
        u_dn = jnp.where(row == 0, u_prev_halo, pltpu.roll(u, 1, axis=0))
        u_up = jnp.where(row == tm - 1, u_next_halo, pltpu.roll(u, tm - 1, axis=0))
        u_dn = jnp.where(pos == 0, 0.0, u_dn)
        u_up = jnp.where(pos == seq_len - 1, 0.0, u_up)
        cw = cw_ref[...]
        y = cw[0:1, :] * u_dn + cw[1:2, :] * u + cw[2:3, :] * u_up
        yc_ref[...] = (gb_ref[...].astype(F32) * y).astype(yc_ref.dtype)

    def branch(y, w_ref, g_ref):
        gate = jax.nn.sigmoid(g_ref[...].astype(F32))
        return gate * jnp.dot(y, w_ref[...], preferred_element_type=F32)

    m = (branch(ya_ref[...], wa_ref, g0_ref) + branch(yb_ref[...], wb_ref, g1_ref)
         + branch(yc_ref[...], wc_ref, g2_ref))
    o_ref[...] = m.astype(o_ref.dtype)


def _merge(ya, yb, proj, conv_w, wa, wb, wc, layer, *, d, tn, nl, seq, lc, skip, col_gb, col_gc, col_xin, col_gate):
    n = ya.shape[0]
    tm = ROW_BLOCK
    w = ya.shape[1]
    hb = tm // SUBLANES
    last_halo = n // SUBLANES - 1
    n_col = d // tn

    def row(i, j):
        return (i, 0)

    def cur(col):
        return lambda i, j: (i, col)

    def prev(col):
        return lambda i, j: (jnp.maximum(i * hb - 1, 0), col)

    def nxt(col):
        return lambda i, j: (jnp.minimum((i + 1) * hb, last_halo), col)

    def gate(br):
        return lambda i, j: (i, col_gate + br * n_col + j)

    in_specs = [pl.BlockSpec((tm, w), row), pl.BlockSpec((tm, w), row),
                pl.BlockSpec((tm, w), cur(col_gb)), pl.BlockSpec((tm, w), cur(col_gc)),
                pl.BlockSpec((tm, w), cur(col_xin)),
                pl.BlockSpec((SUBLANES, w), prev(col_gc)), pl.BlockSpec((SUBLANES, w), prev(col_xin)),
                pl.BlockSpec((SUBLANES, w), nxt(col_gc)), pl.BlockSpec((SUBLANES, w), nxt(col_xin)),
                pl.BlockSpec((None, SC_TAPS, w), lambda i, j: (layer, 0, 0)),
                pl.BlockSpec((None, w, tn), lambda i, j: (layer, 0, j)),
                pl.BlockSpec((None, w, tn), lambda i, j: (layer, 0, j)),
                pl.BlockSpec((None, w, tn), lambda i, j: (layer, 0, j)),
                pl.BlockSpec((tm, tn), gate(0)), pl.BlockSpec((tm, tn), gate(1)), pl.BlockSpec((tm, tn), gate(2))]
    vm = 2 * (5 * _nbytes((tm, w), MXU_DTYPE) + 3 * _nbytes((w, tn), MXU_DTYPE) + 4 * _nbytes((tm, tn), MXU_DTYPE)) \
        + _nbytes((tm, w), MXU_DTYPE) + 6 * _nbytes((tm, max(w, tn)), F32)
    return pl.pallas_call(
        functools.partial(_merge_kernel, nl=nl, seq=seq, lc=lc), name="conv_merge",
        out_shape=jax.ShapeDtypeStruct((n, d), MXU_DTYPE),
        grid=(n // tm - skip, n_col),
        in_specs=in_specs,
        out_specs=pl.BlockSpec((tm, tn), lambda i, j: (i, j)),
        scratch_shapes=[pltpu.VMEM((tm, w), MXU_DTYPE)],
        compiler_params=_cparams(2, vm),
    )(ya, yb, proj, proj, proj, proj, proj, proj, proj, conv_w, wa, wb, wc, proj, proj, proj)


def _sort_network(n):
    size = 1
    while size < n:
        size *= 2
    net = []

    def merge(lo, hi, r):
        step = r * 2
        if step < hi - lo:
            merge(lo, hi, step)
            merge(lo + r, hi, step)
            net.extend((i, i + r) for i in range(lo + r, hi - r, step))
        else:
            net.append((lo, lo + r))

    def sort(lo, hi):
        if hi - lo >= 1:
            mid = lo + (hi - lo) // 2
            sort(lo, mid)
            sort(mid + 1, hi)
            merge(lo, hi, 1)

    sort(0, size - 1)
    return [(i, j) for i, j in net if j < n]


def _topk_cols(s, k, ids=None):
    n_tile = s.shape[0] // SUBLANES
    v = [s[t * SUBLANES:(t + 1) * SUBLANES, :] for t in range(n_tile)]
    if ids is None:
        sub = lax.broadcasted_iota(jnp.int32, (SUBLANES, s.shape[1]), 0)
        r = [sub + t * SUBLANES for t in range(n_tile)]
    else:
        r = [ids[t * SUBLANES:(t + 1) * SUBLANES, :] for t in range(n_tile)]
    for a, b in _sort_network(n_tile):
        first = (v[a] > v[b]) | ((v[a] == v[b]) & (r[a] < r[b]))
        v[a], v[b] = jnp.where(first, v[a], v[b]), jnp.where(first, v[b], v[a])
        r[a], r[b] = jnp.where(first, r[a], r[b]), jnp.where(first, r[b], r[a])
    depth = min(n_tile, k)
    v, r = v[:depth], r[:depth]
    big = jnp.iinfo(jnp.int32).max
    vals, idxs = [], []
    for it in range(k):
        m = jnp.max(v[0], axis=0, keepdims=True)
        idx = jnp.min(jnp.where(v[0] == m, r[0], big), axis=0, keepdims=True)
        vals.append(m)
        idxs.append(idx)
        win = r[0] == idx
        live = min(depth, k - it)
        for t in range(live - 1):
            v[t] = jnp.where(win, v[t + 1], v[t])
            r[t] = jnp.where(win, r[t + 1], r[t])
        v[live - 1] = jnp.where(win, -jnp.inf, v[live - 1])
    return jnp.concatenate(vals, axis=0), jnp.concatenate(idxs, axis=0)


def _candidate_rows():
    k = PEER_TOPK
    groups = []
    for b0 in range(0, k, SUBLANES):
        groups.append((0, 1, b0, b0 + SUBLANES))
    a = 1
    while a < k and k // (a + 1) > 1:
        assert k // (a + 1) <= SUBLANES
        groups.append((a, a + 1, 0, SUBLANES))
        a += 1
    while a < k:
        groups.append((a, a + SUBLANES, 0, 1))
        a += SUBLANES
    covered = {(aa, bb) for a0, a1, b0, b1 in groups for aa in range(a0, a1) for bb in range(b0, b1)}
    assert all((aa, bb) in covered for aa in range(k) for bb in range(k) if (aa + 1) * (bb + 1) <= k)
    return groups


def _take_rows(table, idx):
    iota = lax.broadcasted_iota(jnp.int32, table.shape, 0)
    rows = [jnp.sum(jnp.where(iota == idx[j:j + 1, :], table, 0), axis=0, keepdims=True)
            for j in range(idx.shape[0])]
    return jnp.concatenate(rows, axis=0)


def _peer_topk_kernel(q_ref, keys_ref, g_ref, i1_ref, i2_ref):
    nt = (((1,), (1,)), ((), ()))
    half = PEER_KEY_DIM // 2
    sub = lax.broadcasted_iota(jnp.int32, (SUBLANES, q_ref.shape[0]), 0)
    for hh in range(PEER_HEADS_PER_STEP):
        q = q_ref[:, hh * PEER_KEY_DIM:(hh + 1) * PEER_KEY_DIM]
        s1 = lax.dot_general(keys_ref[hh, :PEER_NKEYS, :], q[:, :half], nt, preferred_element_type=F32)
        s2 = lax.dot_general(keys_ref[hh, PEER_NKEYS:, :], q[:, half:], nt, preferred_element_type=F32)
        v1, j1 = _topk_cols(s1, PEER_TOPK)
        v2, j2 = _topk_cols(s2, PEER_TOPK)
        cand, flat = [], []
        for a0, a1, b0, b1 in _candidate_rows():
            cand.append(v1[a0:a1, :] + v2[b0:b1, :])
            flat.append(sub * (1 if a1 - a0 == 1 else PEER_TOPK) + (a0 * PEER_TOPK + b0))
        top_s, pos = _topk_cols(jnp.concatenate(cand, axis=0), PEER_TOPK, jnp.concatenate(flat, axis=0))
        rows = slice(hh * PEER_TOPK, (hh + 1) * PEER_TOPK)
        i1_ref[rows, :] = _take_rows(j1, jnp.right_shift(pos, PEER_TOPK.bit_length() - 1))
        i2_ref[rows, :] = _take_rows(j2, jnp.bitwise_and(pos, PEER_TOPK - 1))
        e = jnp.exp(top_s - top_s[0:1, :])
        g_ref[rows, :] = e / jnp.sum(e, axis=0, keepdims=True)


def _peer_topk(q, keys, *, skip):
    n = q.shape[0]
    t = 256
    off = skip * (ROW_BLOCK // t)
    rows = PEER_HEADS * PEER_TOPK
    hs = PEER_HEADS_PER_STEP
    out_spec = pl.BlockSpec((hs * PEER_TOPK, t), lambda i, h: (h, i))
    vm = 2 * hs * (_nbytes((t, PEER_KEY_DIM), MXU_DTYPE) + _nbytes(keys.shape[1:], MXU_DTYPE)) \
        + 8 * hs * _nbytes((PEER_TOPK * PEER_TOPK, t), F32)
    return pl.pallas_call(
        _peer_topk_kernel, name="peer_topk",
        out_shape=(jax.ShapeDtypeStruct((rows, n), F32), jax.ShapeDtypeStruct((rows, n), jnp.int32),
                   jax.ShapeDtypeStruct((rows, n), jnp.int32)),
        grid=(n // t - off, PEER_HEADS // hs),
        in_specs=[pl.BlockSpec((t, hs * PEER_KEY_DIM), lambda i, h: (i, h)),
                  pl.BlockSpec((hs,) + keys.shape[1:], lambda i, h: (h, 0, 0))],
        out_specs=(out_spec, out_spec, out_spec),
        compiler_params=_cparams(2, vm),
    )(q, keys)


def _peer_gates_kernel(i1_ref, i2_ref, g_ref, o_ref):
    n_sel = i1_ref.shape[1]
    iota = lax.broadcasted_iota(jnp.int32, (PEER_NKEYS, n_sel), 0)
    nt = (((1,), (1,)), ((), ()))

    def body(t, carry):
        i1b = i1_ref[pl.ds(t, 1), :]
        i2b = i2_ref[pl.ds(t, 1), :]
        g = g_ref[pl.ds(t, 1), :]
        g_hi = g.astype(MXU_DTYPE).astype(F32)
        g_lo = g - g_hi
        m1 = iota == i1b
        lhs = jnp.concatenate([jnp.where(m1, g_hi, 0.0), jnp.where(m1, g_lo, 0.0)], axis=1).astype(MXU_DTYPE)
        hot2 = jnp.where(iota == i2b, 1.0, 0.0).astype(MXU_DTYPE)
        rhs = jnp.concatenate([hot2, hot2], axis=1)
        o_ref[t] = lax.dot_general(lhs, rhs, nt, preferred_element_type=F32)
        return carry

    lax.fori_loop(0, i1_ref.shape[0], body, 0, unroll=64)


def _peer_gates(i1, i2, g, *, skip):
    n, n_sel = i1.shape
    t = 128
    off = skip * (ROW_BLOCK // t)
    spec = pl.BlockSpec((t, n_sel), lambda i: (i, 0))
    vm = 2 * (3 * _nbytes((t, n_sel), F32) + _nbytes((t, PEER_NKEYS, PEER_NKEYS), F32))
    return pl.pallas_call(
        _peer_gates_kernel, name="peer_gates",
        out_shape=jax.ShapeDtypeStruct((n, PEER_NKEYS, PEER_NKEYS), F32),
        grid=(n // t - off,),
        in_specs=[spec, spec, spec],
        out_specs=pl.BlockSpec((t, PEER_NKEYS, PEER_NKEYS), lambda i: (i, 0, 0)),
        compiler_params=_cparams(1, vm),
    )(i1, i2, g)


def _peer_up_kernel(h_ref, u_ref, g_ref, o_ref):
    acc = lax.dot_general(h_ref[...], u_ref[...], (((1,), (1,)), ((), ())), preferred_element_type=F32)
    for jj in range(g_ref.shape[1]):
        a = acc[:, jj * PEER_NKEYS:(jj + 1) * PEER_NKEYS]
        act = 0.5 * a * (1.0 + lax.erf(a * (2.0 ** -0.5)))
        o_ref[:, jj * PEER_NKEYS:(jj + 1) * PEER_NKEYS] = (act * g_ref[:, jj, :]).astype(o_ref.dtype)


def _peer_up(h, u, layer, gmat, *, skip):
    n, d = h.shape
    e = u.shape[1]
    tm = ROW_BLOCK
    ti = SUBLANES
    tn = ti * PEER_NKEYS
    vm = 2 * (_nbytes((tm, d), MXU_DTYPE) + _nbytes((d, tn), MXU_DTYPE) + _nbytes((tm, ti, PEER_NKEYS), F32)
              + _nbytes((tm, tn), MXU_DTYPE)) + 3 * _nbytes((tm, tn), F32)
    return pl.pallas_call(
        _peer_up_kernel, name="peer_up",
        out_shape=jax.ShapeDtypeStruct((n, e), MXU_DTYPE),
        grid=(n // tm - skip, e // tn),
        in_specs=[pl.BlockSpec((tm, d), lambda i, j: (i, 0)),
                  pl.BlockSpec((None, tn, d), lambda i, j: (layer, j, 0)),
                  pl.BlockSpec((tm, ti, PEER_NKEYS), lambda i, j: (i, j, 0))],
        out_specs=pl.BlockSpec((tm, tn), lambda i, j: (i, j)),
        compiler_params=_cparams(2, vm),
    )(h, u, gmat)


def _final_norm_kernel(x_ref, g_ref, o_ref):
    x = x_ref[...]
    o_ref[...] = x * lax.rsqrt(jnp.mean(x * x, axis=-1, keepdims=True) + EPS) * g_ref[...]


def _final_norm(x, gain, *, nl):
    d = x.shape[1]
    tb = 256
    vm = 4 * _nbytes((tb, d), F32) + 2 * _nbytes((tb, d), F32)
    return pl.pallas_call(
        _final_norm_kernel, name="final_norm",
        out_shape=jax.ShapeDtypeStruct((nl, d), F32),
        grid=(nl // tb,),
        in_specs=[pl.BlockSpec((tb, d), lambda i: (i, 0)), pl.BlockSpec((1, d), lambda i: (0, 0))],
        out_specs=pl.BlockSpec((tb, d), lambda i: (i, 0)),
        compiler_params=_cparams(1, vm),
    )(x, gain.reshape(1, d))


def _regroup_kernel(w_ref, o_ref, *, pad_from, pad_to):
    half = MLA_ROPE // 2
    r0 = pl.program_id(1) * o_ref.shape[0]
    in_pad = (r0 >= pad_from) & (r0 < pad_to)

    @pl.when(jnp.logical_not(in_pad))
    def _():
        o_ref[...] = w_ref[0].astype(o_ref.dtype)

    @pl.when(in_pad)
    def _():
        o_ref[...] = jnp.zeros_like(o_ref)

    @pl.when(r0 == pad_from)
    def _():
        o_ref[0:half, :] = w_ref[0, 0:half, :].astype(o_ref.dtype)
        o_ref[LANES // 2:LANES // 2 + half, :] = w_ref[0, half:2 * half, :].astype(o_ref.dtype)


def _regroup_in_proj(w_in_t, *, wide, kv_rank, d):
    n_layer, cols, _ = w_in_t.shape
    src_ckv = 4 * wide
    src_rope = src_ckv + kv_rank
    src_conv = src_rope + MLA_ROPE
    src_gate = src_conv + 3 * wide
    assert src_gate + N_BRANCH * d == cols
    dst_ckv = 7 * wide
    dst_rope = dst_ckv + kv_rank
    dst_gate = 8 * wide
    out_rows = dst_gate + N_BRANCH * d
    tr = 256
    assert dst_rope % tr == 0 and kv_rank % tr == 0 and (N_BRANCH * d) % tr == 0 and tr >= LANES

    def src_row(l, i):
        r = i * tr
        src = jnp.where(r < 4 * wide, r,
                        jnp.where(r < dst_ckv, r - 4 * wide + src_conv,
                                  jnp.where(r < dst_rope, r - dst_ckv + src_ckv,
                                            jnp.where(r < dst_gate, src_rope, r - dst_gate + src_gate))))
        return (l, pl.multiple_of(src, MLA_ROPE), 0)

    vm = 2 * (_nbytes((tr, d), F32) + _nbytes((tr, d), MXU_DTYPE)) + _nbytes((tr, d), F32)
    return pl.pallas_call(
        functools.partial(_regroup_kernel, pad_from=dst_rope, pad_to=dst_gate),
        name="regroup_in_proj",
        out_shape=jax.ShapeDtypeStruct((n_layer, out_rows, d), MXU_DTYPE),
        grid=(n_layer, out_rows // tr),
        in_specs=[pl.BlockSpec((pl.Element(1), pl.Element(tr), pl.Element(d)), src_row)],
        out_specs=pl.BlockSpec((None, tr, d), lambda l, i: (l, i, 0)),
        compiler_params=_cparams(2, vm),
    )(w_in_t)


def _rope_lane_layout(w):
    half = MLA_ROPE // 2
    z = jnp.zeros(w.shape[:-1] + (LANES // 2 - half,), w.dtype)
    return jnp.concatenate([w[..., :half], z, w[..., half:], z], axis=-1)


def _rope_tables(seq, nc):
    pos = jnp.arange(seq, dtype=jnp.int32)
    n_freq = MLA_ROPE // 4
    inv_freq = jnp.power(ROPE_BASE, -jnp.arange(n_freq, dtype=F32) / n_freq)
    row = (pos // GRID_W).astype(F32)
    col = (pos % GRID_W).astype(F32)
    ang = jnp.concatenate([row[:, None] * inv_freq, col[:, None] * inv_freq], axis=-1)
    cos, sin = jnp.cos(ang), jnp.sin(ang)
    cos_l = _rope_lane_layout(jnp.concatenate([cos, cos], axis=-1))
    sin_l = _rope_lane_layout(jnp.concatenate([-sin, sin], axis=-1))
    one = _rope_lane_layout(jnp.ones((nc, MLA_ROPE), F32))
    return one, jnp.zeros((nc, LANES), F32), cos_l, sin_l


def kernel(x, c, ctx, c_ctx, w_ada, b_ada, norm_mix, norm_ffn, w_in, na_rpb, mla_q_norm, mla_kv_norm, mla_w_uq, mla_w_ukv, conv_w, w_branch_a, w_branch_b, w_branch_c, w_out, peer_w_q, peer_sub_keys, peer_u, peer_v, final_norm):
    n_batch, seq, d = x.shape
    lc = ctx.shape[1]
    depth = w_ada.shape[0]
    nc = n_batch * lc
    nl = n_batch * seq
    na_w = NA_HEADS * HEAD_DIM
    q_rank, kv_rank = mla_w_uq.shape[1], mla_w_ukv.shape[1]
    sc_w = conv_w.shape[2]
    rows = seq // GRID_W
    assert nc == ROW_BLOCK and seq % ROW_BLOCK == 0 and d % LANES == 0
    assert rows % NA_Q_ROWS == 0 and rows >= NA_K_ROWS
    assert na_w == q_rank == sc_w == MLA_HEADS * MLA_V and kv_rank * 2 == na_w
    kw = dict(nl=nl, seq=seq, n_batch=n_batch)

    wide = na_w
    col_gate = 8 * wide
    proj_cols = col_gate + N_BRANCH * d

    cmat = jnp.zeros((SUBLANES, d), F32).at[:n_batch].set(c).at[n_batch].set(c_ctx)
    xs, xs_tail = x.reshape(nl, d), ctx.reshape(nc, d)

    na_pairs, na_pair_of, case_of_blk, dc_hot, col_ok = _na_bias_tables(rows)
    one_c, zero_c, cos_l, sin_l = _rope_tables(seq, nc)
    cos_t = jnp.concatenate([cos_l] * n_batch + [one_c], axis=0)
    sin_t = jnp.concatenate([sin_l] * n_batch + [zero_c], axis=0)

    w_proj = _regroup_in_proj(jnp.swapaxes(w_in, 1, 2), wide=wide, kv_rank=kv_rank, d=d)
    b_ada3 = b_ada.reshape(depth, 1, N_MOD * d)
    wa_c, wb_c, wc_c = (w.astype(MXU_DTYPE) for w in (w_branch_a, w_branch_b, w_branch_c))
    w_out_c = w_out.astype(MXU_DTYPE)
    w_q_c = peer_w_q.astype(MXU_DTYPE)
    u_c = peer_u.astype(MXU_DTYPE)
    v_c = peer_v.astype(MXU_DTYPE)

    for l in range(depth):
        last = l == depth - 1
        skip = 1 if last else 0
        wq = mla_w_uq[l].reshape(q_rank, MLA_HEADS, MLA_NOPE + MLA_ROPE)
        wq = jnp.concatenate([wq[..., :MLA_NOPE], _rope_lane_layout(wq[..., MLA_NOPE:])], axis=-1)
        wq = wq.reshape(q_rank, MLA_HEADS * MLA_HEAD_PAD).astype(MXU_DTYPE)
        wkv = mla_w_ukv[l].reshape(kv_rank, MLA_HEADS, MLA_NOPE + MLA_V)
        wk = wkv[..., :MLA_NOPE].reshape(kv_rank, MLA_HEADS * MLA_NOPE).astype(MXU_DTYPE)
        wv = wkv[..., MLA_NOPE:].reshape(kv_rank, MLA_HEADS * MLA_V).astype(MXU_DTYPE)
        pair_table = _na_pair_table(na_rpb[l], na_pairs, dc_hot, col_ok)

        mod = _ada_table(cmat, w_ada, b_ada3, l)
        modtab = mod[:n_batch + 1].reshape((n_batch + 1) * N_MOD, 1, d)

        h = _normmod(xs, norm_mix[l], modtab, 0, skip=0, tail=xs_tail, **kw)
        proj = _matmul(h, w_proj, l, tn=_tile(proj_cols, (2048, 1024, 512, 256)), out_dtype=MXU_DTYPE, w_rows=True,
                       name="in_proj")
        hd = wide // HEAD_DIM
        ya = _na_attention(proj, pair_table, jnp.asarray(case_of_blk), jnp.asarray(na_pair_of), lc=lc,
                           col_q=0, col_k=hd // NA_HEADS_PER_STEP, col_v=2 * hd // NA_HEADS_PER_STEP, **kw)
        qf, kf, vf = _mla_up(proj, mla_q_norm[l], mla_kv_norm[l], wq, wk, wv, cos_t, sin_t,
                             col_cq=3, col_ckv=7 * wide // kv_rank, col_kr=(7 * wide + kv_rank) // LANES)
        yb = _mla_attention(qf, kf, vf, lc=lc, **kw)
        if not last:
            ya = _ctx_attention(proj, proj, proj, ya, nl=nl, lc=lc, n_batch=n_batch, n_heads=NA_HEADS, dk=HEAD_DIM,
                                dv=HEAD_DIM, col_q=0, col_k=hd, col_v=2 * hd, scale=HEAD_DIM ** -0.5)
            yb = _ctx_attention(qf, kf, vf, yb, nl=nl, lc=lc, n_batch=n_batch, n_heads=MLA_HEADS, dk=MLA_HEAD_PAD,
                                dv=MLA_V, col_q=0, col_k=0, col_v=0, scale=(MLA_NOPE + MLA_ROPE) ** -0.5)
        tn_m = 1024 if d % 1024 == 0 else d
        m = _merge(ya, yb, proj, conv_w, wa_c, wb_c, wc_c, l, d=d, tn=tn_m, nl=nl, seq=seq, lc=lc, skip=skip,
                   col_gb=4, col_gc=5, col_xin=6, col_gate=col_gate // tn_m)
        xs = _matmul_resid(m, w_out_c, l, xs, modtab, 2, tn=tn_m, tk=d, skip=skip, name="out_proj",
                           resid_tail=xs_tail, **kw)
        xs_tail = None

        h2 = _normmod(xs, norm_ffn[l], modtab, 3, skip=skip, **kw)
        pq = _matmul(h2, w_q_c, l, tn=2048, out_dtype=MXU_DTYPE, skip=skip, name="peer_query")
        keys = peer_sub_keys[l].reshape(PEER_HEADS, 2 * PEER_NKEYS, PEER_KEY_DIM // 2).astype(MXU_DTYPE)
        gates_t, i1_t, i2_t = _peer_topk(pq, keys, skip=skip)
        gmat = _peer_gates(i1_t.T, i2_t.T, gates_t.T, skip=skip)
        hexp = _peer_up(h2, u_c, l, gmat, skip=skip)
        xs = _matmul_resid(hexp, v_c, l, xs, modtab, 5, tn=min(d, 2048), tk=2048, skip=skip, name="peer_down",
                           **kw)

    out = _final_norm(xs, final_norm, nl=nl)
    return out.reshape(n_batch, seq, d)
```
